```python
import jax, jax.numpy as jnp
from jax import lax
import numpy as np

D_MODEL = 2048
BATCH = 1
SEQ = 8192
DEPTH = 1
DEC_BATCH = 32
DEC_SEQ = 8
PAST_LEN = 16384
PAGE_SIZE = 128

HEAD_DIM_A = 128
N_HEADS_A = (D_MODEL // 2) // HEAD_DIM_A
N_KV_A = N_HEADS_A // 2
ROPE_DIM_A = HEAD_DIM_A // 4
ROPE_THETA = 500000.0
IDX_HEADS = 16
IDX_DIM = 64
IDX_ROPE_DIM = IDX_DIM // 4
TOPK_MAX = 256
Q_BLOCK = 128
V_DIM_B = 128
N_HEADS_B = (D_MODEL // 2) // V_DIM_B
QK_DIM_B = V_DIM_B // 2
RET_CHUNK = 128
RET_THETA = 10000.0
WIDTH_A = N_HEADS_A * HEAD_DIM_A
WIDTH_B = N_HEADS_B * V_DIM_B
MIX_WIDTH = WIDTH_A + WIDTH_B
PROJ_SIZES = (WIDTH_A, N_KV_A * HEAD_DIM_A, N_KV_A * HEAD_DIM_A, IDX_HEADS * IDX_DIM, IDX_DIM, IDX_HEADS,
              N_HEADS_B * QK_DIM_B, N_HEADS_B * QK_DIM_B, WIDTH_B, WIDTH_B)
PROJ_WIDTH = sum(PROJ_SIZES)
N_EXPERTS = 32
TOP_K = 4
D_FF = D_MODEL
SWIGLU_LIMIT = 7.0
SWIGLU_ALPHA = 1.702
LN_EPS = 1e-5
DN_ALPHA = (2 * DEPTH) ** 0.25
DN_BETA = (8 * DEPTH) ** -0.25

kernel_name = 'hymba_dsa_retention_moe_step'

f32 = jnp.float32


def _rope(x, pos, rot_dim, theta):
    half = rot_dim // 2
    inv_freq = 1.0 / (theta ** (jnp.arange(half, dtype=f32) / half))
    ang = pos.astype(f32)[:, None] * inv_freq[None, :]
    cos = jnp.cos(ang)[:, None, :]
    sin = jnp.sin(ang)[:, None, :]
    xf = x.astype(f32)
    x1 = xf[..., :half]
    x2 = xf[..., half:rot_dim]
    out = jnp.concatenate([x1 * cos - x2 * sin, x2 * cos + x1 * sin, xf[..., rot_dim:]], axis=-1)
    return out.astype(x.dtype)


def _layernorm(x, g, b):
    xf = x.astype(f32)
    mu = xf.mean(-1, keepdims=True)
    var = jnp.mean(jnp.square(xf - mu), -1, keepdims=True)
    return ((xf - mu) * lax.rsqrt(var + LN_EPS) * g.astype(f32) + b.astype(f32)).astype(x.dtype)


def _project(x, pos, w_in):
    B, T, _ = x.shape
    cuts = np.cumsum(PROJ_SIZES)[:-1].tolist()
    qa, ka, va, iq, ik, iw, qb, kb, vb, gb = jnp.split(x @ w_in, cuts, axis=-1)
    qa = _rope(qa.reshape(B, T, N_HEADS_A, HEAD_DIM_A), pos, ROPE_DIM_A, ROPE_THETA)
    ka = _rope(ka.reshape(B, T, N_KV_A, HEAD_DIM_A), pos, ROPE_DIM_A, ROPE_THETA)
    va = va.reshape(B, T, N_KV_A, HEAD_DIM_A)
    iq = _rope(iq.reshape(B, T, IDX_HEADS, IDX_DIM), pos, IDX_ROPE_DIM, ROPE_THETA)
    ik = _rope(ik[:, :, None, :], pos, IDX_ROPE_DIM, ROPE_THETA)[:, :, 0, :]
    qb = _rope(qb.reshape(B, T, N_HEADS_B, QK_DIM_B), pos, QK_DIM_B, RET_THETA)
    kb = _rope(kb.reshape(B, T, N_HEADS_B, QK_DIM_B), pos, QK_DIM_B, RET_THETA) * QK_DIM_B ** -0.5
    vb = vb.reshape(B, T, N_HEADS_B, V_DIM_B)
    return qa, ka, va, iq, ik, iw, qb, kb, vb, gb


def _index_select(iq, iw, ik, qpos, kpos, topk):
    logits = jnp.einsum('bqhd,bsd->bqhs', iq.astype(f32), ik.astype(f32)) * IDX_DIM ** -0.5
    score = jnp.einsum('bqhs,bqh->bqs', jax.nn.relu(logits), iw.astype(f32) * IDX_HEADS ** -0.5)
    causal = kpos[None, :] <= qpos[:, None]
    score = jnp.where(causal[None], score, -jnp.inf)
    val, idx = lax.top_k(score, topk)
    return idx, jnp.isfinite(val)


def _attend(q, ksel, vsel, valid):
    B, Tq, H, Dh = q.shape
    qg = q.reshape(B, Tq, N_KV_A, H // N_KV_A, Dh).astype(f32)
    s = jnp.einsum('bqgrd,bqkgd->bqgrk', qg, ksel.astype(f32)) * Dh ** -0.5
    s = jnp.where(valid[:, :, None, None, :], s, -jnp.inf)
    p = jax.nn.softmax(s, axis=-1)
    o = jnp.einsum('bqgrk,bqkgd->bqgrd', p, vsel.astype(f32))
    return o.reshape(B, Tq, H * Dh).astype(q.dtype)


def _dsa_prompt(qa, ka, va, iq, ik, iw):
    B, T = qa.shape[:2]
    topk = min(TOPK_MAX, T // 4)
    kpos = jnp.arange(T)
    gather = jax.vmap(lambda rows, ids: rows[ids])

    def block(i):
        s = i * Q_BLOCK
        sl = lambda a: lax.dynamic_slice_in_dim(a, s, Q_BLOCK, axis=1)
        qpos = s + jnp.arange(Q_BLOCK)
        idx, valid = _index_select(sl(iq), sl(iw), ik, qpos, kpos, topk)
        return _attend(sl(qa), gather(ka, idx), gather(va, idx), valid)

    out = lax.map(block, jnp.arange(T // Q_BLOCK))
    return out.transpose(1, 0, 2, 3).reshape(B, T, WIDTH_A)


def _dsa_sample(qa, ka, va, iq, ik, iw, cache_k, cache_v, cache_idx_k, page_table):
    Bd, Tn = qa.shape[:2]
    past = page_table.shape[1] * PAGE_SIZE
    L = past + Tn
    topk = min(TOPK_MAX, L // 4)
    ik_past = cache_idx_k[page_table].reshape(Bd, past, IDX_DIM)
    ik_all = jnp.concatenate([ik_past.astype(ik.dtype), ik], axis=1)
    qpos = past + jnp.arange(Tn)
    idx, valid = _index_select(iq, iw, ik_all, qpos, jnp.arange(L), topk)
    in_past = idx < past
    pidx = jnp.minimum(idx, past - 1)
    phys = jnp.take_along_axis(page_table, (pidx // PAGE_SIZE).reshape(Bd, -1), axis=1).reshape(idx.shape)
    off = pidx % PAGE_SIZE
    nidx = jnp.clip(idx - past, 0, Tn - 1)
    gather = jax.vmap(lambda rows, ids: rows[ids])
    sel = lambda pool, new: jnp.where(in_past[..., None, None], pool[phys, off].astype(new.dtype), gather(new, nidx))
    return _attend(qa, sel(cache_k, ka), sel(cache_v, va), valid)


def _retention(q, k, v, state):
    B, T, H, dk = q.shape
    dv = v.shape[-1]
    C = min(RET_CHUNK, T)
    if T % C:
        C = T
    n = T // C
    lg = jnp.log1p(-jnp.exp2(-5.0 - jnp.arange(H, dtype=f32)))
    i = jnp.arange(C)
    diff = i[:, None] - i[None, :]
    dmask = jnp.where(diff >= 0, jnp.exp(lg[:, None, None] * jnp.maximum(diff, 0)), 0.0)
    q_dec = jnp.exp(lg[:, None] * (i + 1))[None, :, :, None]
    k_dec = jnp.exp(lg[:, None] * (C - 1 - i))[None, :, :, None]
    s_dec = jnp.exp(lg * C)[None, :, None, None]
    to_chunks = lambda a: a.astype(f32).reshape(B, n, C, H, a.shape[-1]).transpose(1, 0, 3, 2, 4)

    def step(S, inp):
        qc, kc, vc = inp
        att = jnp.einsum('bhid,bhjd->bhij', qc, kc) * dmask
        o = jnp.einsum('bhij,bhjv->bhiv', att, vc) + jnp.einsum('bhid,bhdv->bhiv', qc, S) * q_dec
        S = S * s_dec + jnp.einsum('bhjd,bhjv->bhdv', kc * k_dec, vc)
        return S, o

    S, o = lax.scan(step, state.astype(f32), (to_chunks(q), to_chunks(k), to_chunks(v)))
    return o.transpose(1, 0, 3, 2, 4).reshape(B, T, H, dv), S


def _moe(h, w_router, b_router, w_gate_up, b_gate_up, w_down, b_down):
    logits = (h @ w_router).astype(f32) + b_router.astype(f32)
    top_val, top_idx = lax.top_k(logits, TOP_K)
    gates = jax.nn.softmax(top_val, axis=-1)
    combine = jnp.einsum('nk,nke->ne', gates, jax.nn.one_hot(top_idx, N_EXPERTS, dtype=f32))
    out = jnp.zeros(h.shape, f32)
    for e in range(N_EXPERTS):
        gu = h @ w_gate_up[e] + b_gate_up[e]
        gate = jnp.minimum(gu[:, 0::2], SWIGLU_LIMIT)
        up = jnp.clip(gu[:, 1::2], -SWIGLU_LIMIT, SWIGLU_LIMIT)
        act = (up + 1.0) * gate * jax.nn.sigmoid(SWIGLU_ALPHA * gate)
        out = out + combine[:, e:e + 1] * (act @ w_down[e] + b_down[e]).astype(f32)
    return out.astype(h.dtype)


def _tail(x, attn, ret, gb, w_o, ret_gn_w, ln1_g, ln1_b, w_router, b_router, w_gate_up, b_gate_up,
          w_down, b_down, ln2_g, ln2_b):
    B, T, D = x.shape
    mu = ret.mean(-1, keepdims=True)
    var = jnp.mean(jnp.square(ret - mu), -1, keepdims=True)
    rb = ((ret - mu) * lax.rsqrt(var + LN_EPS)).reshape(B, T, WIDTH_B) * ret_gn_w.astype(f32)
    rb = rb * jax.nn.silu(gb.astype(f32))
    mix = jnp.concatenate([attn, rb.astype(x.dtype)], axis=-1) @ w_o
    h = _layernorm(DN_ALPHA * x + mix, ln1_g, ln1_b)
    f = _moe(h.reshape(B * T, D), w_router, b_router, w_gate_up, b_gate_up, w_down, b_down).reshape(B, T, D)
    return _layernorm(DN_ALPHA * h + f, ln2_g, ln2_b)


def setup_inputs(seed: int = 0) -> dict:
    key = jax.random.key(seed)
    ks = jax.random.split(key, 24)
    n_pages = PAST_LEN // PAGE_SIZE
    n_used = DEC_BATCH * n_pages
    n_pool = (5 * n_used + 3) // 4

    def nrm(k, shape, scale):
        return jax.random.normal(k, shape, f32) * scale

    page_table = jax.random.permutation(ks[0], n_pool)[:n_used].reshape(DEC_BATCH, n_pages).astype(jnp.int32)
    return {
        'x_prompt': nrm(ks[1], (BATCH, SEQ, D_MODEL), 1.0),
        'x_sample': nrm(ks[2], (DEC_BATCH, DEC_SEQ, D_MODEL), 1.0),
        'cache_k': nrm(ks[3], (DEPTH, n_pool, PAGE_SIZE, N_KV_A, HEAD_DIM_A), 1.0),
        'cache_v': nrm(ks[4], (DEPTH, n_pool, PAGE_SIZE, N_KV_A, HEAD_DIM_A), 1.0),
        'cache_idx_k': nrm(ks[5], (DEPTH, n_pool, PAGE_SIZE, IDX_DIM), 1.0),
        'state_ret': nrm(ks[6], (DEPTH, DEC_BATCH, N_HEADS_B, QK_DIM_B, V_DIM_B), 0.5),
        'page_table': page_table,
        'w_in': nrm(ks[7], (DEPTH, D_MODEL, PROJ_WIDTH), D_MODEL ** -0.5),
        'w_o': nrm(ks[8], (DEPTH, MIX_WIDTH, D_MODEL), DN_BETA * MIX_WIDTH ** -0.5),
        'ret_gn_w': 1.0 + nrm(ks[9], (DEPTH, WIDTH_B), 0.02),
        'ln1_g': 1.0 + nrm(ks[10], (DEPTH, D_MODEL), 0.02),
        'ln1_b': nrm(ks[11], (DEPTH, D_MODEL), 0.02),
        'w_router': nrm(ks[12], (DEPTH, D_MODEL, N_EXPERTS), D_MODEL ** -0.5),
        'b_router': nrm(ks[13], (DEPTH, N_EXPERTS), 0.01),
        'w_gate_up': nrm(ks[14], (DEPTH, N_EXPERTS, D_MODEL, 2 * D_FF), D_MODEL ** -0.5),
        'b_gate_up': nrm(ks[15], (DEPTH, N_EXPERTS, 2 * D_FF), 0.02),
        'w_down': nrm(ks[16], (DEPTH, N_EXPERTS, D_FF, D_MODEL), DN_BETA * D_FF ** -0.5),
        'b_down': nrm(ks[17], (DEPTH, N_EXPERTS, D_MODEL), 0.02),
        'ln2_g': 1.0 + nrm(ks[18], (DEPTH, D_MODEL), 0.02),
        'ln2_b': nrm(ks[19], (DEPTH, D_MODEL), 0.02),
    }


def reference(x_prompt, x_sample, cache_k, cache_v, cache_idx_k, state_ret, page_table, w_in, w_o, ret_gn_w,
              ln1_g, ln1_b, w_router, b_router, w_gate_up, b_gate_up, w_down, b_down, ln2_g, ln2_b):
    B, T, _ = x_prompt.shape
    Bd, Tn, _ = x_sample.shape
    past = page_table.shape[1] * PAGE_SIZE
    pos_p = jnp.arange(T)
    pos_s = past + jnp.arange(Tn)
    hp, hs = x_prompt, x_sample
    kp_l, vp_l, ikp_l, sp_l, ks_l, vs_l, iks_l, ss_l = [], [], [], [], [], [], [], []
    for l in range(DEPTH):
        tail_w = (w_o[l], ret_gn_w[l], ln1_g[l], ln1_b[l], w_router[l], b_router[l], w_gate_up[l],
                  b_gate_up[l], w_down[l], b_down[l], ln2_g[l], ln2_b[l])
        qa, ka, va, iq, ik, iw, qb, kb, vb, gb = _project(hp, pos_p, w_in[l])
        attn = _dsa_prompt(qa, ka, va, iq, ik, iw)
        ret, s_p = _retention(qb, kb, vb, jnp.zeros((B, N_HEADS_B, QK_DIM_B, V_DIM_B), f32))
        hp = _tail(hp, attn, ret, gb, *tail_w)
        kp_l.append(ka)
        vp_l.append(va)
        ikp_l.append(ik)
        sp_l.append(s_p)
        qa, ka, va, iq, ik, iw, qb, kb, vb, gb = _project(hs, pos_s, w_in[l])
        attn = _dsa_sample(qa, ka, va, iq, ik, iw, cache_k[l], cache_v[l], cache_idx_k[l], page_table)
        ret, s_s = _retention(qb, kb, vb, state_ret[l])
        hs = _tail(hs, attn, ret, gb, *tail_w)
        ks_l.append(ka)
        vs_l.append(va)
        iks_l.append(ik)
        ss_l.append(s_s)
    k_prompt = jnp.stack(kp_l)
    v_prompt = jnp.stack(vp_l)
    idx_k_prompt = jnp.stack(ikp_l)
    ret_state_prompt = jnp.stack(sp_l)
    k_sample = jnp.stack(ks_l)
    v_sample = jnp.stack(vs_l)
    idx_k_sample = jnp.stack(iks_l)
    ret_state_sample = jnp.stack(ss_l)
    return (hp, hs, k_prompt, v_prompt, idx_k_prompt, ret_state_prompt, k_sample, v_sample, idx_k_sample, ret_state_sample)
```

```python
import functools

import numpy as np
import jax
import jax.numpy as jnp
from jax import lax
from jax.experimental import pallas as pl
from jax.experimental.pallas import tpu as pltpu

f32 = jnp.float32
bf16 = jnp.bfloat16
i32 = jnp.int32

PAGE_SIZE = 128
HEAD_DIM_A = 128
N_HEADS_A = 8
N_KV_A = 4
ROPE_DIM_A = 32
ROPE_THETA = 500000.0
IDX_HEADS = 16
IDX_DIM = 64
IDX_ROPE_DIM = 16
TOPK_MAX = 256
V_DIM_B = 128
N_HEADS_B = 8
QK_DIM_B = 64
RET_CHUNK = 128
RET_THETA = 10000.0
TOP_K = 4
SWIGLU_LIMIT = 7.0
SWIGLU_ALPHA = 1.702
LN_EPS = 1e-5

LANES = 128
VMEM_LIMIT = 56 * 1024 * 1024

PROJ_TN = 512
C_QA, C_KA, C_VA, C_IQ, C_QB, C_KB, C_VB, C_GB, C_TAIL, PROJ_W = 0, 1024, 1536, 2048, 3072, 3584, 4096, 5120, 6144, 6656
PROJ_TILE_TYPES = (1, 1, 1, 0, 2, 2, 3, 4, 0, 0, 0, 0, 5)

INT_MIN = -2 ** 31
KEY_NEG_INF = -2139095041
NEG_BIG = -1e30

NT_DIMS = (((1,), (1,)), ((), ()))
TN_DIMS = (((0,), (0,)), ((), ()))


def _pick(n, cands):
    for c in cands:
        if n % c == 0:
            return c
    raise ValueError(f"no tile for {n}")


def _cparams(sem, vmem=VMEM_LIMIT):
    return pltpu.CompilerParams(dimension_semantics=sem, vmem_limit_bytes=vmem)


def _rope_table(pos, rot_dim, theta, period, scale=1.0, active=LANES):
    half = rot_dim // 2
    inv_freq = 1.0 / (theta ** (jnp.arange(half, dtype=f32) / half))
    ang = pos.astype(f32)[:, None] * inv_freq[None, :]
    cos, sin = jnp.cos(ang), jnp.sin(ang)
    lane = np.arange(LANES)
    d = lane % period
    first = (d < half) & (lane < active)
    second = (d >= half) & (d < rot_dim) & (lane < active)
    idx = np.where(first, d, np.where(second, d - half, 0))
    cos_l, sin_l = cos[:, idx], sin[:, idx]
    c = jnp.where(first | second, cos_l, 1.0)
    s1 = jnp.where(second, sin_l, 0.0)
    s2 = jnp.where(first, -sin_l, 0.0)
    return jnp.concatenate([c, s1, s2], axis=1) * scale


def _proj_kernel(tt_ref, x_ref, w_ref, tab_ref, of_ref, ob_ref, xb_scr):
    j = pl.program_id(1)

    @pl.when(j == 0)
    def _():
        xb_scr[...] = x_ref[...].astype(bf16)

    u = jnp.dot(xb_scr[...], w_ref[...], preferred_element_type=f32)
    t = tt_ref[j]

    def store(v):
        of_ref[...] = v
        ob_ref[...] = v.astype(bf16)

    @pl.when(t == 0)
    def _():
        store(u)

    def rope(half):
        c = tab_ref[0, :, 0:LANES]
        s1 = tab_ref[0, :, LANES:2 * LANES]
        s2 = tab_ref[0, :, 2 * LANES:3 * LANES]
        outs = []
        for q in range(PROJ_TN // LANES):
            uc = u[:, q * LANES:(q + 1) * LANES]
            outs.append(uc * c + pltpu.roll(uc, half, 1) * s1 + pltpu.roll(uc, LANES - half, 1) * s2)
        store(jnp.concatenate(outs, axis=1))

    @pl.when(t == 1)
    def _():
        rope(ROPE_DIM_A // 2)

    @pl.when((t == 2) | (t == 5))
    def _():
        rope(IDX_ROPE_DIM // 2)

    @pl.when((t == 3) | (t == 4))
    def _():
        rope(QK_DIM_B // 2)


def _project(x_all, pos_all, w_in):
    n, d = x_all.shape
    tm = _pick(n, (768, 512, 384, 256, 128, 64, 32, 16, 8))
    o = np.cumsum((0, 1024, 512, 512, 1024, 64, 16, 512, 512, 1024, 1024))
    wp = jnp.concatenate([w_in[:, o[0]:o[4]], w_in[:, o[6]:o[10]], w_in[:, o[4]:o[6]],
                          jnp.zeros((d, PROJ_W - C_TAIL - 80), w_in.dtype)], axis=1).astype(bf16)
    tabs = jnp.stack([
        _rope_table(pos_all, ROPE_DIM_A, ROPE_THETA, HEAD_DIM_A),
        _rope_table(pos_all, ROPE_DIM_A, ROPE_THETA, HEAD_DIM_A),
        _rope_table(pos_all, IDX_ROPE_DIM, ROPE_THETA, IDX_DIM),
        _rope_table(pos_all, QK_DIM_B, RET_THETA, QK_DIM_B),
        _rope_table(pos_all, QK_DIM_B, RET_THETA, QK_DIM_B, scale=QK_DIM_B ** -0.5),
        _rope_table(pos_all, IDX_ROPE_DIM, ROPE_THETA, IDX_DIM, active=IDX_DIM),
    ])
    tt = jnp.asarray(PROJ_TILE_TYPES, i32)
    nj = PROJ_W // PROJ_TN
    return pl.pallas_call(
        _proj_kernel,
        grid_spec=pltpu.PrefetchScalarGridSpec(
            num_scalar_prefetch=1,
            grid=(n // tm, nj),
            in_specs=[
                pl.BlockSpec((tm, d), lambda i, j, tt: (i, 0)),
                pl.BlockSpec((d, PROJ_TN), lambda i, j, tt: (0, j)),
                pl.BlockSpec((1, tm, 3 * LANES), lambda i, j, tt: (tt[j], i, 0)),
            ],
            out_specs=[
                pl.BlockSpec((tm, PROJ_TN), lambda i, j, tt: (i, j)),
                pl.BlockSpec((tm, PROJ_TN), lambda i, j, tt: (i, j)),
            ],
            scratch_shapes=[pltpu.VMEM((tm, d), bf16)],
        ),
        out_shape=[jax.ShapeDtypeStruct((n, PROJ_W), f32), jax.ShapeDtypeStruct((n, PROJ_W), bf16)],
        compiler_params=_cparams(("arbitrary", "arbitrary")),
        name="proj",
    )(tt, x_all, wp, tabs)


def _sort_key(s):
    bits = pltpu.bitcast(s, i32)
    return bits ^ ((bits >> 31) & 0x7FFFFFFF)


def _kth_threshold(count_ge, rows, k):
    def body(step, ans):
        cand = ans + jnp.left_shift(jnp.int32(1), 31 - step)
        return jnp.where(count_ge(cand) >= k, cand, ans)

    ans = lax.fori_loop(0, 32, body, jnp.full((rows, 1), INT_MIN, i32))
    return jnp.maximum(ans, KEY_NEG_INF + 1)


def _dsa_prompt_kernel(iq_ref, iw_ref, qa_ref, ikd_ref, ka_ref, va_ref, o_ref,
                       qst, wst, qg, skey, m_scr, l_scr, acc_scr, *, tq, tk, topk):
    i = pl.program_id(0)
    n_chunks = (i * tq + tq + tk - 1) // tk
    lane = lax.broadcasted_iota(i32, (tq, LANES), 1)
    nrep = tk // LANES

    for p in range(IDX_HEADS // 2):
        blk = iq_ref[:, p * LANES:(p + 1) * LANES]
        qst[(2 * p) * tq:(2 * p + 1) * tq, :] = jnp.where(lane < IDX_DIM, blk, jnp.zeros_like(blk))
        qst[(2 * p + 1) * tq:(2 * p + 2) * tq, :] = jnp.where(lane >= IDX_DIM, blk, jnp.zeros_like(blk))
    w = iw_ref[...]
    wscale = IDX_DIM ** -0.5 * IDX_HEADS ** -0.5
    for h in range(IDX_HEADS):
        wst[h * tq:(h + 1) * tq, :] = jnp.broadcast_to(w[:, IDX_DIM + h:IDX_DIM + h + 1] * wscale, (tq, LANES))

    qpos = i * tq + lax.broadcasted_iota(i32, (tq, tk), 0)

    def score_body(c, carry):
        k0 = pl.multiple_of(c * tk, tk)
        kd = ikd_ref[pl.ds(k0, tk), :]
        logits = lax.dot_general(qst[...], kd, NT_DIMS, preferred_element_type=f32)
        acc = jnp.zeros((tq, tk), f32)
        for h in range(IDX_HEADS):
            wh = wst[h * tq:(h + 1) * tq, :]
            acc = acc + jnp.maximum(logits[h * tq:(h + 1) * tq, :], 0.0) * jnp.concatenate([wh] * nrep, axis=1)
        kpos = k0 + lax.broadcasted_iota(i32, (tq, tk), 1)
        skey[c] = _sort_key(jnp.where(kpos <= qpos, acc, -jnp.inf))
        return carry

    lax.fori_loop(0, n_chunks, score_body, 0)

    def count_ge(cand):
        cb = jnp.broadcast_to(cand, (tq, LANES))

        def body(c, acc):
            blk = skey[c]
            for q in range(nrep):
                acc = acc + jnp.where(blk[:, q * LANES:(q + 1) * LANES] >= cb, 1.0, 0.0)
            return acc

        acc = lax.fori_loop(0, n_chunks, body, jnp.zeros((tq, LANES), f32))
        return jnp.sum(acc, axis=1, keepdims=True)

    thr = _kth_threshold(count_ge, tq, float(topk))
    thr2 = jnp.concatenate([thr, thr], axis=0)

    for g in range(N_KV_A):
        qg[g, 0:tq, :] = qa_ref[:, (2 * g) * LANES:(2 * g + 1) * LANES]
        qg[g, tq:2 * tq, :] = qa_ref[:, (2 * g + 1) * LANES:(2 * g + 2) * LANES]
    m_scr[...] = jnp.full(m_scr.shape, NEG_BIG, f32)
    l_scr[...] = jnp.zeros(l_scr.shape, f32)
    acc_scr[...] = jnp.zeros(acc_scr.shape, f32)
    scale = HEAD_DIM_A ** -0.5

    def att_body(c, carry):
        k0 = pl.multiple_of(c * tk, tk)
        sk = skey[c]
        mask = jnp.concatenate([sk, sk], axis=0) >= thr2
        for g in range(N_KV_A):
            kg = ka_ref[pl.ds(k0, tk), g * LANES:(g + 1) * LANES]
            vg = va_ref[pl.ds(k0, tk), g * LANES:(g + 1) * LANES]
            s = lax.dot_general(qg[g], kg, NT_DIMS, preferred_element_type=f32)
            sm = jnp.where(mask, s, NEG_BIG)
            m_old = m_scr[g]
            m_new = jnp.maximum(m_old, jnp.max(sm, axis=1, keepdims=True))
            alpha = jnp.exp((m_old - m_new) * scale)
            p = jnp.where(mask, jnp.exp((sm - m_new) * scale), 0.0)
            l_scr[g] = alpha * l_scr[g] + jnp.sum(p, axis=1, keepdims=True)
            acc_scr[g] = alpha * acc_scr[g] + jnp.dot(p.astype(bf16), vg, preferred_element_type=f32)
            m_scr[g] = m_new
        return carry

    lax.fori_loop(0, n_chunks, att_body, 0)

    for g in range(N_KV_A):
        o = acc_scr[g] / l_scr[g]
        o_ref[:, (2 * g) * LANES:(2 * g + 1) * LANES] = o[0:tq].astype(o_ref.dtype)
        o_ref[:, (2 * g + 1) * LANES:(2 * g + 2) * LANES] = o[tq:2 * tq].astype(o_ref.dtype)


def _dsa_prompt(u_f, u_b, ikd, t):
    tq = _pick(t, (128,))
    tk = _pick(t, (512, 256, 128))
    topk = min(TOPK_MAX, t // 4)
    wa = N_HEADS_A * HEAD_DIM_A
    wkv = N_KV_A * HEAD_DIM_A
    kern = functools.partial(_dsa_prompt_kernel, tq=tq, tk=tk, topk=topk)
    one = pl.Buffered(1)
    return pl.pallas_call(
        kern,
        grid=(t // tq,),
        in_specs=[
            pl.BlockSpec((tq, IDX_HEADS * IDX_DIM), lambda i: (i, C_IQ // (IDX_HEADS * IDX_DIM))),
            pl.BlockSpec((tq, LANES), lambda i: (i, C_TAIL // LANES)),
            pl.BlockSpec((tq, wa), lambda i: (i, C_QA // wa)),
            pl.BlockSpec((t, LANES), lambda i: (0, 0), pipeline_mode=one),
            pl.BlockSpec((t, wkv), lambda i: (0, C_KA // wkv), pipeline_mode=one),
            pl.BlockSpec((t, wkv), lambda i: (0, C_VA // wkv), pipeline_mode=one),
        ],
        out_specs=pl.BlockSpec((tq, wa), lambda i: (i, 0)),
        out_shape=jax.ShapeDtypeStruct((t, wa), bf16),
        scratch_shapes=[
            pltpu.VMEM((IDX_HEADS * tq, LANES), bf16),
            pltpu.VMEM((IDX_HEADS * tq, LANES), f32),
            pltpu.VMEM((N_KV_A, 2 * tq, LANES), bf16),
            pltpu.VMEM((t // tk, tq, tk), i32),
            pltpu.VMEM((N_KV_A, 2 * tq, 1), f32),
            pltpu.VMEM((N_KV_A, 2 * tq, 1), f32),
            pltpu.VMEM((N_KV_A, 2 * tq, LANES), f32),
        ],
        compiler_params=_cparams(("arbitrary",)),
        name="dsa_prompt",
    )(u_b, u_f, u_b, ikd, u_b, u_b)


def _dsa_sample_score_kernel(pt_ref, *refs, tn, ppg, topk, n_groups):
    pages = refs[:ppg]
    iq_ref, tail_ref, sk_ref, thr_ref, qs, wst = refs[ppg:]
    c = pl.program_id(1)
    wscale = IDX_DIM ** -0.5 * IDX_HEADS ** -0.5
    kw = ppg * PAGE_SIZE

    @pl.when(c == 0)
    def _():
        iq = iq_ref[...]
        w = tail_ref[...]
        for h in range(IDX_HEADS):
            qs[h * tn:(h + 1) * tn, :] = iq[:, h * IDX_DIM:(h + 1) * IDX_DIM]
            wst[h * tn:(h + 1) * tn, :] = jnp.broadcast_to(w[:, IDX_DIM + h:IDX_DIM + h + 1] * wscale, (tn, LANES))

    def scores(keys_bf, width):
        logits = lax.dot_general(qs[...].astype(bf16), keys_bf, NT_DIMS, preferred_element_type=f32)
        acc = jnp.zeros((tn, width), f32)
        for h in range(IDX_HEADS):
            wh = wst[h * tn:(h + 1) * tn, :]
            acc = acc + jnp.maximum(logits[h * tn:(h + 1) * tn, :], 0.0) * jnp.concatenate([wh] * (width // LANES), axis=1)
        return acc

    keys = jnp.concatenate([p[0] for p in pages], axis=0).astype(bf16)
    sk_ref[0, c] = _sort_key(scores(keys, kw))

    @pl.when(c == n_groups - 1)
    def _():
        ik_new = tail_ref[:, 0:IDX_DIM].astype(bf16)
        kpad = jnp.concatenate([ik_new, jnp.zeros((LANES - tn, IDX_DIM), bf16)], axis=0)
        s_new = scores(kpad, LANES)
        qi = lax.broadcasted_iota(i32, (tn, LANES), 0)
        kj = lax.broadcasted_iota(i32, (tn, LANES), 1)
        knew = _sort_key(jnp.where(kj <= qi, s_new, -jnp.inf))
        sk_ref[0, n_groups] = jnp.concatenate([knew, jnp.full((tn, kw - LANES), KEY_NEG_INF, i32)], axis=1)

        def count_ge(cand):
            cb = jnp.broadcast_to(cand, (tn, LANES))

            def body(cc, acc):
                blk = sk_ref[0, cc]
                for q in range(kw // LANES):
                    acc = acc + jnp.where(blk[:, q * LANES:(q + 1) * LANES] >= cb, 1.0, 0.0)
                return acc

            acc = lax.fori_loop(0, n_groups + 1, body, jnp.zeros((tn, LANES), f32))
            return jnp.sum(acc, axis=1, keepdims=True)

        thr = _kth_threshold(count_ge, tn, float(topk))
        thr_ref[0] = jnp.broadcast_to(thr, (tn, LANES))


def _dsa_sample_attn_kernel(pt_ref, *refs, tn, ppg, n_groups):
    kpages = refs[:ppg]
    vpages = refs[ppg:2 * ppg]
    qa_ref, kn_ref, vn_ref, sk_ref, skn_ref, thr_ref, o_ref, qg, m_scr, l_scr, acc_scr = refs[2 * ppg:]
    c = pl.program_id(1)
    scale = HEAD_DIM_A ** -0.5

    @pl.when(c == 0)
    def _():
        qa = qa_ref[...]
        for g in range(N_KV_A):
            qg[g, 0:tn, :] = qa[:, (2 * g) * LANES:(2 * g + 1) * LANES]
            qg[g, tn:2 * tn, :] = qa[:, (2 * g + 1) * LANES:(2 * g + 2) * LANES]
        m_scr[...] = jnp.full(m_scr.shape, NEG_BIG, f32)
        l_scr[...] = jnp.zeros(l_scr.shape, f32)
        acc_scr[...] = jnp.zeros(acc_scr.shape, f32)

    thr = thr_ref[0][:, 0:1]
    thr2 = jnp.concatenate([thr, thr], axis=0)

    def attend(kmat, vmat, sk):
        mask = jnp.concatenate([sk, sk], axis=0) >= thr2
        for g in range(N_KV_A):
            kg = kmat[:, g * LANES:(g + 1) * LANES]
            vg = vmat[:, g * LANES:(g + 1) * LANES]
            s = lax.dot_general(qg[g].astype(bf16), kg, NT_DIMS, preferred_element_type=f32)
            sm = jnp.where(mask, s, NEG_BIG)
            m_old = m_scr[g]
            m_new = jnp.maximum(m_old, jnp.max(sm, axis=1, keepdims=True))
            alpha = jnp.exp((m_old - m_new) * scale)
            p = jnp.where(mask, jnp.exp((sm - m_new) * scale), 0.0)
            l_scr[g] = alpha * l_scr[g] + jnp.sum(p, axis=1, keepdims=True)
            acc_scr[g] = alpha * acc_scr[g] + jnp.dot(p.astype(bf16), vg, preferred_element_type=f32)
            m_scr[g] = m_new

    kmat = jnp.concatenate([p[0] for p in kpages], axis=0).astype(bf16)
    vmat = jnp.concatenate([p[0] for p in vpages], axis=0).astype(bf16)
    attend(kmat, vmat, sk_ref[0, 0])

    @pl.when(c == n_groups - 1)
    def _():
        zpad = jnp.zeros((LANES - tn, N_KV_A * HEAD_DIM_A), bf16)
        attend(jnp.concatenate([kn_ref[...].astype(bf16), zpad], axis=0),
               jnp.concatenate([vn_ref[...].astype(bf16), zpad], axis=0),
               skn_ref[0, 0][:, 0:LANES])
        for g in range(N_KV_A):
            o = acc_scr[g] / l_scr[g]
            o_ref[:, (2 * g) * LANES:(2 * g + 1) * LANES] = o[0:tn]
            o_ref[:, (2 * g + 1) * LANES:(2 * g + 2) * LANES] = o[tn:2 * tn]


def _dsa_sample(u_f, row0, bd, tn, cache_k, cache_v, cache_idx_k, page_table):
    n_pages = page_table.shape[1]
    past = n_pages * PAGE_SIZE
    topk = min(TOPK_MAX, (past + tn) // 4)
    ppg = _pick(n_pages, (8, 4, 2, 1))
    ng = n_pages // ppg
    kw = ppg * PAGE_SIZE
    wkv = N_KV_A * HEAD_DIM_A
    wa = N_HEADS_A * HEAD_DIM_A
    n_pool = cache_k.shape[0]
    ck = cache_k.reshape(n_pool, PAGE_SIZE, wkv)
    cv = cache_v.reshape(n_pool, PAGE_SIZE, wkv)
    rb0 = row0 // tn

    def page_spec(j, width):
        return pl.BlockSpec((1, PAGE_SIZE, width), lambda b, c, pt, j=j: (pt[b, c * ppg + j], 0, 0))

    score = pl.pallas_call(
        functools.partial(_dsa_sample_score_kernel, tn=tn, ppg=ppg, topk=topk, n_groups=ng),
        grid_spec=pltpu.PrefetchScalarGridSpec(
            num_scalar_prefetch=1,
            grid=(bd, ng),
            in_specs=[page_spec(j, IDX_DIM) for j in range(ppg)] + [
                pl.BlockSpec((tn, IDX_HEADS * IDX_DIM), lambda b, c, pt: (rb0 + b, C_IQ // (IDX_HEADS * IDX_DIM))),
                pl.BlockSpec((tn, LANES), lambda b, c, pt: (rb0 + b, C_TAIL // LANES)),
            ],
            out_specs=[
                pl.BlockSpec((1, ng + 1, tn, kw), lambda b, c, pt: (b, 0, 0, 0)),
                pl.BlockSpec((1, tn, LANES), lambda b, c, pt: (b, 0, 0)),
            ],
            scratch_shapes=[pltpu.VMEM((IDX_HEADS * tn, IDX_DIM), f32), pltpu.VMEM((IDX_HEADS * tn, LANES), f32)],
        ),
        out_shape=[jax.ShapeDtypeStruct((bd, ng + 1, tn, kw), i32), jax.ShapeDtypeStruct((bd, tn, LANES), i32)],
        compiler_params=_cparams(("arbitrary", "arbitrary")),
        name="dsa_sample_score",
    )
    skeys, thr = score(page_table, *([cache_idx_k] * ppg), u_f, u_f)

    attn = pl.pallas_call(
        functools.partial(_dsa_sample_attn_kernel, tn=tn, ppg=ppg, n_groups=ng),
        grid_spec=pltpu.PrefetchScalarGridSpec(
            num_scalar_prefetch=1,
            grid=(bd, ng),
            in_specs=[page_spec(j, wkv) for j in range(ppg)] + [page_spec(j, wkv) for j in range(ppg)] + [
                pl.BlockSpec((tn, wa), lambda b, c, pt: (rb0 + b, C_QA // wa)),
                pl.BlockSpec((tn, wkv), lambda b, c, pt: (rb0 + b, C_KA // wkv)),
                pl.BlockSpec((tn, wkv), lambda b, c, pt: (rb0 + b, C_VA // wkv)),
                pl.BlockSpec((1, 1, tn, kw), lambda b, c, pt: (b, c, 0, 0)),
                pl.BlockSpec((1, 1, tn, kw), lambda b, c, pt: (b, ng, 0, 0)),
                pl.BlockSpec((1, tn, LANES), lambda b, c, pt: (b, 0, 0)),
            ],
            out_specs=pl.BlockSpec((tn, wa), lambda b, c, pt: (b, 0)),
            scratch_shapes=[
                pltpu.VMEM((N_KV_A, 2 * tn, LANES), f32),
                pltpu.VMEM((N_KV_A, 2 * tn, 1), f32),
                pltpu.VMEM((N_KV_A, 2 * tn, 1), f32),
                pltpu.VMEM((N_KV_A, 2 * tn, LANES), f32),
            ],
        ),
        out_shape=jax.ShapeDtypeStruct((bd * tn, wa), f32),
        compiler_params=_cparams(("arbitrary", "arbitrary")),
        name="dsa_sample_attn",
    )
    return attn(page_table, *([ck] * ppg), *([cv] * ppg), u_f, u_f, u_f, skeys, skeys, thr)


def _ret_kernel(q_ref, k_ref, v_ref, g_ref, gnw_ref, dmask_ref, qdec_ref, kdec_ref, sdec_ref, s0_ref,
                o_ref, sout_ref, s_scr, *, n_chunks):
    c = pl.program_id(1)

    @pl.when(c == 0)
    def _():
        s_scr[...] = s0_ref[0]

    for h in range(N_HEADS_B):
        q = q_ref[:, h * QK_DIM_B:(h + 1) * QK_DIM_B]
        k = k_ref[:, h * QK_DIM_B:(h + 1) * QK_DIM_B]
        v = v_ref[:, h * V_DIM_B:(h + 1) * V_DIM_B].astype(bf16)
        gate = g_ref[:, h * V_DIM_B:(h + 1) * V_DIM_B]
        qb = q.astype(bf16)
        att = lax.dot_general(qb, k.astype(bf16), NT_DIMS, preferred_element_type=f32) * dmask_ref[h]
        s_old = s_scr[h]
        o = (jnp.dot(att.astype(bf16), v, preferred_element_type=f32)
             + jnp.dot(qb, s_old.astype(bf16), preferred_element_type=f32) * qdec_ref[h])
        kd = (k * kdec_ref[h]).astype(bf16)
        s_scr[h] = s_old * sdec_ref[h] + lax.dot_general(kd, v, TN_DIMS, preferred_element_type=f32)
        mu = jnp.mean(o, axis=-1, keepdims=True)
        var = jnp.mean(jnp.square(o - mu), axis=-1, keepdims=True)
        rb = (o - mu) * lax.rsqrt(var + LN_EPS) * gnw_ref[:, h * V_DIM_B:(h + 1) * V_DIM_B]
        rb = rb * (gate / (1.0 + jnp.exp(-gate)))
        o_ref[:, h * V_DIM_B:(h + 1) * V_DIM_B] = rb.astype(o_ref.dtype)

    @pl.when(c == n_chunks - 1)
    def _():
        sout_ref[0] = s_scr[...]


def _retention(u_f, row0, nb, t, state0, gn_w, out_dtype):
    ch = min(RET_CHUNK, t)
    if t % ch:
        ch = t
    n = t // ch
    hb = N_HEADS_B
    lg = jnp.log1p(-jnp.exp2(-5.0 - jnp.arange(hb, dtype=f32)))
    i = jnp.arange(ch)
    diff = i[:, None] - i[None, :]
    dmask = jnp.where(diff >= 0, jnp.exp(lg[:, None, None] * jnp.maximum(diff, 0)), 0.0)
    qdec = jnp.broadcast_to(jnp.exp(lg[:, None] * (i + 1))[:, :, None], (hb, ch, V_DIM_B))
    kdec = jnp.broadcast_to(jnp.exp(lg[:, None] * (ch - 1 - i))[:, :, None], (hb, ch, QK_DIM_B))
    sdec = jnp.broadcast_to(jnp.exp(lg * ch)[:, None, None], (hb, 1, V_DIM_B))
    wqk = hb * QK_DIM_B
    wv = hb * V_DIM_B
    rb0 = row0 // ch
    full3 = lambda shp: pl.BlockSpec(shp, lambda b, c: (0, 0, 0))
    return pl.pallas_call(
        functools.partial(_ret_kernel, n_chunks=n),
        grid=(nb, n),
        in_specs=[
            pl.BlockSpec((ch, wqk), lambda b, c: (rb0 + b * n + c, C_QB // wqk)),
            pl.BlockSpec((ch, wqk), lambda b, c: (rb0 + b * n + c, C_KB // wqk)),
            pl.BlockSpec((ch, wv), lambda b, c: (rb0 + b * n + c, C_VB // wv)),
            pl.BlockSpec((ch, wv), lambda b, c: (rb0 + b * n + c, C_GB // wv)),
            pl.BlockSpec((1, wv), lambda b, c: (0, 0)),
            full3((hb, ch, ch)), full3((hb, ch, V_DIM_B)), full3((hb, ch, QK_DIM_B)), full3((hb, 1, V_DIM_B)),
            pl.BlockSpec((1, hb, QK_DIM_B, V_DIM_B), lambda b, c: (b, 0, 0, 0)),
        ],
        out_specs=[
            pl.BlockSpec((ch, wv), lambda b, c: (b * n + c, 0)),
            pl.BlockSpec((1, hb, QK_DIM_B, V_DIM_B), lambda b, c: (b, 0, 0, 0)),
        ],
        out_shape=[jax.ShapeDtypeStruct((nb * t, wv), out_dtype),
                   jax.ShapeDtypeStruct((nb, hb, QK_DIM_B, V_DIM_B), f32)],
        scratch_shapes=[pltpu.VMEM((hb, QK_DIM_B, V_DIM_B), f32)],
        compiler_params=_cparams(("arbitrary", "arbitrary")),
        name="retention",
    )(u_f, u_f, u_f, u_f, gn_w.reshape(1, wv), dmask, qdec, kdec, sdec, state0)


def _split_hi_lo(a):
    hi = a.astype(bf16)
    return hi, (a - hi.astype(f32)).astype(bf16)


def _tail1_kernel(attn_ref, rb_ref, x_ref, wo_ref, g1_ref, b1_ref, wrh_ref, wrl_ref, br_ref,
                  h_ref, hq_ref, eid_ref, gate_ref, *, tm, n_exp, alpha, wa):
    mix = (jnp.dot(attn_ref[...], wo_ref[0:wa, :], preferred_element_type=f32)
           + jnp.dot(rb_ref[...], wo_ref[wa:, :], preferred_element_type=f32))
    z = alpha * x_ref[...] + mix
    mu = jnp.mean(z, axis=-1, keepdims=True)
    var = jnp.mean(jnp.square(z - mu), axis=-1, keepdims=True)
    h = (z - mu) * lax.rsqrt(var + LN_EPS) * g1_ref[...] + b1_ref[...]
    h_ref[...] = h
    nq = h.shape[1] // LANES
    for j in range(nq):
        hq_ref[pl.ds(j, tm, stride=nq), :] = h[:, j * LANES:(j + 1) * LANES]

    hh, hl = _split_hi_lo(h)
    logits = (jnp.dot(hh, wrh_ref[...], preferred_element_type=f32)
              + jnp.dot(hl, wrh_ref[...], preferred_element_type=f32)
              + jnp.dot(hh, wrl_ref[...], preferred_element_type=f32)) + br_ref[...]
    lane = lax.broadcasted_iota(i32, (tm, LANES), 1)
    lanef = lane.astype(f32)
    logits = jnp.where(lane < n_exp, logits, -jnp.inf)
    vals, ids = [], []
    for _ in range(TOP_K):
        m = jnp.max(logits, axis=1, keepdims=True)
        idx = jnp.min(jnp.where(logits == m, lanef, float(LANES)), axis=1, keepdims=True)
        vals.append(m)
        ids.append(idx)
        logits = jnp.where(lanef == idx, -jnp.inf, logits)
    es = [jnp.exp(v - vals[0]) for v in vals]
    den = es[0] + es[1] + es[2] + es[3]
    eid_ref[...] = jnp.concatenate(ids, axis=1).astype(i32)
    gate_ref[...] = jnp.concatenate([e / den for e in es], axis=1)


def _tail1(attn_b, rb_b, x_all, w_o, ln1_g, ln1_b, w_router, b_router, alpha):
    n, d = x_all.shape
    wa = attn_b.shape[1]
    n_exp = w_router.shape[1]
    tm = _pick(n, (256, 128, 64, 32, 16, 8))
    nq = d // LANES
    wr = jnp.zeros((d, LANES), f32).at[:, :n_exp].set(w_router)
    wrh, wrl = _split_hi_lo(wr)
    br = jnp.zeros((1, LANES), f32).at[0, :n_exp].set(b_router)
    row = lambda w: pl.BlockSpec((1, w), lambda i: (0, 0))
    return pl.pallas_call(
        functools.partial(_tail1_kernel, tm=tm, n_exp=n_exp, alpha=alpha, wa=wa),
        grid=(n // tm,),
        in_specs=[
            pl.BlockSpec((tm, wa), lambda i: (i, 0)),
            pl.BlockSpec((tm, rb_b.shape[1]), lambda i: (i, 0)),
            pl.BlockSpec((tm, d), lambda i: (i, 0)),
            pl.BlockSpec(w_o.shape, lambda i: (0, 0), pipeline_mode=pl.Buffered(1)),
            row(d), row(d),
            pl.BlockSpec((d, LANES), lambda i: (0, 0)), pl.BlockSpec((d, LANES), lambda i: (0, 0)), row(LANES),
        ],
        out_specs=[
            pl.BlockSpec((tm, d), lambda i: (i, 0)),
            pl.BlockSpec((tm * nq, LANES), lambda i: (i, 0)),
            pl.BlockSpec((tm, TOP_K), lambda i: (i, 0)),
            pl.BlockSpec((tm, TOP_K), lambda i: (i, 0)),
        ],
        out_shape=[jax.ShapeDtypeStruct((n, d), f32), jax.ShapeDtypeStruct((n * nq, LANES), f32),
                   jax.ShapeDtypeStruct((n, TOP_K), i32), jax.ShapeDtypeStruct((n, TOP_K), f32)],
        compiler_params=_cparams(("arbitrary",)),
        name="tail1",
    )(attn_b, rb_b, x_all, w_o.astype(bf16), ln1_g.reshape(1, d), ln1_b.reshape(1, d), wrh, wrl, br)


MOE_R = 1024
MOE_SUB = 256
MOE_TF = 256


def _moe_plan(eid, n_exp, r_cap):
    n = eid.shape[0]
    p = n * TOP_K
    flat = eid.reshape(p)
    onehot = (flat[:, None] == jnp.arange(n_exp, dtype=i32)[None, :]).astype(i32)
    csum = jnp.cumsum(onehot, axis=0)
    rank = jnp.sum((csum - onehot) * onehot, axis=1)
    counts = csum[-1]
    ngrp = (counts + r_cap - 1) // r_cap
    gend = jnp.cumsum(ngrp)
    gstart = gend - ngrp
    g_of = gstart[flat] + rank // r_cap
    slot = rank % r_cap
    n_groups = n_exp + p // r_cap
    t_idx = jnp.arange(p, dtype=i32) // TOP_K
    k_idx = jnp.arange(p, dtype=i32) % TOP_K
    tok = jnp.zeros((n_groups, r_cap), i32).at[g_of, slot].set(t_idx)
    dst = jnp.zeros((n_groups, r_cap), i32).at[g_of, slot].set(k_idx * n + t_idx)
    gid = jnp.arange(n_groups, dtype=i32)
    total = gend[-1]
    gclamp = jnp.minimum(gid, total - 1)
    g_exp = jnp.searchsorted(gend, gclamp, side="right").astype(i32)
    g_rows = jnp.clip(counts[g_exp] - (gclamp - gstart[g_exp]) * r_cap, 0, r_cap)
    g_rows = jnp.where(gid < total, g_rows, 0).astype(i32)
    return g_exp, g_rows, tok.reshape(n_groups, 1, r_cap), dst.reshape(n_groups, 1, r_cap)


def _moe_kernel(ge_ref, gr_ref, tok_ref, dst_ref, hq_ref, wgu_ref, wd_ref, bgu_ref, bd_ref, pm_ref, y_ref,
                qbuf, xb, acc, wgu_b, wd_b, sem_in, sem_out, *, nq, nj):
    g = pl.program_id(0)
    j = pl.program_id(1)
    rows = gr_ref[g]
    nsub = (rows + MOE_SUB - 1) // MOE_SUB
    active = rows > 0

    def row_copy_in(r):
        t = tok_ref[0, 0, r]
        return pltpu.make_async_copy(hq_ref.at[pl.ds(pl.multiple_of(t * nq, nq), nq), :],
                                     qbuf.at[pl.ds(pl.multiple_of(r * nq, nq), nq), :], sem_in)

    def row_copy_out(r):
        d = dst_ref[0, 0, r]
        return pltpu.make_async_copy(qbuf.at[pl.ds(pl.multiple_of(r * nq, nq), nq), :],
                                     y_ref.at[pl.ds(pl.multiple_of(d * nq, nq), nq), :], sem_out)

    @pl.when(active & (j == 0))
    def _():
        npad = nsub * MOE_SUB

        def issue(r, c):
            row_copy_in(r).start()
            return c

        def wait(r, c):
            row_copy_in(r).wait()
            return c

        lax.fori_loop(0, npad, issue, 0)
        lax.fori_loop(0, npad, wait, 0)

        def conv(s, c):
            r0 = pl.multiple_of(s * MOE_SUB, MOE_SUB)
            for jj in range(nq):
                xb[pl.ds(r0, MOE_SUB), jj * LANES:(jj + 1) * LANES] = (
                    qbuf[pl.ds(r0 * nq + jj, MOE_SUB, stride=nq), :].astype(bf16))
            acc[pl.ds(r0, MOE_SUB), :] = jnp.zeros((MOE_SUB, acc.shape[1]), f32)
            return c

        lax.fori_loop(0, nsub, conv, 0)

    @pl.when(active)
    def _():
        wgu_b[...] = wgu_ref[0].astype(bf16)
        wd_b[...] = wd_ref[0].astype(bf16)

        def sub(s, c):
            r0 = pl.multiple_of(s * MOE_SUB, MOE_SUB)
            gu = jnp.dot(xb[pl.ds(r0, MOE_SUB), :], wgu_b[...], preferred_element_type=f32) + bgu_ref[0]
            gub = gu.astype(bf16)
            gates, ups = [], []
            for q in range(2 * MOE_TF // 256):
                de = jnp.dot(gub[:, q * 256:(q + 1) * 256], pm_ref[...], preferred_element_type=f32)
                gates.append(de[:, 0:LANES])
                ups.append(de[:, LANES:2 * LANES])
            gate = jnp.minimum(jnp.concatenate(gates, axis=1), SWIGLU_LIMIT)
            up = jnp.clip(jnp.concatenate(ups, axis=1), -SWIGLU_LIMIT, SWIGLU_LIMIT)
            act = (up + 1.0) * gate * (1.0 / (1.0 + jnp.exp(-SWIGLU_ALPHA * gate)))
            acc[pl.ds(r0, MOE_SUB), :] += jnp.dot(act.astype(bf16), wd_b[...], preferred_element_type=f32)
            return c

        lax.fori_loop(0, nsub, sub, 0)

    @pl.when(active & (j == nj - 1))
    def _():
        def stage(s, c):
            r0 = pl.multiple_of(s * MOE_SUB, MOE_SUB)
            a = acc[pl.ds(r0, MOE_SUB), :] + bd_ref[0]
            for jj in range(nq):
                qbuf[pl.ds(r0 * nq + jj, MOE_SUB, stride=nq), :] = a[:, jj * LANES:(jj + 1) * LANES]
            return c

        lax.fori_loop(0, nsub, stage, 0)

        def issue(r, c):
            row_copy_out(r).start()
            return c

        def wait(r, c):
            row_copy_out(r).wait()
            return c

        lax.fori_loop(0, rows, issue, 0)
        lax.fori_loop(0, rows, wait, 0)


def _deinterleave_matrix():
    pm = np.zeros((256, 256), np.float32)
    i = np.arange(LANES)
    pm[2 * i, i] = 1.0
    pm[2 * i + 1, LANES + i] = 1.0
    return jnp.asarray(pm, bf16)


def _moe(hq, eid, w_gate_up, b_gate_up, w_down, b_down, n):
    n_exp, d, f2 = w_gate_up.shape
    dff = f2 // 2
    nq = d // LANES
    nj = dff // MOE_TF
    g_exp, g_rows, tok, dst = _moe_plan(eid, n_exp, MOE_R)
    n_groups = g_exp.shape[0]

    def jeff(g, j, gr):
        return jnp.where(gr[g] > 0, j, nj - 1)

    return pl.pallas_call(
        functools.partial(_moe_kernel, nq=nq, nj=nj),
        grid_spec=pltpu.PrefetchScalarGridSpec(
            num_scalar_prefetch=2,
            grid=(n_groups, nj),
            in_specs=[
                pl.BlockSpec((1, 1, MOE_R), lambda g, j, ge, gr: (g, 0, 0), memory_space=pltpu.SMEM),
                pl.BlockSpec((1, 1, MOE_R), lambda g, j, ge, gr: (g, 0, 0), memory_space=pltpu.SMEM),
                pl.BlockSpec(memory_space=pl.ANY),
                pl.BlockSpec((1, d, 2 * MOE_TF), lambda g, j, ge, gr: (ge[g], 0, jeff(g, j, gr))),
                pl.BlockSpec((1, MOE_TF, d), lambda g, j, ge, gr: (ge[g], jeff(g, j, gr), 0)),
                pl.BlockSpec((1, 1, 2 * MOE_TF), lambda g, j, ge, gr: (ge[g], 0, jeff(g, j, gr))),
                pl.BlockSpec((1, 1, d), lambda g, j, ge, gr: (ge[g], 0, 0)),
                pl.BlockSpec((256, 256), lambda g, j, ge, gr: (0, 0)),
            ],
            out_specs=pl.BlockSpec(memory_space=pl.ANY),
            scratch_shapes=[
                pltpu.VMEM((MOE_R * nq, LANES), f32),
                pltpu.VMEM((MOE_R, d), bf16),
                pltpu.VMEM((MOE_R, d), f32),
                pltpu.VMEM((d, 2 * MOE_TF), bf16),
                pltpu.VMEM((MOE_TF, d), bf16),
                pltpu.SemaphoreType.DMA(()),
                pltpu.SemaphoreType.DMA(()),
            ],
        ),
        out_shape=jax.ShapeDtypeStruct((TOP_K * n * nq, LANES), f32),
        compiler_params=_cparams(("arbitrary", "arbitrary")),
        name="moe",
    )(g_exp, g_rows, tok, dst, hq, w_gate_up, w_down, b_gate_up.reshape(n_exp, 1, f2),
      b_down.reshape(n_exp, 1, d), _deinterleave_matrix())


def _final_kernel(h_ref, y0_ref, y1_ref, y2_ref, y3_ref, gate_ref, g2_ref, b2_ref, o_ref, *, tm, nq, alpha):
    gates = gate_ref[...]
    f = jnp.zeros(h_ref.shape, f32)
    for k, y_ref in enumerate((y0_ref, y1_ref, y2_ref, y3_ref)):
        yk = jnp.concatenate([y_ref[pl.ds(jj, tm, stride=nq), :] for jj in range(nq)], axis=1)
        f = f + gates[:, k:k + 1] * yk
    z = alpha * h_ref[...] + f
    mu = jnp.mean(z, axis=-1, keepdims=True)
    var = jnp.mean(jnp.square(z - mu), axis=-1, keepdims=True)
    o_ref[...] = (z - mu) * lax.rsqrt(var + LN_EPS) * g2_ref[...] + b2_ref[...]


def _final(h, y4q, gates, ln2_g, ln2_b, alpha):
    n, d = h.shape
    nq = d // LANES
    tm = _pick(n, (256, 128, 64, 32, 16, 8))
    nb = n // tm
    row = pl.BlockSpec((1, d), lambda i: (0, 0))
    yspec = lambda k: pl.BlockSpec((tm * nq, LANES), lambda i, k=k: (k * nb + i, 0))
    return pl.pallas_call(
        functools.partial(_final_kernel, tm=tm, nq=nq, alpha=alpha),
        grid=(nb,),
        in_specs=[pl.BlockSpec((tm, d), lambda i: (i, 0)), yspec(0), yspec(1), yspec(2), yspec(3),
                  pl.BlockSpec((tm, TOP_K), lambda i: (i, 0)), row, row],
        out_specs=pl.BlockSpec((tm, d), lambda i: (i, 0)),
        out_shape=jax.ShapeDtypeStruct((n, d), f32),
        compiler_params=_cparams(("arbitrary",)),
        name="final",
    )(h, y4q, y4q, y4q, y4q, gates, ln2_g.reshape(1, d), ln2_b.reshape(1, d))


def kernel(x_prompt, x_sample, cache_k, cache_v, cache_idx_k, state_ret, page_table, w_in, w_o, ret_gn_w,
           ln1_g, ln1_b, w_router, b_router, w_gate_up, b_gate_up, w_down, b_down, ln2_g, ln2_b):
    depth = w_in.shape[0]
    assert depth == 1, "single-layer step"
    bp, t, d = x_prompt.shape
    bd, tn, _ = x_sample.shape
    assert bp == 1
    past = page_table.shape[1] * PAGE_SIZE
    np_, ns = bp * t, bd * tn
    n = np_ + ns
    alpha = (2 * depth) ** 0.25
    l = 0

    x_all = jnp.concatenate([x_prompt.reshape(np_, d), x_sample.reshape(ns, d)], axis=0)
    pos_all = jnp.concatenate([jnp.arange(t), jnp.tile(past + jnp.arange(tn), bd)])
    u_f, u_b = _project(x_all, pos_all, w_in[l])

    ik_b = u_b[:np_, C_TAIL:C_TAIL + IDX_DIM]
    attn_p = _dsa_prompt(u_f, u_b, jnp.concatenate([ik_b, ik_b], axis=1), t)
    attn_s = _dsa_sample(u_f, np_, bd, tn, cache_k[l], cache_v[l], cache_idx_k[l], page_table)

    zero_state = jnp.zeros((bp, N_HEADS_B, QK_DIM_B, V_DIM_B), f32)
    rb_p, s_p = _retention(u_f, 0, bp, t, zero_state, ret_gn_w[l], bf16)
    rb_s, s_s = _retention(u_f, np_, bd, tn, state_ret[l], ret_gn_w[l], f32)

    attn_all = jnp.concatenate([attn_p, attn_s.astype(bf16)], axis=0)
    rb_all = jnp.concatenate([rb_p, rb_s.astype(bf16)], axis=0)
    h, hq, eid, gates = _tail1(attn_all, rb_all, x_all, w_o[l], ln1_g[l], ln1_b[l], w_router[l], b_router[l], alpha)
    y4q = _moe(hq, eid, w_gate_up[l], b_gate_up[l], w_down[l], b_down[l], n)
    y = _final(h, y4q, gates, ln2_g[l], ln2_b[l], alpha)

    wkv = N_KV_A * HEAD_DIM_A
    kv = lambda rows, c0, lead: u_f[rows, c0:c0 + wkv].reshape(lead + (N_KV_A, HEAD_DIM_A))[None]
    ps, ss = slice(0, np_), slice(np_, n)
    return (
        y[:np_].reshape(bp, t, d), y[np_:].reshape(bd, tn, d),
        kv(ps, C_KA, (bp, t)), kv(ps, C_VA, (bp, t)), u_f[ps, C_TAIL:C_TAIL + IDX_DIM].reshape(1, bp, t, IDX_DIM),
        s_p[None],
        kv(ss, C_KA, (bd, tn)), kv(ss, C_VA, (bd, tn)), u_f[ss, C_TAIL:C_TAIL + IDX_DIM].reshape(1, bd, tn, IDX_DIM),
        s_s[None],
    )
```

```python
import functools

import numpy as np
import jax
import jax.numpy as jnp
from jax import lax
from jax.experimental import pallas as pl
from jax.experimental.pallas import tpu as pltpu

f32 = jnp.float32
bf16 = jnp.bfloat16
i32 = jnp.int32

PAGE_SIZE = 128
HEAD_DIM_A = 128
N_HEADS_A = 8
N_KV_A = 4
ROPE_DIM_A = 32
ROPE_THETA = 500000.0
IDX_HEADS = 16
IDX_DIM = 64
IDX_ROPE_DIM = 16
TOPK_MAX = 256
V_DIM_B = 128
N_HEADS_B = 8
QK_DIM_B = 64
RET_CHUNK = 128
RET_THETA = 10000.0
TOP_K = 4
SWIGLU_LIMIT = 7.0
SWIGLU_ALPHA = 1.702
LN_EPS = 1e-5

LANES = 128
VMEM_LIMIT = 56 * 1024 * 1024

PROJ_TN = 512
C_QA, C_KA, C_VA, C_IQ, C_QB, C_KB, C_VB, C_GB, C_TAIL, PROJ_W = 0, 1024, 1536, 2048, 3072, 3584, 4096, 5120, 6144, 6656
PROJ_TILE_TYPES = (6, 6, 1, 0, 2, 2, 3, 4, 0, 0, 0, 0, 5)
QA_SCALE = HEAD_DIM_A ** -0.5 * 1.4426950408889634

INT_MIN = -2 ** 31
KEY_NEG_INF = -2139095041
NEG_BIG = -1e30

NT_DIMS = (((1,), (1,)), ((), ()))
TN_DIMS = (((0,), (0,)), ((), ()))


def _pick(n, cands):
    for c in cands:
        if n % c == 0:
            return c
    raise ValueError(f"no tile for {n}")


def _cparams(sem, vmem=VMEM_LIMIT):
    return pltpu.CompilerParams(dimension_semantics=sem, vmem_limit_bytes=vmem)


def _rope_table(pos, rot_dim, theta, period, scale=1.0, active=LANES):
    half = rot_dim // 2
    inv_freq = 1.0 / (theta ** (jnp.arange(half, dtype=f32) / half))
    ang = pos.astype(f32)[:, None] * inv_freq[None, :]
    cos, sin = jnp.cos(ang), jnp.sin(ang)
    lane = np.arange(LANES)
    d = lane % period
    first = (d < half) & (lane < active)
    second = (d >= half) & (d < rot_dim) & (lane < active)
    idx = np.where(first, d, np.where(second, d - half, 0))
    cos_l, sin_l = cos[:, idx], sin[:, idx]
    c = jnp.where(first | second, cos_l, 1.0)
    s1 = jnp.where(second, sin_l, 0.0)
    s2 = jnp.where(first, -sin_l, 0.0)
    return jnp.concatenate([c, s1, s2], axis=1) * scale


def _proj_kernel(tt_ref, x_ref, w_ref, tab_ref, of_ref, ob_ref, xb_scr):
    j = pl.program_id(1)

    @pl.when(j == 0)
    def _():
        xb_scr[...] = x_ref[...].astype(bf16)

    u = jnp.dot(xb_scr[...], w_ref[...], preferred_element_type=f32)
    t = tt_ref[j]

    def store(v):
        of_ref[...] = v
        ob_ref[...] = v.astype(bf16)

    @pl.when(t == 0)
    def _():
        store(u)

    def rope(half):
        c = tab_ref[0, :, 0:LANES]
        s1 = tab_ref[0, :, LANES:2 * LANES]
        s2 = tab_ref[0, :, 2 * LANES:3 * LANES]
        outs = []
        for q in range(PROJ_TN // LANES):
            uc = u[:, q * LANES:(q + 1) * LANES]
            outs.append(uc * c + pltpu.roll(uc, half, 1) * s1 + pltpu.roll(uc, LANES - half, 1) * s2)
        store(jnp.concatenate(outs, axis=1))

    @pl.when((t == 1) | (t == 6))
    def _():
        rope(ROPE_DIM_A // 2)

    @pl.when((t == 2) | (t == 5))
    def _():
        rope(IDX_ROPE_DIM // 2)

    @pl.when((t == 3) | (t == 4))
    def _():
        rope(QK_DIM_B // 2)


def _project(x_all, pos_all, w_in):
    n, d = x_all.shape
    tm = _pick(n, (768, 512, 384, 256, 128, 64, 32, 16, 8))
    o = np.cumsum((0, 1024, 512, 512, 1024, 64, 16, 512, 512, 1024, 1024))
    wp = jnp.concatenate([w_in[:, o[0]:o[4]], w_in[:, o[6]:o[10]], w_in[:, o[4]:o[6]],
                          jnp.zeros((d, PROJ_W - C_TAIL - 80), w_in.dtype)], axis=1).astype(bf16)
    tabs = jnp.stack([
        _rope_table(pos_all, ROPE_DIM_A, ROPE_THETA, HEAD_DIM_A),
        _rope_table(pos_all, ROPE_DIM_A, ROPE_THETA, HEAD_DIM_A),
        _rope_table(pos_all, IDX_ROPE_DIM, ROPE_THETA, IDX_DIM),
        _rope_table(pos_all, QK_DIM_B, RET_THETA, QK_DIM_B),
        _rope_table(pos_all, QK_DIM_B, RET_THETA, QK_DIM_B, scale=QK_DIM_B ** -0.5),
        _rope_table(pos_all, IDX_ROPE_DIM, ROPE_THETA, IDX_DIM, active=IDX_DIM),
        _rope_table(pos_all, ROPE_DIM_A, ROPE_THETA, HEAD_DIM_A, scale=QA_SCALE),
    ])
    tt = jnp.asarray(PROJ_TILE_TYPES, i32)
    nj = PROJ_W // PROJ_TN
    return pl.pallas_call(
        _proj_kernel,
        grid_spec=pltpu.PrefetchScalarGridSpec(
            num_scalar_prefetch=1,
            grid=(n // tm, nj),
            in_specs=[
                pl.BlockSpec((tm, d), lambda i, j, tt: (i, 0)),
                pl.BlockSpec((d, PROJ_TN), lambda i, j, tt: (0, j)),
                pl.BlockSpec((1, tm, 3 * LANES), lambda i, j, tt: (tt[j], i, 0)),
            ],
            out_specs=[
                pl.BlockSpec((tm, PROJ_TN), lambda i, j, tt: (i, j)),
                pl.BlockSpec((tm, PROJ_TN), lambda i, j, tt: (i, j)),
            ],
            scratch_shapes=[pltpu.VMEM((tm, d), bf16)],
        ),
        out_shape=[jax.ShapeDtypeStruct((n, PROJ_W), f32), jax.ShapeDtypeStruct((n, PROJ_W), bf16)],
        compiler_params=_cparams(("arbitrary", "arbitrary")),
        name="proj",
    )(tt, x_all, wp, tabs)


def _key_to_float(key):
    return pltpu.bitcast(key ^ ((key >> 31) & 0x7FFFFFFF), f32)


def _kth_threshold(count_ge, rows, k):
    def body(step, ans):
        cand = ans + jnp.left_shift(jnp.int32(1), 31 - step)
        return jnp.where(count_ge(_key_to_float(cand)) >= k, cand, ans)

    ans = lax.fori_loop(0, 32, body, jnp.full((rows, 1), INT_MIN, i32))
    return _key_to_float(jnp.maximum(ans, KEY_NEG_INF + 1))


def _lane_blocks(x):
    return [x[:, i * LANES:(i + 1) * LANES] for i in range(x.shape[1] // LANES)]


def _flash_update(qs, ks, vs, bias, m_scr, l_scr, acc_scr, batched):
    n = len(qs)
    score = lambda g: lax.dot_general(qs[g], ks[g], NT_DIMS, preferred_element_type=f32) + bias

    def update(g, sm):
        m_old = m_scr[g]
        m_new = jnp.maximum(m_old, jnp.max(functools.reduce(jnp.maximum, _lane_blocks(sm)), axis=1, keepdims=True))
        alpha = jnp.exp2(m_old - m_new)
        p = jnp.exp2(sm - m_new)
        l_scr[g] = alpha * l_scr[g] + jnp.sum(functools.reduce(jnp.add, _lane_blocks(p)), axis=1, keepdims=True)
        m_scr[g] = m_new
        acc_scr[g] = alpha * acc_scr[g] + jnp.dot(p.astype(bf16), vs[g], preferred_element_type=f32)

    if batched:
        sms = [score(g) for g in range(n)]
        for g in range(n):
            update(g, sms[g])
    else:
        for g in range(n):
            update(g, score(g))


def _dsa_prompt_kernel(iq_ref, iw_ref, qa_ref, ikd_ref, ka_ref, va_ref, o_ref,
                       qst, wst, qg, skey, m_scr, l_scr, acc_scr, *, tq, tk, topk):
    i = pl.program_id(0)
    n_chunks = (i * tq + tq + tk - 1) // tk
    lane = lax.broadcasted_iota(i32, (tq, LANES), 1)
    nrep = tk // LANES

    for p in range(IDX_HEADS // 2):
        blk = iq_ref[:, p * LANES:(p + 1) * LANES]
        qst[(2 * p) * tq:(2 * p + 1) * tq, :] = jnp.where(lane < IDX_DIM, blk, jnp.zeros_like(blk))
        qst[(2 * p + 1) * tq:(2 * p + 2) * tq, :] = jnp.where(lane >= IDX_DIM, blk, jnp.zeros_like(blk))
    w = iw_ref[...]
    wscale = IDX_DIM ** -0.5 * IDX_HEADS ** -0.5
    for h in range(IDX_HEADS):
        wst[h * tq:(h + 1) * tq, :] = jnp.broadcast_to(w[:, IDX_DIM + h:IDX_DIM + h + 1] * wscale, (tq, LANES))

    qpos = i * tq + lax.broadcasted_iota(i32, (tq, tk), 0)

    def score_body(c, carry):
        k0 = pl.multiple_of(c * tk, tk)
        kd = ikd_ref[pl.ds(k0, tk), :]
        logits = lax.dot_general(qst[...], kd, NT_DIMS, preferred_element_type=f32)
        acc = jnp.zeros((tq, tk), f32)
        for h in range(IDX_HEADS):
            wh = wst[h * tq:(h + 1) * tq, :]
            acc = acc + jnp.maximum(logits[h * tq:(h + 1) * tq, :], 0.0) * jnp.concatenate([wh] * nrep, axis=1)
        kpos = k0 + lax.broadcasted_iota(i32, (tq, tk), 1)
        skey[c] = jnp.where(kpos <= qpos, acc, -jnp.inf)
        return carry

    lax.fori_loop(0, n_chunks, score_body, 0)

    def count_ge(cand):
        cb = jnp.broadcast_to(cand, (tq, LANES))

        def body(c, acc):
            blk = skey[c]
            for q in range(nrep):
                acc = acc + jnp.where(blk[:, q * LANES:(q + 1) * LANES] >= cb, 1.0, 0.0)
            return acc

        acc = lax.fori_loop(0, n_chunks, body, jnp.zeros((tq, LANES), f32))
        return jnp.sum(acc, axis=1, keepdims=True)

    thr = _kth_threshold(count_ge, tq, float(topk))
    thr2 = jnp.concatenate([thr, thr], axis=0)

    for g in range(N_KV_A):
        qg[g, 0:tq, :] = qa_ref[:, (2 * g) * LANES:(2 * g + 1) * LANES]
        qg[g, tq:2 * tq, :] = qa_ref[:, (2 * g + 1) * LANES:(2 * g + 2) * LANES]
    m_scr[...] = jnp.full(m_scr.shape, NEG_BIG, f32)
    l_scr[...] = jnp.zeros(l_scr.shape, f32)
    acc_scr[...] = jnp.zeros(acc_scr.shape, f32)

    def att_body(c, carry):
        k0 = pl.multiple_of(c * tk, tk)
        sk = skey[c]
        bias = jnp.where(jnp.concatenate([sk, sk], axis=0) >= thr2, 0.0, NEG_BIG)
        _flash_update([qg[g] for g in range(N_KV_A)],
                      [ka_ref[pl.ds(k0, tk), g * LANES:(g + 1) * LANES] for g in range(N_KV_A)],
                      [va_ref[pl.ds(k0, tk), g * LANES:(g + 1) * LANES] for g in range(N_KV_A)],
                      bias, m_scr, l_scr, acc_scr, batched=False)
        return carry

    lax.fori_loop(0, n_chunks, att_body, 0)

    for g in range(N_KV_A):
        o = acc_scr[g] / l_scr[g]
        o_ref[:, (2 * g) * LANES:(2 * g + 1) * LANES] = o[0:tq].astype(o_ref.dtype)
        o_ref[:, (2 * g + 1) * LANES:(2 * g + 2) * LANES] = o[tq:2 * tq].astype(o_ref.dtype)


def _dsa_prompt(u_f, u_b, ikd, t):
    tq = _pick(t, (128,))
    tk = _pick(t, (512, 256, 128))
    topk = min(TOPK_MAX, t // 4)
    wa = N_HEADS_A * HEAD_DIM_A
    wkv = N_KV_A * HEAD_DIM_A
    kern = functools.partial(_dsa_prompt_kernel, tq=tq, tk=tk, topk=topk)
    one = pl.Buffered(1)
    return pl.pallas_call(
        kern,
        grid=(t // tq,),
        in_specs=[
            pl.BlockSpec((tq, IDX_HEADS * IDX_DIM), lambda i: (i, C_IQ // (IDX_HEADS * IDX_DIM))),
            pl.BlockSpec((tq, LANES), lambda i: (i, C_TAIL // LANES)),
            pl.BlockSpec((tq, wa), lambda i: (i, C_QA // wa)),
            pl.BlockSpec((t, LANES), lambda i: (0, 0), pipeline_mode=one),
            pl.BlockSpec((t, wkv), lambda i: (0, C_KA // wkv), pipeline_mode=one),
            pl.BlockSpec((t, wkv), lambda i: (0, C_VA // wkv), pipeline_mode=one),
        ],
        out_specs=pl.BlockSpec((tq, wa), lambda i: (i, 0)),
        out_shape=jax.ShapeDtypeStruct((t, wa), bf16),
        scratch_shapes=[
            pltpu.VMEM((IDX_HEADS * tq, LANES), bf16),
            pltpu.VMEM((IDX_HEADS * tq, LANES), f32),
            pltpu.VMEM((N_KV_A, 2 * tq, LANES), bf16),
            pltpu.VMEM((t // tk, tq, tk), f32),
            pltpu.VMEM((N_KV_A, 2 * tq, 1), f32),
            pltpu.VMEM((N_KV_A, 2 * tq, 1), f32),
            pltpu.VMEM((N_KV_A, 2 * tq, LANES), f32),
        ],
        compiler_params=_cparams(("arbitrary",)),
        name="dsa_prompt",
    )(u_b, u_f, u_b, ikd, u_b, u_b)


def _dsa_sample_score_kernel(pt_ref, *refs, tn, ppg, topk, n_groups):
    pages = refs[:ppg]
    iq_ref, tail_ref, sc_ref, thr_ref, qs, wst = refs[ppg:]
    c = pl.program_id(1)
    wscale = IDX_DIM ** -0.5 * IDX_HEADS ** -0.5
    kw = ppg * PAGE_SIZE

    @pl.when(c == 0)
    def _():
        iq = iq_ref[...]
        w = tail_ref[...]
        for h in range(IDX_HEADS):
            qs[h * tn:(h + 1) * tn, :] = iq[:, h * IDX_DIM:(h + 1) * IDX_DIM]
            wst[h * tn:(h + 1) * tn, :] = jnp.broadcast_to(w[:, IDX_DIM + h:IDX_DIM + h + 1] * wscale, (tn, LANES))

    def head_sum(logits):
        width = logits.shape[1]
        acc = jnp.zeros((tn, width), f32)
        for h in range(IDX_HEADS):
            wh = wst[h * tn:(h + 1) * tn, :]
            acc = acc + jnp.maximum(logits[h * tn:(h + 1) * tn, :], 0.0) * jnp.concatenate([wh] * (width // LANES), axis=1)
        return acc

    keys_t = jnp.concatenate([p[...] for p in pages], axis=1).astype(bf16)
    sc_ref[0, c] = head_sum(jnp.dot(qs[...].astype(bf16), keys_t, preferred_element_type=f32))

    @pl.when(c == n_groups - 1)
    def _():
        ik_new = tail_ref[:, 0:IDX_DIM].astype(bf16)
        kpad = jnp.concatenate([ik_new, jnp.zeros((LANES - tn, IDX_DIM), bf16)], axis=0)
        s_new = head_sum(lax.dot_general(qs[...].astype(bf16), kpad, NT_DIMS, preferred_element_type=f32))
        qi = lax.broadcasted_iota(i32, (tn, LANES), 0)
        kj = lax.broadcasted_iota(i32, (tn, LANES), 1)
        snew = jnp.where(kj <= qi, s_new, -jnp.inf)
        sc_ref[0, n_groups] = jnp.concatenate([snew, jnp.full((tn, kw - LANES), -jnp.inf, f32)], axis=1)

        def count_ge(cand):
            cb = jnp.broadcast_to(cand, (tn, LANES))

            def body(cc, acc):
                blk = sc_ref[0, cc]
                for q in range(kw // LANES):
                    acc = acc + jnp.where(blk[:, q * LANES:(q + 1) * LANES] >= cb, 1.0, 0.0)
                return acc

            acc = lax.fori_loop(0, n_groups + 1, body, jnp.zeros((tn, LANES), f32))
            return jnp.sum(acc, axis=1, keepdims=True)

        thr = _kth_threshold(count_ge, tn, float(topk))
        thr_ref[0] = jnp.broadcast_to(thr, (tn, LANES))


def _dsa_sample_attn_kernel(pt_ref, *refs, tn, ppg, n_groups):
    kpages = refs[:ppg]
    vpages = refs[ppg:2 * ppg]
    qa_ref, kn_ref, vn_ref, sc_ref, scn_ref, thr_ref, o_ref, qg, m_scr, l_scr, acc_scr = refs[2 * ppg:]
    c = pl.program_id(1)

    @pl.when(c == 0)
    def _():
        qa = qa_ref[...]
        for g in range(N_KV_A):
            qg[g, 0:tn, :] = qa[:, (2 * g) * LANES:(2 * g + 1) * LANES]
            qg[g, tn:2 * tn, :] = qa[:, (2 * g + 1) * LANES:(2 * g + 2) * LANES]
        m_scr[...] = jnp.full(m_scr.shape, NEG_BIG, f32)
        l_scr[...] = jnp.zeros(l_scr.shape, f32)
        acc_scr[...] = jnp.zeros(acc_scr.shape, f32)

    thr = thr_ref[0][:, 0:1]
    thr2 = jnp.concatenate([thr, thr], axis=0)

    def attend(ks, vs, sc):
        bias = jnp.where(jnp.concatenate([sc, sc], axis=0) >= thr2, 0.0, NEG_BIG)
        _flash_update([qg[g].astype(bf16) for g in range(N_KV_A)], ks, vs, bias, m_scr, l_scr, acc_scr, batched=True)

    group = lambda pages, g: jnp.concatenate(
        [p[pl.ds(g, PAGE_SIZE, stride=N_KV_A), :] for p in pages], axis=0).astype(bf16)
    attend([group(kpages, g) for g in range(N_KV_A)], [group(vpages, g) for g in range(N_KV_A)], sc_ref[0, 0])

    @pl.when(c == n_groups - 1)
    def _():
        zpad = jnp.zeros((LANES - tn, N_KV_A * HEAD_DIM_A), bf16)
        attend(_lane_blocks(jnp.concatenate([kn_ref[...].astype(bf16), zpad], axis=0)),
               _lane_blocks(jnp.concatenate([vn_ref[...].astype(bf16), zpad], axis=0)),
               scn_ref[0, 0][:, 0:LANES])
        for g in range(N_KV_A):
            o = acc_scr[g] / l_scr[g]
            o_ref[:, (2 * g) * LANES:(2 * g + 1) * LANES] = o[0:tn]
            o_ref[:, (2 * g + 1) * LANES:(2 * g + 2) * LANES] = o[tn:2 * tn]


def _dsa_sample(u_f, row0, bd, tn, cache_k, cache_v, idx_kt, page_table):
    n_pages = page_table.shape[1]
    past = n_pages * PAGE_SIZE
    topk = min(TOPK_MAX, (past + tn) // 4)
    ppg = _pick(n_pages, (16, 8, 4, 2, 1))
    ng = n_pages // ppg
    kw = ppg * PAGE_SIZE
    wkv = N_KV_A * HEAD_DIM_A
    wa = N_HEADS_A * HEAD_DIM_A
    rb0 = row0 // tn

    def idx_spec(j):
        return pl.BlockSpec((None, IDX_DIM, PAGE_SIZE), lambda b, c, pt, j=j: (pt[b, c * ppg + j], 0, 0))

    def kv_spec(j):
        return pl.BlockSpec((None, PAGE_SIZE * N_KV_A, HEAD_DIM_A), lambda b, c, pt, j=j: (pt[b, c * ppg + j], 0, 0))

    score = pl.pallas_call(
        functools.partial(_dsa_sample_score_kernel, tn=tn, ppg=ppg, topk=topk, n_groups=ng),
        grid_spec=pltpu.PrefetchScalarGridSpec(
            num_scalar_prefetch=1,
            grid=(bd, ng),
            in_specs=[idx_spec(j) for j in range(ppg)] + [
                pl.BlockSpec((tn, IDX_HEADS * IDX_DIM), lambda b, c, pt: (rb0 + b, C_IQ // (IDX_HEADS * IDX_DIM))),
                pl.BlockSpec((tn, LANES), lambda b, c, pt: (rb0 + b, C_TAIL // LANES)),
            ],
            out_specs=[
                pl.BlockSpec((1, ng + 1, tn, kw), lambda b, c, pt: (b, 0, 0, 0)),
                pl.BlockSpec((1, tn, LANES), lambda b, c, pt: (b, 0, 0)),
            ],
            scratch_shapes=[pltpu.VMEM((IDX_HEADS * tn, IDX_DIM), f32), pltpu.VMEM((IDX_HEADS * tn, LANES), f32)],
        ),
        out_shape=[jax.ShapeDtypeStruct((bd, ng + 1, tn, kw), f32), jax.ShapeDtypeStruct((bd, tn, LANES), f32)],
        compiler_params=_cparams(("arbitrary", "arbitrary")),
        name="dsa_sample_score",
    )
    scores, thr = score(page_table, *([idx_kt] * ppg), u_f, u_f)

    attn = pl.pallas_call(
        functools.partial(_dsa_sample_attn_kernel, tn=tn, ppg=ppg, n_groups=ng),
        grid_spec=pltpu.PrefetchScalarGridSpec(
            num_scalar_prefetch=1,
            grid=(bd, ng),
            in_specs=[kv_spec(j) for j in range(ppg)] + [kv_spec(j) for j in range(ppg)] + [
                pl.BlockSpec((tn, wa), lambda b, c, pt: (rb0 + b, C_QA // wa)),
                pl.BlockSpec((tn, wkv), lambda b, c, pt: (rb0 + b, C_KA // wkv)),
                pl.BlockSpec((tn, wkv), lambda b, c, pt: (rb0 + b, C_VA // wkv)),
                pl.BlockSpec((1, 1, tn, kw), lambda b, c, pt: (b, c, 0, 0)),
                pl.BlockSpec((1, 1, tn, kw), lambda b, c, pt: (b, ng, 0, 0)),
                pl.BlockSpec((1, tn, LANES), lambda b, c, pt: (b, 0, 0)),
            ],
            out_specs=pl.BlockSpec((tn, wa), lambda b, c, pt: (b, 0)),
            scratch_shapes=[
                pltpu.VMEM((N_KV_A, 2 * tn, LANES), f32),
                pltpu.VMEM((N_KV_A, 2 * tn, 1), f32),
                pltpu.VMEM((N_KV_A, 2 * tn, 1), f32),
                pltpu.VMEM((N_KV_A, 2 * tn, LANES), f32),
            ],
        ),
        out_shape=jax.ShapeDtypeStruct((bd * tn, wa), f32),
        compiler_params=_cparams(("arbitrary", "arbitrary")),
        name="dsa_sample_attn",
    )
    return attn(page_table, *([cache_k] * ppg), *([cache_v] * ppg), u_f, u_f, u_f, scores, scores, thr)


def _ret_kernel(q_ref, k_ref, v_ref, g_ref, gnw_ref, dmask_ref, qdec_ref, kdec_ref, sdec_ref, s0_ref,
                o_ref, sout_ref, s_scr, *, n_chunks):
    c = pl.program_id(1)

    @pl.when(c == 0)
    def _():
        s_scr[...] = s0_ref[0]

    for h in range(N_HEADS_B):
        q = q_ref[:, h * QK_DIM_B:(h + 1) * QK_DIM_B]
        k = k_ref[:, h * QK_DIM_B:(h + 1) * QK_DIM_B]
        v = v_ref[:, h * V_DIM_B:(h + 1) * V_DIM_B].astype(bf16)
        gate = g_ref[:, h * V_DIM_B:(h + 1) * V_DIM_B]
        qb = q.astype(bf16)
        att = lax.dot_general(qb, k.astype(bf16), NT_DIMS, preferred_element_type=f32) * dmask_ref[h]
        s_old = s_scr[h]
        o = (jnp.dot(att.astype(bf16), v, preferred_element_type=f32)
             + jnp.dot(qb, s_old.astype(bf16), preferred_element_type=f32) * qdec_ref[h])
        kd = (k * kdec_ref[h]).astype(bf16)
        s_scr[h] = s_old * sdec_ref[h] + lax.dot_general(kd, v, TN_DIMS, preferred_element_type=f32)
        mu = jnp.mean(o, axis=-1, keepdims=True)
        var = jnp.mean(jnp.square(o - mu), axis=-1, keepdims=True)
        rb = (o - mu) * lax.rsqrt(var + LN_EPS) * gnw_ref[:, h * V_DIM_B:(h + 1) * V_DIM_B]
        rb = rb * (gate / (1.0 + jnp.exp(-gate)))
        o_ref[:, h * V_DIM_B:(h + 1) * V_DIM_B] = rb.astype(o_ref.dtype)

    @pl.when(c == n_chunks - 1)
    def _():
        sout_ref[0] = s_scr[...]


def _retention(u_f, row0, nb, t, state0, gn_w, out_dtype):
    ch = min(RET_CHUNK, t)
    if t % ch:
        ch = t
    n = t // ch
    hb = N_HEADS_B
    lg = jnp.log1p(-jnp.exp2(-5.0 - jnp.arange(hb, dtype=f32)))
    i = jnp.arange(ch)
    diff = i[:, None] - i[None, :]
    dmask = jnp.where(diff >= 0, jnp.exp(lg[:, None, None] * jnp.maximum(diff, 0)), 0.0)
    qdec = jnp.broadcast_to(jnp.exp(lg[:, None] * (i + 1))[:, :, None], (hb, ch, V_DIM_B))
    kdec = jnp.broadcast_to(jnp.exp(lg[:, None] * (ch - 1 - i))[:, :, None], (hb, ch, QK_DIM_B))
    sdec = jnp.broadcast_to(jnp.exp(lg * ch)[:, None, None], (hb, 1, V_DIM_B))
    wqk = hb * QK_DIM_B
    wv = hb * V_DIM_B
    rb0 = row0 // ch
    full3 = lambda shp: pl.BlockSpec(shp, lambda b, c: (0, 0, 0))
    return pl.pallas_call(
        functools.partial(_ret_kernel, n_chunks=n),
        grid=(nb, n),
        in_specs=[
            pl.BlockSpec((ch, wqk), lambda b, c: (rb0 + b * n + c, C_QB // wqk)),
            pl.BlockSpec((ch, wqk), lambda b, c: (rb0 + b * n + c, C_KB // wqk)),
            pl.BlockSpec((ch, wv), lambda b, c: (rb0 + b * n + c, C_VB // wv)),
            pl.BlockSpec((ch, wv), lambda b, c: (rb0 + b * n + c, C_GB // wv)),
            pl.BlockSpec((1, wv), lambda b, c: (0, 0)),
            full3((hb, ch, ch)), full3((hb, ch, V_DIM_B)), full3((hb, ch, QK_DIM_B)), full3((hb, 1, V_DIM_B)),
            pl.BlockSpec((1, hb, QK_DIM_B, V_DIM_B), lambda b, c: (b, 0, 0, 0)),
        ],
        out_specs=[
            pl.BlockSpec((ch, wv), lambda b, c: (b * n + c, 0)),
            pl.BlockSpec((1, hb, QK_DIM_B, V_DIM_B), lambda b, c: (b, 0, 0, 0)),
        ],
        out_shape=[jax.ShapeDtypeStruct((nb * t, wv), out_dtype),
                   jax.ShapeDtypeStruct((nb, hb, QK_DIM_B, V_DIM_B), f32)],
        scratch_shapes=[pltpu.VMEM((hb, QK_DIM_B, V_DIM_B), f32)],
        compiler_params=_cparams(("arbitrary", "arbitrary")),
        name="retention",
    )(u_f, u_f, u_f, u_f, gn_w.reshape(1, wv), dmask, qdec, kdec, sdec, state0)


def _split_hi_lo(a):
    hi = a.astype(bf16)
    return hi, (a - hi.astype(f32)).astype(bf16)


def _tail1_kernel(attn_ref, rb_ref, x_ref, wo_ref, g1_ref, b1_ref, wrh_ref, wrl_ref, br_ref,
                  h_ref, hq_ref, eid_ref, gate_ref, *, tm, n_exp, alpha, wa):
    mix = (jnp.dot(attn_ref[...], wo_ref[0:wa, :], preferred_element_type=f32)
           + jnp.dot(rb_ref[...], wo_ref[wa:, :], preferred_element_type=f32))
    z = alpha * x_ref[...] + mix
    mu = jnp.mean(z, axis=-1, keepdims=True)
    var = jnp.mean(jnp.square(z - mu), axis=-1, keepdims=True)
    h = (z - mu) * lax.rsqrt(var + LN_EPS) * g1_ref[...] + b1_ref[...]
    h_ref[...] = h
    nq = h.shape[1] // LANES
    for j in range(nq):
        hq_ref[pl.ds(j, tm, stride=nq), :] = h[:, j * LANES:(j + 1) * LANES]

    hh, hl = _split_hi_lo(h)
    logits = (jnp.dot(hh, wrh_ref[...], preferred_element_type=f32)
              + jnp.dot(hl, wrh_ref[...], preferred_element_type=f32)
              + jnp.dot(hh, wrl_ref[...], preferred_element_type=f32)) + br_ref[...]
    lane = lax.broadcasted_iota(i32, (tm, LANES), 1)
    lanef = lane.astype(f32)
    logits = jnp.where(lane < n_exp, logits, -jnp.inf)
    vals, ids = [], []
    for _ in range(TOP_K):
        m = jnp.max(logits, axis=1, keepdims=True)
        idx = jnp.min(jnp.where(logits == m, lanef, float(LANES)), axis=1, keepdims=True)
        vals.append(m)
        ids.append(idx)
        logits = jnp.where(lanef == idx, -jnp.inf, logits)
    es = [jnp.exp(v - vals[0]) for v in vals]
    den = es[0] + es[1] + es[2] + es[3]
    eid_ref[...] = jnp.concatenate(ids, axis=1).astype(i32)
    gate_ref[...] = jnp.concatenate([e / den for e in es], axis=1)


def _tail1(attn_b, rb_b, x_all, w_o, ln1_g, ln1_b, w_router, b_router, alpha):
    n, d = x_all.shape
    wa = attn_b.shape[1]
    n_exp = w_router.shape[1]
    tm = _pick(n, (256, 128, 64, 32, 16, 8))
    nq = d // LANES
    wr = jnp.zeros((d, LANES), f32).at[:, :n_exp].set(w_router)
    wrh, wrl = _split_hi_lo(wr)
    br = jnp.zeros((1, LANES), f32).at[0, :n_exp].set(b_router)
    row = lambda w: pl.BlockSpec((1, w), lambda i: (0, 0))
    return pl.pallas_call(
        functools.partial(_tail1_kernel, tm=tm, n_exp=n_exp, alpha=alpha, wa=wa),
        grid=(n // tm,),
        in_specs=[
            pl.BlockSpec((tm, wa), lambda i: (i, 0)),
            pl.BlockSpec((tm, rb_b.shape[1]), lambda i: (i, 0)),
            pl.BlockSpec((tm, d), lambda i: (i, 0)),
            pl.BlockSpec(w_o.shape, lambda i: (0, 0), pipeline_mode=pl.Buffered(1)),
            row(d), row(d),
            pl.BlockSpec((d, LANES), lambda i: (0, 0)), pl.BlockSpec((d, LANES), lambda i: (0, 0)), row(LANES),
        ],
        out_specs=[
            pl.BlockSpec((tm, d), lambda i: (i, 0)),
            pl.BlockSpec((tm * nq, LANES), lambda i: (i, 0)),
            pl.BlockSpec((tm, TOP_K), lambda i: (i, 0)),
            pl.BlockSpec((tm, TOP_K), lambda i: (i, 0)),
        ],
        out_shape=[jax.ShapeDtypeStruct((n, d), f32), jax.ShapeDtypeStruct((n * nq, LANES), f32),
                   jax.ShapeDtypeStruct((n, TOP_K), i32), jax.ShapeDtypeStruct((n, TOP_K), f32)],
        compiler_params=_cparams(("arbitrary",)),
        name="tail1",
    )(attn_b, rb_b, x_all, w_o.astype(bf16), ln1_g.reshape(1, d), ln1_b.reshape(1, d), wrh, wrl, br)


MOE_R = 2048
MOE_SUB = 256
MOE_TF = 256
MOE_ISSUE_UNROLL = 8


def _moe_plan(eid, n_exp, r_cap):
    n = eid.shape[0]
    p = n * TOP_K
    flat = eid.reshape(p)
    onehot = (flat[:, None] == jnp.arange(n_exp, dtype=i32)[None, :]).astype(i32)
    csum = jnp.cumsum(onehot, axis=0)
    rank = jnp.sum((csum - onehot) * onehot, axis=1)
    counts = csum[-1]
    ngrp = (counts + r_cap - 1) // r_cap
    gend = jnp.cumsum(ngrp)
    gstart = gend - ngrp
    g_of = gstart[flat] + rank // r_cap
    slot = rank % r_cap
    n_groups = n_exp + p // r_cap
    t_idx = jnp.arange(p, dtype=i32) // TOP_K
    k_idx = jnp.arange(p, dtype=i32) % TOP_K
    tok = jnp.zeros((n_groups, r_cap), i32).at[g_of, slot].set(t_idx)
    spare = jnp.broadcast_to(p + jnp.arange(r_cap, dtype=i32) % MOE_SUB, (n_groups, r_cap))
    dst = spare.at[g_of, slot].set(k_idx * n + t_idx)
    gid = jnp.arange(n_groups, dtype=i32)
    total = gend[-1]
    gclamp = jnp.minimum(gid, total - 1)
    g_exp = jnp.sum((gend[None, :] <= gclamp[:, None]).astype(i32), axis=1)
    g_rows = jnp.clip(counts[g_exp] - (gclamp - gstart[g_exp]) * r_cap, 0, r_cap)
    g_rows = jnp.where(gid < total, g_rows, 0).astype(i32)
    return g_exp, g_rows, tok.reshape(n_groups, 1, r_cap), dst.reshape(n_groups, 1, r_cap)


def _moe_kernel(ge_ref, gr_ref, tok_ref, dst_ref, hq_ref, wgu_ref, wd_ref, bgu_ref, bd_ref, pm_ref, y_ref,
                qbuf, xb, acc, wgu_b, wd_b, sem_in, sem_out, *, nq, nj, spare_row0):
    g = pl.program_id(0)
    j = pl.program_id(1)
    rows = gr_ref[g]
    nsub = (rows + MOE_SUB - 1) // MOE_SUB
    active = rows > 0

    def in_copy(s, i, slot):
        t = tok_ref[0, 0, s * MOE_SUB + i]
        return pltpu.make_async_copy(hq_ref.at[pl.ds(pl.multiple_of(t * nq, nq), nq), :],
                                     qbuf.at[slot, pl.ds(pl.multiple_of(i * nq, nq), nq), :], sem_in.at[slot])

    def out_copy(s, i, slot):
        d = dst_ref[0, 0, s * MOE_SUB + i]
        return pltpu.make_async_copy(qbuf.at[slot, pl.ds(pl.multiple_of(i * nq, nq), nq), :],
                                     y_ref.at[pl.ds(pl.multiple_of(d * nq, nq), nq), :], sem_out.at[slot])

    def for_rows(fn):
        def body(b, c):
            for u in range(MOE_ISSUE_UNROLL):
                fn(b * MOE_ISSUE_UNROLL + u)
            return c

        lax.fori_loop(0, MOE_SUB // MOE_ISSUE_UNROLL, body, 0)

    def wait_in(slot):
        pltpu.make_async_copy(hq_ref.at[pl.ds(0, MOE_SUB * nq), :], qbuf.at[slot], sem_in.at[slot]).wait()

    def wait_out(slot):
        pltpu.make_async_copy(qbuf.at[slot], y_ref.at[pl.ds(0, MOE_SUB * nq), :], sem_out.at[slot]).wait()

    def convert(s, slot):
        r0 = pl.multiple_of(s * MOE_SUB, MOE_SUB)
        for jj in range(nq):
            xb[pl.ds(r0, MOE_SUB), jj * LANES:(jj + 1) * LANES] = (
                qbuf[slot, pl.ds(jj, MOE_SUB, stride=nq), :].astype(bf16))
        acc[pl.ds(r0, MOE_SUB), :] = jnp.zeros((MOE_SUB, acc.shape[1]), f32)

    def stage(s, slot):
        r0 = pl.multiple_of(s * MOE_SUB, MOE_SUB)
        a = acc[pl.ds(r0, MOE_SUB), :] + bd_ref[0]
        for jj in range(nq):
            qbuf[slot, pl.ds(jj, MOE_SUB, stride=nq), :] = a[:, jj * LANES:(jj + 1) * LANES]

    npairs = (nsub + 1) // 2

    @pl.when((g == 0) & (j == 0))
    def _():
        qbuf[1] = jnp.zeros(qbuf.shape[1:], f32)
        fill = pltpu.make_async_copy(qbuf.at[1], y_ref.at[pl.ds(spare_row0, MOE_SUB * nq), :], sem_out.at[1])
        fill.start()
        fill.wait()

    @pl.when(active & (j == 0))
    def _():
        for_rows(lambda i: in_copy(0, i, 0).start())

        def pair(pp, c):
            s0 = 2 * pp
            s1 = s0 + 1

            @pl.when(s1 < nsub)
            def _():
                for_rows(lambda i: in_copy(s1, i, 1).start())

            wait_in(0)
            convert(s0, 0)

            @pl.when(s1 < nsub)
            def _():
                @pl.when(s1 + 1 < nsub)
                def _():
                    for_rows(lambda i: in_copy(s1 + 1, i, 0).start())

                wait_in(1)
                convert(s1, 1)

            return c

        lax.fori_loop(0, npairs, pair, 0)

    @pl.when(active)
    def _():
        wgu_b[...] = wgu_ref[0].astype(bf16)
        wd_b[...] = wd_ref[0].astype(bf16)

        def sub(s, c):
            r0 = pl.multiple_of(s * MOE_SUB, MOE_SUB)
            gu = jnp.dot(xb[pl.ds(r0, MOE_SUB), :], wgu_b[...], preferred_element_type=f32) + bgu_ref[0]
            gub = gu.astype(bf16)
            gates, ups = [], []
            for q in range(2 * MOE_TF // 256):
                de = jnp.dot(gub[:, q * 256:(q + 1) * 256], pm_ref[...], preferred_element_type=f32)
                gates.append(de[:, 0:LANES])
                ups.append(de[:, LANES:2 * LANES])
            gate = jnp.minimum(jnp.concatenate(gates, axis=1), SWIGLU_LIMIT)
            up = jnp.clip(jnp.concatenate(ups, axis=1), -SWIGLU_LIMIT, SWIGLU_LIMIT)
            act = (up + 1.0) * gate * (1.0 / (1.0 + jnp.exp(-SWIGLU_ALPHA * gate)))
            acc[pl.ds(r0, MOE_SUB), :] += jnp.dot(act.astype(bf16), wd_b[...], preferred_element_type=f32)
            return c

        lax.fori_loop(0, nsub, sub, 0)

    @pl.when(active & (j == nj - 1))
    def _():
        def pair(pp, c):
            s0 = 2 * pp
            s1 = s0 + 1

            @pl.when(pp > 0)
            def _():
                wait_out(0)

            stage(s0, 0)
            for_rows(lambda i: out_copy(s0, i, 0).start())

            @pl.when(s1 < nsub)
            def _():
                @pl.when(pp > 0)
                def _():
                    wait_out(1)

                stage(s1, 1)
                for_rows(lambda i: out_copy(s1, i, 1).start())

            return c

        lax.fori_loop(0, npairs, pair, 0)
        wait_out(0)

        @pl.when(nsub >= 2)
        def _():
            wait_out(1)


def _deinterleave_matrix():
    pm = np.zeros((256, 256), np.float32)
    i = np.arange(LANES)
    pm[2 * i, i] = 1.0
    pm[2 * i + 1, LANES + i] = 1.0
    return jnp.asarray(pm, bf16)


def _moe(hq, eid, w_gate_up, b_gate_up, w_down, b_down, n):
    n_exp, d, f2 = w_gate_up.shape
    dff = f2 // 2
    nq = d // LANES
    nj = dff // MOE_TF
    g_exp, g_rows, tok, dst = _moe_plan(eid, n_exp, MOE_R)
    n_groups = g_exp.shape[0]

    def jeff(g, j, gr):
        return jnp.where(gr[g] > 0, j, nj - 1)

    return pl.pallas_call(
        functools.partial(_moe_kernel, nq=nq, nj=nj, spare_row0=TOP_K * n * nq),
        grid_spec=pltpu.PrefetchScalarGridSpec(
            num_scalar_prefetch=2,
            grid=(n_groups, nj),
            in_specs=[
                pl.BlockSpec((1, 1, MOE_R), lambda g, j, ge, gr: (g, 0, 0), memory_space=pltpu.SMEM),
                pl.BlockSpec((1, 1, MOE_R), lambda g, j, ge, gr: (g, 0, 0), memory_space=pltpu.SMEM),
                pl.BlockSpec(memory_space=pl.ANY),
                pl.BlockSpec((1, d, 2 * MOE_TF), lambda g, j, ge, gr: (ge[g], 0, jeff(g, j, gr))),
                pl.BlockSpec((1, MOE_TF, d), lambda g, j, ge, gr: (ge[g], jeff(g, j, gr), 0)),
                pl.BlockSpec((1, 1, 2 * MOE_TF), lambda g, j, ge, gr: (ge[g], 0, jeff(g, j, gr))),
                pl.BlockSpec((1, 1, d), lambda g, j, ge, gr: (ge[g], 0, 0)),
                pl.BlockSpec((256, 256), lambda g, j, ge, gr: (0, 0)),
            ],
            out_specs=pl.BlockSpec(memory_space=pl.ANY),
            scratch_shapes=[
                pltpu.VMEM((2, MOE_SUB * nq, LANES), f32),
                pltpu.VMEM((MOE_R, d), bf16),
                pltpu.VMEM((MOE_R, d), f32),
                pltpu.VMEM((d, 2 * MOE_TF), bf16),
                pltpu.VMEM((MOE_TF, d), bf16),
                pltpu.SemaphoreType.DMA((2,)),
                pltpu.SemaphoreType.DMA((2,)),
            ],
        ),
        out_shape=jax.ShapeDtypeStruct(((TOP_K * n + MOE_SUB) * nq, LANES), f32),
        compiler_params=_cparams(("arbitrary", "arbitrary")),
        name="moe",
    )(g_exp, g_rows, tok, dst, hq, w_gate_up, w_down, b_gate_up.reshape(n_exp, 1, f2),
      b_down.reshape(n_exp, 1, d), _deinterleave_matrix())


def _final_kernel(h_ref, y0_ref, y1_ref, y2_ref, y3_ref, gate_ref, g2_ref, b2_ref, o_ref, *, tm, nq, alpha):
    gates = gate_ref[...]
    f = jnp.zeros(h_ref.shape, f32)
    for k, y_ref in enumerate((y0_ref, y1_ref, y2_ref, y3_ref)):
        yk = jnp.concatenate([y_ref[pl.ds(jj, tm, stride=nq), :] for jj in range(nq)], axis=1)
        f = f + gates[:, k:k + 1] * yk
    z = alpha * h_ref[...] + f
    mu = jnp.mean(z, axis=-1, keepdims=True)
    var = jnp.mean(jnp.square(z - mu), axis=-1, keepdims=True)
    o_ref[...] = (z - mu) * lax.rsqrt(var + LN_EPS) * g2_ref[...] + b2_ref[...]


def _final(h, y4q, gates, ln2_g, ln2_b, alpha):
    n, d = h.shape
    nq = d // LANES
    tm = _pick(n, (256, 128, 64, 32, 16, 8))
    nb = n // tm
    row = pl.BlockSpec((1, d), lambda i: (0, 0))
    yspec = lambda k: pl.BlockSpec((tm * nq, LANES), lambda i, k=k: (k * nb + i, 0))
    return pl.pallas_call(
        functools.partial(_final_kernel, tm=tm, nq=nq, alpha=alpha),
        grid=(nb,),
        in_specs=[pl.BlockSpec((tm, d), lambda i: (i, 0)), yspec(0), yspec(1), yspec(2), yspec(3),
                  pl.BlockSpec((tm, TOP_K), lambda i: (i, 0)), row, row],
        out_specs=pl.BlockSpec((tm, d), lambda i: (i, 0)),
        out_shape=jax.ShapeDtypeStruct((n, d), f32),
        compiler_params=_cparams(("arbitrary",)),
        name="final",
    )(h, y4q, y4q, y4q, y4q, gates, ln2_g.reshape(1, d), ln2_b.reshape(1, d))


def kernel(x_prompt, x_sample, cache_k, cache_v, cache_idx_k, state_ret, page_table, w_in, w_o, ret_gn_w,
           ln1_g, ln1_b, w_router, b_router, w_gate_up, b_gate_up, w_down, b_down, ln2_g, ln2_b):
    depth = w_in.shape[0]
    assert depth == 1, "single-layer step"
    bp, t, d = x_prompt.shape
    bd, tn, _ = x_sample.shape
    assert bp == 1
    past = page_table.shape[1] * PAGE_SIZE
    np_, ns = bp * t, bd * tn
    n = np_ + ns
    alpha = (2 * depth) ** 0.25
    wkv = N_KV_A * HEAD_DIM_A
    layer = lambda a: a.reshape(a.shape[1:])

    x_all = jnp.concatenate([x_prompt.reshape(np_, d), x_sample.reshape(ns, d)], axis=0)
    pos_all = jnp.concatenate([jnp.arange(t), jnp.tile(past + jnp.arange(tn), bd)])
    u_f, u_b = _project(x_all, pos_all, layer(w_in))

    ik_b = u_b[:np_, C_TAIL:C_TAIL + IDX_DIM]
    attn_p = _dsa_prompt(u_f, u_b, jnp.concatenate([ik_b, ik_b], axis=1), t)
    pages = lambda a: a.reshape(a.shape[1], PAGE_SIZE * N_KV_A, HEAD_DIM_A)
    attn_s = _dsa_sample(u_f, np_, bd, tn, pages(cache_k), pages(cache_v), jnp.swapaxes(layer(cache_idx_k), 1, 2),
                         page_table)

    zero_state = jnp.zeros((bp, N_HEADS_B, QK_DIM_B, V_DIM_B), f32)
    rb_p, s_p = _retention(u_f, 0, bp, t, zero_state, layer(ret_gn_w), bf16)
    rb_s, s_s = _retention(u_f, np_, bd, tn, layer(state_ret), layer(ret_gn_w), f32)

    attn_all = jnp.concatenate([attn_p, attn_s.astype(bf16)], axis=0)
    rb_all = jnp.concatenate([rb_p, rb_s.astype(bf16)], axis=0)
    h, hq, eid, gates = _tail1(attn_all, rb_all, x_all, layer(w_o), layer(ln1_g), layer(ln1_b), layer(w_router),
                               layer(b_router), alpha)
    y4q = _moe(hq, eid, layer(w_gate_up), layer(b_gate_up), layer(w_down), layer(b_down), n)
    y = _final(h, y4q, gates, layer(ln2_g), layer(ln2_b), alpha)

    kv = lambda rows, c0, lead: u_f[rows, c0:c0 + wkv].reshape(lead + (N_KV_A, HEAD_DIM_A))[None]
    ps, ss = slice(0, np_), slice(np_, n)
    return (
        y[:np_].reshape(bp, t, d), y[np_:].reshape(bd, tn, d),
        kv(ps, C_KA, (bp, t)), kv(ps, C_VA, (bp, t)), u_f[ps, C_TAIL:C_TAIL + IDX_DIM].reshape(1, bp, t, IDX_DIM),
        s_p[None],
        kv(ss, C_KA, (bd, tn)), kv(ss, C_VA, (bd, tn)), u_f[ss, C_TAIL:C_TAIL + IDX_DIM].reshape(1, bd, tn, IDX_DIM),
        s_s[None],
    )
```

```python
import functools

import numpy as np
import jax
import jax.numpy as jnp
from jax import lax
from jax.experimental import pallas as pl
from jax.experimental.pallas import tpu as pltpu

f32 = jnp.float32
bf16 = jnp.bfloat16
i32 = jnp.int32

PAGE_SIZE = 128
HEAD_DIM_A = 128
N_HEADS_A = 8
N_KV_A = 4
ROPE_DIM_A = 32
ROPE_THETA = 500000.0
IDX_HEADS = 16
IDX_DIM = 64
IDX_ROPE_DIM = 16
TOPK_MAX = 256
V_DIM_B = 128
N_HEADS_B = 8
QK_DIM_B = 64
RET_CHUNK = 128
RET_THETA = 10000.0
TOP_K = 4
SWIGLU_LIMIT = 7.0
SWIGLU_ALPHA = 1.702
LN_EPS = 1e-5

LANES = 128
VMEM_LIMIT = 56 * 1024 * 1024

PROJ_TN = 512
C_QA, C_KA, C_VA, C_IQ, C_QB, C_KB, C_VB, C_GB, C_TAIL, PROJ_W = 0, 1024, 1536, 2048, 3072, 3584, 4096, 5120, 6144, 6656
PROJ_TILE_TYPES = (6, 6, 1, 0, 2, 2, 3, 4, 0, 0, 0, 0, 5)
QA_SCALE = HEAD_DIM_A ** -0.5 * 1.4426950408889634

INT_MIN = -2 ** 31
KEY_NEG_INF = -2139095041
NEG_BIG = -1e30

NT_DIMS = (((1,), (1,)), ((), ()))
TN_DIMS = (((0,), (0,)), ((), ()))


def _pick(n, cands):
    for c in cands:
        if n % c == 0:
            return c
    raise ValueError(f"no tile for {n}")


def _cparams(sem, vmem=VMEM_LIMIT):
    return pltpu.CompilerParams(dimension_semantics=sem, vmem_limit_bytes=vmem)


def _rope_table(pos, rot_dim, theta, period, scale=1.0, active=LANES):
    half = rot_dim // 2
    inv_freq = 1.0 / (theta ** (jnp.arange(half, dtype=f32) / half))
    ang = pos.astype(f32)[:, None] * inv_freq[None, :]
    cos, sin = jnp.cos(ang), jnp.sin(ang)
    lane = np.arange(LANES)
    d = lane % period
    first = (d < half) & (lane < active)
    second = (d >= half) & (d < rot_dim) & (lane < active)
    idx = np.where(first, d, np.where(second, d - half, 0))
    cos_l, sin_l = cos[:, idx], sin[:, idx]
    c = jnp.where(first | second, cos_l, 1.0)
    s1 = jnp.where(second, sin_l, 0.0)
    s2 = jnp.where(first, -sin_l, 0.0)
    return jnp.concatenate([c, s1, s2], axis=1) * scale


def _proj_kernel(tt_ref, x_ref, w_ref, tab_ref, of_ref, ob_ref, xb_scr):
    j = pl.program_id(1)

    @pl.when(j == 0)
    def _():
        xb_scr[...] = x_ref[...].astype(bf16)

    u = jnp.dot(xb_scr[...], w_ref[...], preferred_element_type=f32)
    t = tt_ref[j]

    def store(v):
        of_ref[...] = v
        ob_ref[...] = v.astype(bf16)

    @pl.when(t == 0)
    def _():
        store(u)

    def rope(half):
        c = tab_ref[0, :, 0:LANES]
        s1 = tab_ref[0, :, LANES:2 * LANES]
        s2 = tab_ref[0, :, 2 * LANES:3 * LANES]
        outs = []
        for q in range(PROJ_TN // LANES):
            uc = u[:, q * LANES:(q + 1) * LANES]
            outs.append(uc * c + pltpu.roll(uc, half, 1) * s1 + pltpu.roll(uc, LANES - half, 1) * s2)
        store(jnp.concatenate(outs, axis=1))

    @pl.when((t == 1) | (t == 6))
    def _():
        rope(ROPE_DIM_A // 2)

    @pl.when((t == 2) | (t == 5))
    def _():
        rope(IDX_ROPE_DIM // 2)

    @pl.when((t == 3) | (t == 4))
    def _():
        rope(QK_DIM_B // 2)


def _project(x_all, pos_all, w_in):
    n, d = x_all.shape
    tm = _pick(n, (768, 512, 384, 256, 128, 64, 32, 16, 8))
    o = np.cumsum((0, 1024, 512, 512, 1024, 64, 16, 512, 512, 1024, 1024))
    wp = jnp.concatenate([w_in[:, o[0]:o[4]], w_in[:, o[6]:o[10]], w_in[:, o[4]:o[6]],
                          jnp.zeros((d, PROJ_W - C_TAIL - 80), w_in.dtype)], axis=1).astype(bf16)
    tabs = jnp.stack([
        _rope_table(pos_all, ROPE_DIM_A, ROPE_THETA, HEAD_DIM_A),
        _rope_table(pos_all, ROPE_DIM_A, ROPE_THETA, HEAD_DIM_A),
        _rope_table(pos_all, IDX_ROPE_DIM, ROPE_THETA, IDX_DIM),
        _rope_table(pos_all, QK_DIM_B, RET_THETA, QK_DIM_B),
        _rope_table(pos_all, QK_DIM_B, RET_THETA, QK_DIM_B, scale=QK_DIM_B ** -0.5),
        _rope_table(pos_all, IDX_ROPE_DIM, ROPE_THETA, IDX_DIM, active=IDX_DIM),
        _rope_table(pos_all, ROPE_DIM_A, ROPE_THETA, HEAD_DIM_A, scale=QA_SCALE),
    ])
    tt = jnp.asarray(PROJ_TILE_TYPES, i32)
    nj = PROJ_W // PROJ_TN
    return pl.pallas_call(
        _proj_kernel,
        grid_spec=pltpu.PrefetchScalarGridSpec(
            num_scalar_prefetch=1,
            grid=(n // tm, nj),
            in_specs=[
                pl.BlockSpec((tm, d), lambda i, j, tt: (i, 0)),
                pl.BlockSpec((d, PROJ_TN), lambda i, j, tt: (0, j)),
                pl.BlockSpec((1, tm, 3 * LANES), lambda i, j, tt: (tt[j], i, 0)),
            ],
            out_specs=[
                pl.BlockSpec((tm, PROJ_TN), lambda i, j, tt: (i, j)),
                pl.BlockSpec((tm, PROJ_TN), lambda i, j, tt: (i, j)),
            ],
            scratch_shapes=[pltpu.VMEM((tm, d), bf16)],
        ),
        out_shape=[jax.ShapeDtypeStruct((n, PROJ_W), f32), jax.ShapeDtypeStruct((n, PROJ_W), bf16)],
        compiler_params=_cparams(("arbitrary", "arbitrary")),
        name="proj",
    )(tt, x_all, wp, tabs)


def _key_to_float(key):
    return pltpu.bitcast(key ^ ((key >> 31) & 0x7FFFFFFF), f32)


def _kth_threshold(count_ge, shape, k):
    def body(step, ans):
        cand = ans + jnp.left_shift(jnp.int32(1), 31 - step)
        return jnp.where(count_ge(_key_to_float(cand)) >= k, cand, ans)

    ans = lax.fori_loop(0, 32, body, jnp.full(shape, INT_MIN, i32))
    return _key_to_float(jnp.maximum(ans, KEY_NEG_INF + 1))


def _lane_blocks(x):
    return [x[:, i * LANES:(i + 1) * LANES] for i in range(x.shape[1] // LANES)]


def _flash_update(qs, ks, vs, bias, m_scr, l_scr, acc_scr, batched):
    n = len(qs)
    score = lambda g: lax.dot_general(qs[g], ks[g], NT_DIMS, preferred_element_type=f32) + bias

    def update(g, sm):
        m_old = m_scr[g]
        m_new = jnp.maximum(m_old, jnp.max(functools.reduce(jnp.maximum, _lane_blocks(sm)), axis=1, keepdims=True))
        alpha = jnp.exp2(m_old - m_new)
        p = jnp.exp2(sm - m_new)
        l_scr[g] = alpha * l_scr[g] + jnp.sum(functools.reduce(jnp.add, _lane_blocks(p)), axis=1, keepdims=True)
        m_scr[g] = m_new
        acc_scr[g] = alpha * acc_scr[g] + jnp.dot(p.astype(bf16), vs[g], preferred_element_type=f32)

    if batched:
        sms = [score(g) for g in range(n)]
        for g in range(n):
            update(g, sms[g])
    else:
        for g in range(n):
            update(g, score(g))


def _dsa_prompt_kernel(iq_ref, iw_ref, qa_ref, ikd_ref, ka_ref, vat_ref, o_ref,
                       qst, qgt, sct, m_scr, l_scr, acc_scr, *, tq, tk, topk):
    i = pl.program_id(0)
    n_chunks = (i * tq + tq + tk - 1) // tk
    row = lax.broadcasted_iota(i32, (LANES, tq), 0)

    for p in range(IDX_HEADS // 2):
        blk = iq_ref[:, p * LANES:(p + 1) * LANES].T
        qst[:, (2 * p) * tq:(2 * p + 1) * tq] = jnp.where(row < IDX_DIM, blk, 0.0).astype(bf16)
        qst[:, (2 * p + 1) * tq:(2 * p + 2) * tq] = jnp.where(row >= IDX_DIM, blk, 0.0).astype(bf16)
    wt = iw_ref[...].T * (IDX_DIM ** -0.5 * IDX_HEADS ** -0.5)
    for h in range(N_HEADS_A):
        qgt[h // 2, :, (h % 2) * tq:(h % 2 + 1) * tq] = qa_ref[:, h * LANES:(h + 1) * LANES].T.astype(bf16)

    qpos = i * tq + lax.broadcasted_iota(i32, (tk, tq), 1)

    def score_body(c, carry):
        k0 = pl.multiple_of(c * tk, tk)
        logits = jnp.dot(ikd_ref[pl.ds(k0, tk), :], qst[...], preferred_element_type=f32)
        acc = jnp.zeros((tk, tq), f32)
        for h in range(IDX_HEADS):
            acc = acc + jnp.maximum(logits[:, h * tq:(h + 1) * tq], 0.0) * wt[IDX_DIM + h:IDX_DIM + h + 1, :]
        kpos = k0 + lax.broadcasted_iota(i32, (tk, tq), 0)
        sct[c] = jnp.where(kpos <= qpos, acc, -jnp.inf)
        return carry

    lax.fori_loop(0, n_chunks, score_body, 0)

    def count_ge(cand):
        cb = jnp.broadcast_to(cand, (8, tq))

        def body(c, accs):
            accs = list(accs)
            for r in range(tk // 8):
                accs[r % 4] = accs[r % 4] + jnp.where(sct[c, r * 8:(r + 1) * 8, :] >= cb, 1.0, 0.0)
            return tuple(accs)

        accs = lax.fori_loop(0, n_chunks, body, (jnp.zeros((8, tq), f32),) * 4)
        return jnp.sum((accs[0] + accs[1]) + (accs[2] + accs[3]), axis=0, keepdims=True)

    thr = _kth_threshold(count_ge, (1, tq), float(topk))

    m_scr[...] = jnp.full(m_scr.shape, NEG_BIG, f32)
    l_scr[...] = jnp.zeros(l_scr.shape, f32)
    acc_scr[...] = jnp.zeros(acc_scr.shape, f32)

    def att_body(c, carry):
        k0 = pl.multiple_of(c * tk, tk)
        bias1 = jnp.where(sct[c] >= thr, 0.0, NEG_BIG)
        bias = jnp.concatenate([bias1, bias1], axis=1)
        groups = range(N_KV_A)
        sms = [jnp.dot(ka_ref[pl.ds(k0, tk), g * LANES:(g + 1) * LANES], qgt[g], preferred_element_type=f32) + bias
               for g in groups]
        m_new = [jnp.maximum(m_scr[g], jnp.max(sms[g], axis=0, keepdims=True)) for g in groups]
        alpha = [jnp.exp2(m_scr[g] - m_new[g]) for g in groups]
        ps = [jnp.exp2(sms[g] - m_new[g]) for g in groups]
        for g in groups:
            l_scr[g] = alpha[g] * l_scr[g] + jnp.sum(ps[g], axis=0, keepdims=True)
            m_scr[g] = m_new[g]
        for g in groups:
            acc_scr[g] = alpha[g] * acc_scr[g] + jnp.dot(vat_ref[c, g * LANES:(g + 1) * LANES, :], ps[g].astype(bf16),
                                                         preferred_element_type=f32)
        return carry

    lax.fori_loop(0, n_chunks, att_body, 0)

    for g in range(N_KV_A):
        o = acc_scr[g] / l_scr[g]
        o_ref[:, (2 * g) * LANES:(2 * g + 1) * LANES] = o[:, 0:tq].T.astype(o_ref.dtype)
        o_ref[:, (2 * g + 1) * LANES:(2 * g + 2) * LANES] = o[:, tq:2 * tq].T.astype(o_ref.dtype)


def _dsa_prompt(u_f, u_b, t):
    tq = _pick(t, (128,))
    tk = _pick(t, (512, 256, 128))
    topk = min(TOPK_MAX, t // 4)
    wa = N_HEADS_A * HEAD_DIM_A
    wkv = N_KV_A * HEAD_DIM_A
    ik_b = u_b[:t, C_TAIL:C_TAIL + IDX_DIM]
    ikd = jnp.concatenate([ik_b, ik_b], axis=1)
    vat = u_b[:t, C_VA:C_VA + wkv].reshape(t // tk, tk, wkv).transpose(0, 2, 1)
    kern = functools.partial(_dsa_prompt_kernel, tq=tq, tk=tk, topk=topk)
    one = pl.Buffered(1)
    return pl.pallas_call(
        kern,
        grid=(t // tq,),
        in_specs=[
            pl.BlockSpec((tq, IDX_HEADS * IDX_DIM), lambda i: (i, C_IQ // (IDX_HEADS * IDX_DIM))),
            pl.BlockSpec((tq, LANES), lambda i: (i, C_TAIL // LANES)),
            pl.BlockSpec((tq, wa), lambda i: (i, C_QA // wa)),
            pl.BlockSpec((t, LANES), lambda i: (0, 0), pipeline_mode=one),
            pl.BlockSpec((t, wkv), lambda i: (0, C_KA // wkv), pipeline_mode=one),
            pl.BlockSpec((t // tk, wkv, tk), lambda i: (0, 0, 0), pipeline_mode=one),
        ],
        out_specs=pl.BlockSpec((tq, wa), lambda i: (i, 0)),
        out_shape=jax.ShapeDtypeStruct((t, wa), bf16),
        scratch_shapes=[
            pltpu.VMEM((LANES, IDX_HEADS * tq), bf16),
            pltpu.VMEM((N_KV_A, LANES, 2 * tq), bf16),
            pltpu.VMEM((t // tk, tk, tq), f32),
            pltpu.VMEM((N_KV_A, 1, 2 * tq), f32),
            pltpu.VMEM((N_KV_A, 1, 2 * tq), f32),
            pltpu.VMEM((N_KV_A, LANES, 2 * tq), f32),
        ],
        compiler_params=_cparams(("arbitrary",)),
        name="dsa_prompt",
    )(u_f, u_f, u_f, ikd, u_b, vat)


def _dsa_sample_score_kernel(pt_ref, *refs, tn, ppg, topk, n_groups):
    pages = refs[:ppg]
    iq_ref, tail_ref, sc_ref, thr_ref, qs, wst = refs[ppg:]
    c = pl.program_id(1)
    wscale = IDX_DIM ** -0.5 * IDX_HEADS ** -0.5
    kw = ppg * PAGE_SIZE

    @pl.when(c == 0)
    def _():
        iq = iq_ref[...]
        w = tail_ref[...]
        for h in range(IDX_HEADS):
            qs[h * tn:(h + 1) * tn, :] = iq[:, h * IDX_DIM:(h + 1) * IDX_DIM]
            wst[h * tn:(h + 1) * tn, :] = jnp.broadcast_to(w[:, IDX_DIM + h:IDX_DIM + h + 1] * wscale, (tn, LANES))

    def head_sum(logits):
        width = logits.shape[1]
        acc = jnp.zeros((tn, width), f32)
        for h in range(IDX_HEADS):
            wh = wst[h * tn:(h + 1) * tn, :]
            acc = acc + jnp.maximum(logits[h * tn:(h + 1) * tn, :], 0.0) * jnp.concatenate([wh] * (width // LANES), axis=1)
        return acc

    keys_t = jnp.concatenate([p[...] for p in pages], axis=1).astype(bf16)
    sc_ref[0, c] = head_sum(jnp.dot(qs[...].astype(bf16), keys_t, preferred_element_type=f32))

    @pl.when(c == n_groups - 1)
    def _():
        ik_new = tail_ref[:, 0:IDX_DIM].astype(bf16)
        kpad = jnp.concatenate([ik_new, jnp.zeros((LANES - tn, IDX_DIM), bf16)], axis=0)
        s_new = head_sum(lax.dot_general(qs[...].astype(bf16), kpad, NT_DIMS, preferred_element_type=f32))
        qi = lax.broadcasted_iota(i32, (tn, LANES), 0)
        kj = lax.broadcasted_iota(i32, (tn, LANES), 1)
        snew = jnp.where(kj <= qi, s_new, -jnp.inf)
        sc_ref[0, n_groups] = jnp.concatenate([snew, jnp.full((tn, kw - LANES), -jnp.inf, f32)], axis=1)

        def count_ge(cand):
            cb = jnp.broadcast_to(cand, (tn, LANES))

            def body(cc, acc):
                blk = sc_ref[0, cc]
                for q in range(kw // LANES):
                    acc = acc + jnp.where(blk[:, q * LANES:(q + 1) * LANES] >= cb, 1.0, 0.0)
                return acc

            acc = lax.fori_loop(0, n_groups + 1, body, jnp.zeros((tn, LANES), f32))
            return jnp.sum(acc, axis=1, keepdims=True)

        thr = _kth_threshold(count_ge, (tn, 1), float(topk))
        thr_ref[0] = jnp.broadcast_to(thr, (tn, LANES))


def _dsa_sample_attn_kernel(pt_ref, *refs, tn, ppg, n_groups):
    kpages = refs[:ppg]
    vpages = refs[ppg:2 * ppg]
    qa_ref, kn_ref, vn_ref, sc_ref, scn_ref, thr_ref, o_ref, qg, m_scr, l_scr, acc_scr = refs[2 * ppg:]
    c = pl.program_id(1)

    @pl.when(c == 0)
    def _():
        qa = qa_ref[...]
        for g in range(N_KV_A):
            qg[g, 0:tn, :] = qa[:, (2 * g) * LANES:(2 * g + 1) * LANES]
            qg[g, tn:2 * tn, :] = qa[:, (2 * g + 1) * LANES:(2 * g + 2) * LANES]
        m_scr[...] = jnp.full(m_scr.shape, NEG_BIG, f32)
        l_scr[...] = jnp.zeros(l_scr.shape, f32)
        acc_scr[...] = jnp.zeros(acc_scr.shape, f32)

    thr = thr_ref[0][:, 0:1]
    thr2 = jnp.concatenate([thr, thr], axis=0)

    def attend(ks, vs, sc):
        bias = jnp.where(jnp.concatenate([sc, sc], axis=0) >= thr2, 0.0, NEG_BIG)
        _flash_update([qg[g].astype(bf16) for g in range(N_KV_A)], ks, vs, bias, m_scr, l_scr, acc_scr, batched=True)

    group = lambda pages, g: jnp.concatenate(
        [p[pl.ds(g, PAGE_SIZE, stride=N_KV_A), :] for p in pages], axis=0).astype(bf16)
    attend([group(kpages, g) for g in range(N_KV_A)], [group(vpages, g) for g in range(N_KV_A)], sc_ref[0, 0])

    @pl.when(c == n_groups - 1)
    def _():
        zpad = jnp.zeros((LANES - tn, N_KV_A * HEAD_DIM_A), bf16)
        attend(_lane_blocks(jnp.concatenate([kn_ref[...].astype(bf16), zpad], axis=0)),
               _lane_blocks(jnp.concatenate([vn_ref[...].astype(bf16), zpad], axis=0)),
               scn_ref[0, 0][:, 0:LANES])
        for g in range(N_KV_A):
            o = acc_scr[g] / l_scr[g]
            o_ref[:, (2 * g) * LANES:(2 * g + 1) * LANES] = o[0:tn]
            o_ref[:, (2 * g + 1) * LANES:(2 * g + 2) * LANES] = o[tn:2 * tn]


def _dsa_sample(u_f, row0, bd, tn, cache_k, cache_v, idx_kt, page_table):
    n_pages = page_table.shape[1]
    past = n_pages * PAGE_SIZE
    topk = min(TOPK_MAX, (past + tn) // 4)
    ppa = _pick(n_pages, (16, 8, 4, 2, 1))
    pps = _pick(n_pages, (32, 16, 8, 4, 2, 1))
    nga, ngs = n_pages // ppa, n_pages // pps
    kwa, kws = ppa * PAGE_SIZE, pps * PAGE_SIZE
    per = kws // kwa
    wkv = N_KV_A * HEAD_DIM_A
    wa = N_HEADS_A * HEAD_DIM_A
    rb0 = row0 // tn

    def idx_spec(j):
        return pl.BlockSpec((None, IDX_DIM, PAGE_SIZE), lambda b, c, pt, j=j: (pt[b, c * pps + j], 0, 0))

    def kv_spec(j):
        return pl.BlockSpec((None, PAGE_SIZE * N_KV_A, HEAD_DIM_A), lambda b, c, pt, j=j: (pt[b, c * ppa + j], 0, 0))

    score = pl.pallas_call(
        functools.partial(_dsa_sample_score_kernel, tn=tn, ppg=pps, topk=topk, n_groups=ngs),
        grid_spec=pltpu.PrefetchScalarGridSpec(
            num_scalar_prefetch=1,
            grid=(bd, ngs),
            in_specs=[idx_spec(j) for j in range(pps)] + [
                pl.BlockSpec((tn, IDX_HEADS * IDX_DIM), lambda b, c, pt: (rb0 + b, C_IQ // (IDX_HEADS * IDX_DIM))),
                pl.BlockSpec((tn, LANES), lambda b, c, pt: (rb0 + b, C_TAIL // LANES)),
            ],
            out_specs=[
                pl.BlockSpec((1, ngs + 1, tn, kws), lambda b, c, pt: (b, 0, 0, 0)),
                pl.BlockSpec((1, tn, LANES), lambda b, c, pt: (b, 0, 0)),
            ],
            scratch_shapes=[pltpu.VMEM((IDX_HEADS * tn, IDX_DIM), f32), pltpu.VMEM((IDX_HEADS * tn, LANES), f32)],
        ),
        out_shape=[jax.ShapeDtypeStruct((bd, ngs + 1, tn, kws), f32), jax.ShapeDtypeStruct((bd, tn, LANES), f32)],
        compiler_params=_cparams(("arbitrary", "arbitrary")),
        name="dsa_sample_score",
    )
    scores, thr = score(page_table, *([idx_kt] * pps), u_f, u_f)

    attn = pl.pallas_call(
        functools.partial(_dsa_sample_attn_kernel, tn=tn, ppg=ppa, n_groups=nga),
        grid_spec=pltpu.PrefetchScalarGridSpec(
            num_scalar_prefetch=1,
            grid=(bd, nga),
            in_specs=[kv_spec(j) for j in range(ppa)] + [kv_spec(j) for j in range(ppa)] + [
                pl.BlockSpec((tn, wa), lambda b, c, pt: (rb0 + b, C_QA // wa)),
                pl.BlockSpec((tn, wkv), lambda b, c, pt: (rb0 + b, C_KA // wkv)),
                pl.BlockSpec((tn, wkv), lambda b, c, pt: (rb0 + b, C_VA // wkv)),
                pl.BlockSpec((1, 1, tn, kwa), lambda b, c, pt: (b, c // per, 0, c % per)),
                pl.BlockSpec((1, 1, tn, kwa), lambda b, c, pt: (b, ngs, 0, 0)),
                pl.BlockSpec((1, tn, LANES), lambda b, c, pt: (b, 0, 0)),
            ],
            out_specs=pl.BlockSpec((tn, wa), lambda b, c, pt: (b, 0)),
            scratch_shapes=[
                pltpu.VMEM((N_KV_A, 2 * tn, LANES), f32),
                pltpu.VMEM((N_KV_A, 2 * tn, 1), f32),
                pltpu.VMEM((N_KV_A, 2 * tn, 1), f32),
                pltpu.VMEM((N_KV_A, 2 * tn, LANES), f32),
            ],
        ),
        out_shape=jax.ShapeDtypeStruct((bd * tn, wa), f32),
        compiler_params=_cparams(("arbitrary", "arbitrary")),
        name="dsa_sample_attn",
    )
    return attn(page_table, *([cache_k] * ppa), *([cache_v] * ppa), u_f, u_f, u_f, scores, scores, thr)


def _ret_kernel(q_ref, k_ref, v_ref, g_ref, gnw_ref, dmask_ref, qdec_ref, kdec_ref, sdec_ref, s0_ref,
                o_ref, sout_ref, s_scr, *, n_chunks):
    c = pl.program_id(1)

    @pl.when(c == 0)
    def _():
        s_scr[...] = s0_ref[0]

    for h in range(N_HEADS_B):
        q = q_ref[:, h * QK_DIM_B:(h + 1) * QK_DIM_B]
        k = k_ref[:, h * QK_DIM_B:(h + 1) * QK_DIM_B]
        v = v_ref[:, h * V_DIM_B:(h + 1) * V_DIM_B].astype(bf16)
        gate = g_ref[:, h * V_DIM_B:(h + 1) * V_DIM_B]
        qb = q.astype(bf16)
        att = lax.dot_general(qb, k.astype(bf16), NT_DIMS, preferred_element_type=f32) * dmask_ref[h]
        s_old = s_scr[h]
        o = (jnp.dot(att.astype(bf16), v, preferred_element_type=f32)
             + jnp.dot(qb, s_old.astype(bf16), preferred_element_type=f32) * qdec_ref[h])
        kd = (k * kdec_ref[h]).astype(bf16)
        s_scr[h] = s_old * sdec_ref[h] + lax.dot_general(kd, v, TN_DIMS, preferred_element_type=f32)
        mu = jnp.mean(o, axis=-1, keepdims=True)
        var = jnp.mean(jnp.square(o - mu), axis=-1, keepdims=True)
        rb = (o - mu) * lax.rsqrt(var + LN_EPS) * gnw_ref[:, h * V_DIM_B:(h + 1) * V_DIM_B]
        rb = rb * (gate / (1.0 + jnp.exp(-gate)))
        o_ref[:, h * V_DIM_B:(h + 1) * V_DIM_B] = rb.astype(o_ref.dtype)

    @pl.when(c == n_chunks - 1)
    def _():
        sout_ref[0] = s_scr[...]


def _retention(u_f, row0, nb, t, state0, gn_w, out_dtype):
    ch = min(RET_CHUNK, t)
    if t % ch:
        ch = t
    n = t // ch
    hb = N_HEADS_B
    lg = jnp.log1p(-jnp.exp2(-5.0 - jnp.arange(hb, dtype=f32)))
    i = jnp.arange(ch)
    diff = i[:, None] - i[None, :]
    dmask = jnp.where(diff >= 0, jnp.exp(lg[:, None, None] * jnp.maximum(diff, 0)), 0.0)
    qdec = jnp.broadcast_to(jnp.exp(lg[:, None] * (i + 1))[:, :, None], (hb, ch, V_DIM_B))
    kdec = jnp.broadcast_to(jnp.exp(lg[:, None] * (ch - 1 - i))[:, :, None], (hb, ch, QK_DIM_B))
    sdec = jnp.broadcast_to(jnp.exp(lg * ch)[:, None, None], (hb, 1, V_DIM_B))
    wqk = hb * QK_DIM_B
    wv = hb * V_DIM_B
    rb0 = row0 // ch
    full3 = lambda shp: pl.BlockSpec(shp, lambda b, c: (0, 0, 0))
    return pl.pallas_call(
        functools.partial(_ret_kernel, n_chunks=n),
        grid=(nb, n),
        in_specs=[
            pl.BlockSpec((ch, wqk), lambda b, c: (rb0 + b * n + c, C_QB // wqk)),
            pl.BlockSpec((ch, wqk), lambda b, c: (rb0 + b * n + c, C_KB // wqk)),
            pl.BlockSpec((ch, wv), lambda b, c: (rb0 + b * n + c, C_VB // wv)),
            pl.BlockSpec((ch, wv), lambda b, c: (rb0 + b * n + c, C_GB // wv)),
            pl.BlockSpec((1, wv), lambda b, c: (0, 0)),
            full3((hb, ch, ch)), full3((hb, ch, V_DIM_B)), full3((hb, ch, QK_DIM_B)), full3((hb, 1, V_DIM_B)),
            pl.BlockSpec((1, hb, QK_DIM_B, V_DIM_B), lambda b, c: (b, 0, 0, 0)),
        ],
        out_specs=[
            pl.BlockSpec((ch, wv), lambda b, c: (b * n + c, 0)),
            pl.BlockSpec((1, hb, QK_DIM_B, V_DIM_B), lambda b, c: (b, 0, 0, 0)),
        ],
        out_shape=[jax.ShapeDtypeStruct((nb * t, wv), out_dtype),
                   jax.ShapeDtypeStruct((nb, hb, QK_DIM_B, V_DIM_B), f32)],
        scratch_shapes=[pltpu.VMEM((hb, QK_DIM_B, V_DIM_B), f32)],
        compiler_params=_cparams(("arbitrary", "arbitrary")),
        name="retention",
    )(u_f, u_f, u_f, u_f, gn_w.reshape(1, wv), dmask, qdec, kdec, sdec, state0)


def _split_hi_lo(a):
    hi = a.astype(bf16)
    return hi, (a - hi.astype(f32)).astype(bf16)


def _tail1_kernel(attn_ref, rb_ref, x_ref, wo_ref, g1_ref, b1_ref, wrh_ref, wrl_ref, br_ref,
                  h_ref, hq_ref, eid_ref, gate_ref, *, tm, n_exp, alpha, wa):
    mix = (jnp.dot(attn_ref[...], wo_ref[0:wa, :], preferred_element_type=f32)
           + jnp.dot(rb_ref[...], wo_ref[wa:, :], preferred_element_type=f32))
    z = alpha * x_ref[...] + mix
    mu = jnp.mean(z, axis=-1, keepdims=True)
    var = jnp.mean(jnp.square(z - mu), axis=-1, keepdims=True)
    h = (z - mu) * lax.rsqrt(var + LN_EPS) * g1_ref[...] + b1_ref[...]
    h_ref[...] = h
    nq = h.shape[1] // LANES
    for j in range(nq):
        hq_ref[pl.ds(j, tm, stride=nq), :] = h[:, j * LANES:(j + 1) * LANES]

    hh, hl = _split_hi_lo(h)
    logits = (jnp.dot(hh, wrh_ref[...], preferred_element_type=f32)
              + jnp.dot(hl, wrh_ref[...], preferred_element_type=f32)
              + jnp.dot(hh, wrl_ref[...], preferred_element_type=f32)) + br_ref[...]
    lane = lax.broadcasted_iota(i32, (tm, LANES), 1)
    lanef = lane.astype(f32)
    logits = jnp.where(lane < n_exp, logits, -jnp.inf)
    vals, ids = [], []
    for _ in range(TOP_K):
        m = jnp.max(logits, axis=1, keepdims=True)
        idx = jnp.min(jnp.where(logits == m, lanef, float(LANES)), axis=1, keepdims=True)
        vals.append(m)
        ids.append(idx)
        logits = jnp.where(lanef == idx, -jnp.inf, logits)
    es = [jnp.exp(v - vals[0]) for v in vals]
    den = es[0] + es[1] + es[2] + es[3]
    eid_ref[...] = jnp.concatenate(ids, axis=1).astype(i32)
    gate_ref[...] = jnp.concatenate([e / den for e in es], axis=1)


def _tail1(attn_b, rb_b, x_all, w_o, ln1_g, ln1_b, w_router, b_router, alpha):
    n, d = x_all.shape
    wa = attn_b.shape[1]
    n_exp = w_router.shape[1]
    tm = _pick(n, (256, 128, 64, 32, 16, 8))
    nq = d // LANES
    wr = jnp.zeros((d, LANES), f32).at[:, :n_exp].set(w_router)
    wrh, wrl = _split_hi_lo(wr)
    br = jnp.zeros((1, LANES), f32).at[0, :n_exp].set(b_router)
    row = lambda w: pl.BlockSpec((1, w), lambda i: (0, 0))
    return pl.pallas_call(
        functools.partial(_tail1_kernel, tm=tm, n_exp=n_exp, alpha=alpha, wa=wa),
        grid=(n // tm,),
        in_specs=[
            pl.BlockSpec((tm, wa), lambda i: (i, 0)),
            pl.BlockSpec((tm, rb_b.shape[1]), lambda i: (i, 0)),
            pl.BlockSpec((tm, d), lambda i: (i, 0)),
            pl.BlockSpec(w_o.shape, lambda i: (0, 0), pipeline_mode=pl.Buffered(1)),
            row(d), row(d),
            pl.BlockSpec((d, LANES), lambda i: (0, 0)), pl.BlockSpec((d, LANES), lambda i: (0, 0)), row(LANES),
        ],
        out_specs=[
            pl.BlockSpec((tm, d), lambda i: (i, 0)),
            pl.BlockSpec((tm * nq, LANES), lambda i: (i, 0)),
            pl.BlockSpec((tm, TOP_K), lambda i: (i, 0)),
            pl.BlockSpec((tm, TOP_K), lambda i: (i, 0)),
        ],
        out_shape=[jax.ShapeDtypeStruct((n, d), f32), jax.ShapeDtypeStruct((n * nq, LANES), f32),
                   jax.ShapeDtypeStruct((n, TOP_K), i32), jax.ShapeDtypeStruct((n, TOP_K), f32)],
        compiler_params=_cparams(("arbitrary",)),
        name="tail1",
    )(attn_b, rb_b, x_all, w_o.astype(bf16), ln1_g.reshape(1, d), ln1_b.reshape(1, d), wrh, wrl, br)


MOE_R = 2048
MOE_SUB = 256
MOE_TF = 256
MOE_ISSUE_UNROLL = 8
MOE_TILES = (512, 256, 128)


def _moe_plan(eid, n_exp, r_cap):
    n = eid.shape[0]
    p = n * TOP_K
    flat = eid.reshape(p)
    onehot = (flat[:, None] == jnp.arange(n_exp, dtype=i32)[None, :]).astype(i32)
    csum = jnp.cumsum(onehot, axis=0)
    rank = jnp.sum((csum - onehot) * onehot, axis=1)
    counts = csum[-1]
    ngrp = (counts + r_cap - 1) // r_cap
    gend = jnp.cumsum(ngrp)
    gstart = gend - ngrp
    g_of = gstart[flat] + rank // r_cap
    slot = rank % r_cap
    n_groups = n_exp + p // r_cap
    t_idx = jnp.arange(p, dtype=i32) // TOP_K
    k_idx = jnp.arange(p, dtype=i32) % TOP_K
    tok = jnp.zeros((n_groups, r_cap), i32).at[g_of, slot].set(t_idx)
    spare = jnp.broadcast_to(p + jnp.arange(r_cap, dtype=i32) % MOE_SUB, (n_groups, r_cap))
    dst = spare.at[g_of, slot].set(k_idx * n + t_idx)
    gid = jnp.arange(n_groups, dtype=i32)
    total = gend[-1]
    gclamp = jnp.minimum(gid, total - 1)
    g_exp = jnp.sum((gend[None, :] <= gclamp[:, None]).astype(i32), axis=1)
    g_rows = jnp.clip(counts[g_exp] - (gclamp - gstart[g_exp]) * r_cap, 0, r_cap)
    g_rows = jnp.where(gid < total, g_rows, 0).astype(i32)
    return g_exp, g_rows, tok.reshape(n_groups, 1, r_cap), dst.reshape(n_groups, 1, r_cap)


def _moe_kernel(ge_ref, gr_ref, tok_ref, dst_ref, hq_ref, wgu_ref, wd_ref, bgu_ref, bd_ref, pm_ref, y_ref,
                qbuf, xb, acc, wgu_b, wd_b, sem_in, sem_out, *, nq, nj, spare_row0):
    g = pl.program_id(0)
    j = pl.program_id(1)
    rows = gr_ref[g]
    nsub = (rows + MOE_SUB - 1) // MOE_SUB
    active = rows > 0

    def in_copy(s, i, slot):
        t = tok_ref[0, 0, s * MOE_SUB + i]
        return pltpu.make_async_copy(hq_ref.at[pl.ds(pl.multiple_of(t * nq, nq), nq), :],
                                     qbuf.at[slot, pl.ds(pl.multiple_of(i * nq, nq), nq), :], sem_in.at[slot])

    def out_copy(s, i, slot):
        d = dst_ref[0, 0, s * MOE_SUB + i]
        return pltpu.make_async_copy(qbuf.at[slot, pl.ds(pl.multiple_of(i * nq, nq), nq), :],
                                     y_ref.at[pl.ds(pl.multiple_of(d * nq, nq), nq), :], sem_out.at[slot])

    def for_rows(fn):
        def body(b, c):
            for u in range(MOE_ISSUE_UNROLL):
                fn(b * MOE_ISSUE_UNROLL + u)
            return c

        lax.fori_loop(0, MOE_SUB // MOE_ISSUE_UNROLL, body, 0)

    def wait_in(slot):
        pltpu.make_async_copy(hq_ref.at[pl.ds(0, MOE_SUB * nq), :], qbuf.at[slot], sem_in.at[slot]).wait()

    def wait_out(slot):
        pltpu.make_async_copy(qbuf.at[slot], y_ref.at[pl.ds(0, MOE_SUB * nq), :], sem_out.at[slot]).wait()

    def convert(s, slot):
        r0 = pl.multiple_of(s * MOE_SUB, MOE_SUB)
        for jj in range(nq):
            xb[pl.ds(r0, MOE_SUB), jj * LANES:(jj + 1) * LANES] = (
                qbuf[slot, pl.ds(jj, MOE_SUB, stride=nq), :].astype(bf16))
        acc[pl.ds(r0, MOE_SUB), :] = jnp.zeros((MOE_SUB, acc.shape[1]), f32)

    def stage(s, slot):
        r0 = pl.multiple_of(s * MOE_SUB, MOE_SUB)
        a = acc[pl.ds(r0, MOE_SUB), :] + bd_ref[0]
        for jj in range(nq):
            qbuf[slot, pl.ds(jj, MOE_SUB, stride=nq), :] = a[:, jj * LANES:(jj + 1) * LANES]

    npairs = (nsub + 1) // 2

    @pl.when((g == 0) & (j == 0))
    def _():
        qbuf[1] = jnp.zeros(qbuf.shape[1:], f32)
        fill = pltpu.make_async_copy(qbuf.at[1], y_ref.at[pl.ds(spare_row0, MOE_SUB * nq), :], sem_out.at[1])
        fill.start()
        fill.wait()

    @pl.when(active & (j == 0))
    def _():
        for_rows(lambda i: in_copy(0, i, 0).start())

        def pair(pp, c):
            s0 = 2 * pp
            s1 = s0 + 1

            @pl.when(s1 < nsub)
            def _():
                for_rows(lambda i: in_copy(s1, i, 1).start())

            wait_in(0)
            convert(s0, 0)

            @pl.when(s1 < nsub)
            def _():
                @pl.when(s1 + 1 < nsub)
                def _():
                    for_rows(lambda i: in_copy(s1 + 1, i, 0).start())

                wait_in(1)
                convert(s1, 1)

            return c

        lax.fori_loop(0, npairs, pair, 0)

    @pl.when(active)
    def _():
        wgu_b[...] = wgu_ref[0].astype(bf16)
        wd_b[...] = wd_ref[0].astype(bf16)

        def tile(r0, size):
            gu = jnp.dot(xb[pl.ds(r0, size), :], wgu_b[...], preferred_element_type=f32) + bgu_ref[0]
            gub = gu.astype(bf16)
            gates, ups = [], []
            for q in range(2 * MOE_TF // 256):
                de = jnp.dot(gub[:, q * 256:(q + 1) * 256], pm_ref[...], preferred_element_type=f32)
                gates.append(de[:, 0:LANES])
                ups.append(de[:, LANES:2 * LANES])
            gate = jnp.minimum(jnp.concatenate(gates, axis=1), SWIGLU_LIMIT)
            up = jnp.clip(jnp.concatenate(ups, axis=1), -SWIGLU_LIMIT, SWIGLU_LIMIT)
            act = (up + 1.0) * gate * (1.0 / (1.0 + jnp.exp(-SWIGLU_ALPHA * gate)))
            acc[pl.ds(r0, size), :] += jnp.dot(act.astype(bf16), wd_b[...], preferred_element_type=f32)

        big, small = MOE_TILES[0], MOE_TILES[-1]
        padded = (rows + small - 1) // small * small
        nbig = padded // big

        def big_tile(s, c):
            tile(pl.multiple_of(s * big, big), big)
            return c

        lax.fori_loop(0, nbig, big_tile, 0)
        done = nbig * big
        for size in MOE_TILES[1:]:
            take = (padded - done) >= size

            @pl.when(take)
            def _(done=done, size=size):
                tile(pl.multiple_of(done, small), size)

            done = done + jnp.where(take, size, 0)

    @pl.when(active & (j == nj - 1))
    def _():
        def pair(pp, c):
            s0 = 2 * pp
            s1 = s0 + 1

            @pl.when(pp > 0)
            def _():
                wait_out(0)

            stage(s0, 0)
            for_rows(lambda i: out_copy(s0, i, 0).start())

            @pl.when(s1 < nsub)
            def _():
                @pl.when(pp > 0)
                def _():
                    wait_out(1)

                stage(s1, 1)
                for_rows(lambda i: out_copy(s1, i, 1).start())

            return c

        lax.fori_loop(0, npairs, pair, 0)
        wait_out(0)

        @pl.when(nsub >= 2)
        def _():
            wait_out(1)


def _deinterleave_matrix():
    pm = np.zeros((256, 256), np.float32)
    i = np.arange(LANES)
    pm[2 * i, i] = 1.0
    pm[2 * i + 1, LANES + i] = 1.0
    return jnp.asarray(pm, bf16)


def _moe(hq, eid, w_gate_up, b_gate_up, w_down, b_down, n):
    n_exp, d, f2 = w_gate_up.shape
    dff = f2 // 2
    nq = d // LANES
    nj = dff // MOE_TF
    g_exp, g_rows, tok, dst = _moe_plan(eid, n_exp, MOE_R)
    n_groups = g_exp.shape[0]

    def jeff(g, j, gr):
        return jnp.where(gr[g] > 0, j, nj - 1)

    return pl.pallas_call(
        functools.partial(_moe_kernel, nq=nq, nj=nj, spare_row0=TOP_K * n * nq),
        grid_spec=pltpu.PrefetchScalarGridSpec(
            num_scalar_prefetch=2,
            grid=(n_groups, nj),
            in_specs=[
                pl.BlockSpec((1, 1, MOE_R), lambda g, j, ge, gr: (g, 0, 0), memory_space=pltpu.SMEM),
                pl.BlockSpec((1, 1, MOE_R), lambda g, j, ge, gr: (g, 0, 0), memory_space=pltpu.SMEM),
                pl.BlockSpec(memory_space=pl.ANY),
                pl.BlockSpec((1, d, 2 * MOE_TF), lambda g, j, ge, gr: (ge[g], 0, jeff(g, j, gr))),
                pl.BlockSpec((1, MOE_TF, d), lambda g, j, ge, gr: (ge[g], jeff(g, j, gr), 0)),
                pl.BlockSpec((1, 1, 2 * MOE_TF), lambda g, j, ge, gr: (ge[g], 0, jeff(g, j, gr))),
                pl.BlockSpec((1, 1, d), lambda g, j, ge, gr: (ge[g], 0, 0)),
                pl.BlockSpec((256, 256), lambda g, j, ge, gr: (0, 0)),
            ],
            out_specs=pl.BlockSpec(memory_space=pl.ANY),
            scratch_shapes=[
                pltpu.VMEM((2, MOE_SUB * nq, LANES), f32),
                pltpu.VMEM((MOE_R, d), bf16),
                pltpu.VMEM((MOE_R, d), f32),
                pltpu.VMEM((d, 2 * MOE_TF), bf16),
                pltpu.VMEM((MOE_TF, d), bf16),
                pltpu.SemaphoreType.DMA((2,)),
                pltpu.SemaphoreType.DMA((2,)),
            ],
        ),
        out_shape=jax.ShapeDtypeStruct(((TOP_K * n + MOE_SUB) * nq, LANES), f32),
        compiler_params=_cparams(("arbitrary", "arbitrary")),
        name="moe",
    )(g_exp, g_rows, tok, dst, hq, w_gate_up, w_down, b_gate_up.reshape(n_exp, 1, f2),
      b_down.reshape(n_exp, 1, d), _deinterleave_matrix())


def _final_kernel(h_ref, y0_ref, y1_ref, y2_ref, y3_ref, gate_ref, g2_ref, b2_ref, o_ref, *, tm, nq, alpha):
    gates = gate_ref[...]
    f = jnp.zeros(h_ref.shape, f32)
    for k, y_ref in enumerate((y0_ref, y1_ref, y2_ref, y3_ref)):
        yk = jnp.concatenate([y_ref[pl.ds(jj, tm, stride=nq), :] for jj in range(nq)], axis=1)
        f = f + gates[:, k:k + 1] * yk
    z = alpha * h_ref[...] + f
    mu = jnp.mean(z, axis=-1, keepdims=True)
    var = jnp.mean(jnp.square(z - mu), axis=-1, keepdims=True)
    o_ref[...] = (z - mu) * lax.rsqrt(var + LN_EPS) * g2_ref[...] + b2_ref[...]


def _final(h, y4q, gates, ln2_g, ln2_b, alpha):
    n, d = h.shape
    nq = d // LANES
    tm = _pick(n, (256, 128, 64, 32, 16, 8))
    nb = n // tm
    row = pl.BlockSpec((1, d), lambda i: (0, 0))
    yspec = lambda k: pl.BlockSpec((tm * nq, LANES), lambda i, k=k: (k * nb + i, 0))
    return pl.pallas_call(
        functools.partial(_final_kernel, tm=tm, nq=nq, alpha=alpha),
        grid=(nb,),
        in_specs=[pl.BlockSpec((tm, d), lambda i: (i, 0)), yspec(0), yspec(1), yspec(2), yspec(3),
                  pl.BlockSpec((tm, TOP_K), lambda i: (i, 0)), row, row],
        out_specs=pl.BlockSpec((tm, d), lambda i: (i, 0)),
        out_shape=jax.ShapeDtypeStruct((n, d), f32),
        compiler_params=_cparams(("arbitrary",)),
        name="final",
    )(h, y4q, y4q, y4q, y4q, gates, ln2_g.reshape(1, d), ln2_b.reshape(1, d))


def kernel(x_prompt, x_sample, cache_k, cache_v, cache_idx_k, state_ret, page_table, w_in, w_o, ret_gn_w,
           ln1_g, ln1_b, w_router, b_router, w_gate_up, b_gate_up, w_down, b_down, ln2_g, ln2_b):
    depth = w_in.shape[0]
    assert depth == 1, "single-layer step"
    bp, t, d = x_prompt.shape
    bd, tn, _ = x_sample.shape
    assert bp == 1
    past = page_table.shape[1] * PAGE_SIZE
    np_, ns = bp * t, bd * tn
    n = np_ + ns
    alpha = (2 * depth) ** 0.25
    wkv = N_KV_A * HEAD_DIM_A
    layer = lambda a: a.reshape(a.shape[1:])

    x_all = jnp.concatenate([x_prompt.reshape(np_, d), x_sample.reshape(ns, d)], axis=0)
    pos_all = jnp.concatenate([jnp.arange(t), jnp.tile(past + jnp.arange(tn), bd)])
    u_f, u_b = _project(x_all, pos_all, layer(w_in))

    attn_p = _dsa_prompt(u_f, u_b, t)
    pages = lambda a: a.reshape(a.shape[1], PAGE_SIZE * N_KV_A, HEAD_DIM_A)
    attn_s = _dsa_sample(u_f, np_, bd, tn, pages(cache_k), pages(cache_v), jnp.swapaxes(layer(cache_idx_k), 1, 2),
                         page_table)

    zero_state = jnp.zeros((bp, N_HEADS_B, QK_DIM_B, V_DIM_B), f32)
    rb_p, s_p = _retention(u_f, 0, bp, t, zero_state, layer(ret_gn_w), bf16)
    rb_s, s_s = _retention(u_f, np_, bd, tn, layer(state_ret), layer(ret_gn_w), f32)

    attn_all = jnp.concatenate([attn_p, attn_s.astype(bf16)], axis=0)
    rb_all = jnp.concatenate([rb_p, rb_s.astype(bf16)], axis=0)
    h, hq, eid, gates = _tail1(attn_all, rb_all, x_all, layer(w_o), layer(ln1_g), layer(ln1_b), layer(w_router),
                               layer(b_router), alpha)
    y4q = _moe(hq, eid, layer(w_gate_up), layer(b_gate_up), layer(w_down), layer(b_down), n)
    y = _final(h, y4q, gates, layer(ln2_g), layer(ln2_b), alpha)

    kv = lambda rows, c0, lead: u_f[rows, c0:c0 + wkv].reshape(lead + (N_KV_A, HEAD_DIM_A))[None]
    ps, ss = slice(0, np_), slice(np_, n)
    return (
        y[:np_].reshape(bp, t, d), y[np_:].reshape(bd, tn, d),
        kv(ps, C_KA, (bp, t)), kv(ps, C_VA, (bp, t)), u_f[ps, C_TAIL:C_TAIL + IDX_DIM].reshape(1, bp, t, IDX_DIM),
        s_p[None],
        kv(ss, C_KA, (bd, tn)), kv(ss, C_VA, (bd, tn)), u_f[ss, C_TAIL:C_TAIL + IDX_DIM].reshape(1, bd, tn, IDX_DIM),
        s_s[None],
    )
```

```python
import functools

import numpy as np
import jax
import jax.numpy as jnp
from jax import lax
from jax.experimental import pallas as pl
from jax.experimental.pallas import tpu as pltpu

f32 = jnp.float32
bf16 = jnp.bfloat16
i32 = jnp.int32

PAGE_SIZE = 128
HEAD_DIM_A = 128
N_HEADS_A = 8
N_KV_A = 4
ROPE_DIM_A = 32
ROPE_THETA = 500000.0
IDX_HEADS = 16
IDX_DIM = 64
IDX_ROPE_DIM = 16
TOPK_MAX = 256
V_DIM_B = 128
N_HEADS_B = 8
QK_DIM_B = 64
RET_CHUNK = 128
RET_THETA = 10000.0
TOP_K = 4
SWIGLU_LIMIT = 7.0
SWIGLU_ALPHA = 1.702
LN_EPS = 1e-5

LANES = 128
VMEM_LIMIT = 56 * 1024 * 1024

PROJ_TN = 512
C_QA, C_KA, C_VA, C_IQ, C_QB, C_KB, C_VB, C_GB, C_TAIL, PROJ_W = 0, 1024, 1536, 2048, 3072, 3584, 4096, 5120, 6144, 6656
PROJ_TILE_TYPES = (6, 6, 1, 0, 2, 2, 3, 4, 0, 0, 0, 0, 5)
QA_SCALE = HEAD_DIM_A ** -0.5 * 1.4426950408889634

INT_MIN = -2 ** 31
KEY_NEG_INF = -2139095041
NEG_BIG = -1e30

NT_DIMS = (((1,), (1,)), ((), ()))
TN_DIMS = (((0,), (0,)), ((), ()))


def _pick(n, cands):
    for c in cands:
        if n % c == 0:
            return c
    raise ValueError(f"no tile for {n}")


def _cparams(sem, vmem=VMEM_LIMIT):
    return pltpu.CompilerParams(dimension_semantics=sem, vmem_limit_bytes=vmem)


def _rope_table(pos, rot_dim, theta, period, scale=1.0, active=LANES):
    half = rot_dim // 2
    inv_freq = 1.0 / (theta ** (jnp.arange(half, dtype=f32) / half))
    ang = pos.astype(f32)[:, None] * inv_freq[None, :]
    cos, sin = jnp.cos(ang), jnp.sin(ang)
    lane = np.arange(LANES)
    d = lane % period
    first = (d < half) & (lane < active)
    second = (d >= half) & (d < rot_dim) & (lane < active)
    idx = np.where(first, d, np.where(second, d - half, 0))
    cos_l, sin_l = cos[:, idx], sin[:, idx]
    c = jnp.where(first | second, cos_l, 1.0)
    s1 = jnp.where(second, sin_l, 0.0)
    s2 = jnp.where(first, -sin_l, 0.0)
    return jnp.concatenate([c, s1, s2], axis=1) * scale


def _proj_kernel(tt_ref, x_ref, w_ref, tab_ref, of_ref, ob_ref, xb_scr):
    j = pl.program_id(1)

    @pl.when(j == 0)
    def _():
        xb_scr[...] = x_ref[...].astype(bf16)

    u = jnp.dot(xb_scr[...], w_ref[...], preferred_element_type=f32)
    t = tt_ref[j]

    def store(v):
        of_ref[...] = v
        ob_ref[...] = v.astype(bf16)

    @pl.when(t == 0)
    def _():
        store(u)

    def rope(half):
        c = tab_ref[0, :, 0:LANES]
        s1 = tab_ref[0, :, LANES:2 * LANES]
        s2 = tab_ref[0, :, 2 * LANES:3 * LANES]
        outs = []
        for q in range(PROJ_TN // LANES):
            uc = u[:, q * LANES:(q + 1) * LANES]
            outs.append(uc * c + pltpu.roll(uc, half, 1) * s1 + pltpu.roll(uc, LANES - half, 1) * s2)
        store(jnp.concatenate(outs, axis=1))

    @pl.when((t == 1) | (t == 6))
    def _():
        rope(ROPE_DIM_A // 2)

    @pl.when((t == 2) | (t == 5))
    def _():
        rope(IDX_ROPE_DIM // 2)

    @pl.when((t == 3) | (t == 4))
    def _():
        rope(QK_DIM_B // 2)


def _project(x_all, pos_all, w_in):
    n, d = x_all.shape
    tm = _pick(n, (768, 512, 384, 256, 128, 64, 32, 16, 8))
    o = np.cumsum((0, 1024, 512, 512, 1024, 64, 16, 512, 512, 1024, 1024))
    wp = jnp.concatenate([w_in[:, o[0]:o[4]], w_in[:, o[6]:o[10]], w_in[:, o[4]:o[6]],
                          jnp.zeros((d, PROJ_W - C_TAIL - 80), w_in.dtype)], axis=1).astype(bf16)
    tabs = jnp.stack([
        _rope_table(pos_all, ROPE_DIM_A, ROPE_THETA, HEAD_DIM_A),
        _rope_table(pos_all, ROPE_DIM_A, ROPE_THETA, HEAD_DIM_A),
        _rope_table(pos_all, IDX_ROPE_DIM, ROPE_THETA, IDX_DIM),
        _rope_table(pos_all, QK_DIM_B, RET_THETA, QK_DIM_B),
        _rope_table(pos_all, QK_DIM_B, RET_THETA, QK_DIM_B, scale=QK_DIM_B ** -0.5),
        _rope_table(pos_all, IDX_ROPE_DIM, ROPE_THETA, IDX_DIM, active=IDX_DIM),
        _rope_table(pos_all, ROPE_DIM_A, ROPE_THETA, HEAD_DIM_A, scale=QA_SCALE),
    ])
    tt = jnp.asarray(PROJ_TILE_TYPES, i32)
    nj = PROJ_W // PROJ_TN
    return pl.pallas_call(
        _proj_kernel,
        grid_spec=pltpu.PrefetchScalarGridSpec(
            num_scalar_prefetch=1,
            grid=(n // tm, nj),
            in_specs=[
                pl.BlockSpec((tm, d), lambda i, j, tt: (i, 0)),
                pl.BlockSpec((d, PROJ_TN), lambda i, j, tt: (0, j)),
                pl.BlockSpec((1, tm, 3 * LANES), lambda i, j, tt: (tt[j], i, 0)),
            ],
            out_specs=[
                pl.BlockSpec((tm, PROJ_TN), lambda i, j, tt: (i, j)),
                pl.BlockSpec((tm, PROJ_TN), lambda i, j, tt: (i, j)),
            ],
            scratch_shapes=[pltpu.VMEM((tm, d), bf16)],
        ),
        out_shape=[jax.ShapeDtypeStruct((n, PROJ_W), f32), jax.ShapeDtypeStruct((n, PROJ_W), bf16)],
        compiler_params=_cparams(("arbitrary", "arbitrary")),
        name="proj",
    )(tt, x_all, wp, tabs)


def _key_to_float(key):
    return pltpu.bitcast(key ^ ((key >> 31) & 0x7FFFFFFF), f32)


def _kth_threshold(count_ge, shape, k):
    def body(step, ans):
        cand = ans + jnp.left_shift(jnp.int32(1), 31 - step)
        return jnp.where(count_ge(_key_to_float(cand)) >= k, cand, ans)

    ans = lax.fori_loop(0, 32, body, jnp.full(shape, INT_MIN, i32))
    return _key_to_float(jnp.maximum(ans, KEY_NEG_INF + 1))


def _lane_blocks(x):
    return [x[:, i * LANES:(i + 1) * LANES] for i in range(x.shape[1] // LANES)]


def _flash_update(qs, ks, vs, bias, m_scr, l_scr, acc_scr, batched):
    n = len(qs)
    score = lambda g: lax.dot_general(qs[g], ks[g], NT_DIMS, preferred_element_type=f32) + bias

    def update(g, sm):
        m_old = m_scr[g]
        m_new = jnp.maximum(m_old, jnp.max(functools.reduce(jnp.maximum, _lane_blocks(sm)), axis=1, keepdims=True))
        alpha = jnp.exp2(m_old - m_new)
        p = jnp.exp2(sm - m_new)
        l_scr[g] = alpha * l_scr[g] + jnp.sum(functools.reduce(jnp.add, _lane_blocks(p)), axis=1, keepdims=True)
        m_scr[g] = m_new
        acc_scr[g] = alpha * acc_scr[g] + jnp.dot(p.astype(bf16), vs[g], preferred_element_type=f32)

    if batched:
        sms = [score(g) for g in range(n)]
        for g in range(n):
            update(g, sms[g])
    else:
        for g in range(n):
            update(g, score(g))


def _dsa_prompt_kernel(iq_ref, iw_ref, qa_ref, ikd_ref, ka_ref, vat_ref, o_ref,
                       qst, qgt, sct, m_scr, l_scr, acc_scr, *, tq, tk, topk):
    i = pl.program_id(0)
    n_chunks = (i * tq + tq + tk - 1) // tk
    row = lax.broadcasted_iota(i32, (LANES, tq), 0)

    for p in range(IDX_HEADS // 2):
        blk = iq_ref[:, p * LANES:(p + 1) * LANES].T
        qst[:, (2 * p) * tq:(2 * p + 1) * tq] = jnp.where(row < IDX_DIM, blk, 0.0).astype(bf16)
        qst[:, (2 * p + 1) * tq:(2 * p + 2) * tq] = jnp.where(row >= IDX_DIM, blk, 0.0).astype(bf16)
    wt = iw_ref[...].T * (IDX_DIM ** -0.5 * IDX_HEADS ** -0.5)
    for h in range(N_HEADS_A):
        qgt[h // 2, :, (h % 2) * tq:(h % 2 + 1) * tq] = qa_ref[:, h * LANES:(h + 1) * LANES].T.astype(bf16)

    qpos = i * tq + lax.broadcasted_iota(i32, (tk, tq), 1)

    def score_body(c, carry):
        k0 = pl.multiple_of(c * tk, tk)
        logits = jnp.dot(ikd_ref[pl.ds(k0, tk), :], qst[...], preferred_element_type=f32)
        acc = jnp.zeros((tk, tq), f32)
        for h in range(IDX_HEADS):
            acc = acc + jnp.maximum(logits[:, h * tq:(h + 1) * tq], 0.0) * wt[IDX_DIM + h:IDX_DIM + h + 1, :]
        kpos = k0 + lax.broadcasted_iota(i32, (tk, tq), 0)
        sct[c] = jnp.where(kpos <= qpos, acc, -jnp.inf)
        return carry

    lax.fori_loop(0, n_chunks, score_body, 0)

    def count_ge(cand):
        cb = jnp.broadcast_to(cand, (8, tq))

        def body(c, accs):
            accs = list(accs)
            for r in range(tk // 8):
                accs[r % 4] = accs[r % 4] + jnp.where(sct[c, r * 8:(r + 1) * 8, :] >= cb, 1.0, 0.0)
            return tuple(accs)

        accs = lax.fori_loop(0, n_chunks, body, (jnp.zeros((8, tq), f32),) * 4)
        return jnp.sum((accs[0] + accs[1]) + (accs[2] + accs[3]), axis=0, keepdims=True)

    thr = _kth_threshold(count_ge, (1, tq), float(topk))

    m_scr[...] = jnp.full(m_scr.shape, NEG_BIG, f32)
    l_scr[...] = jnp.zeros(l_scr.shape, f32)
    acc_scr[...] = jnp.zeros(acc_scr.shape, f32)

    def att_body(c, carry):
        k0 = pl.multiple_of(c * tk, tk)
        bias1 = jnp.where(sct[c] >= thr, 0.0, NEG_BIG)
        bias = jnp.concatenate([bias1, bias1], axis=1)
        groups = range(N_KV_A)
        sms = [jnp.dot(ka_ref[pl.ds(k0, tk), g * LANES:(g + 1) * LANES], qgt[g], preferred_element_type=f32) + bias
               for g in groups]
        m_new = [jnp.maximum(m_scr[g], jnp.max(sms[g], axis=0, keepdims=True)) for g in groups]
        alpha = [jnp.exp2(m_scr[g] - m_new[g]) for g in groups]
        ps = [jnp.exp2(sms[g] - m_new[g]) for g in groups]
        for g in groups:
            l_scr[g] = alpha[g] * l_scr[g] + jnp.sum(ps[g], axis=0, keepdims=True)
            m_scr[g] = m_new[g]
        for g in groups:
            acc_scr[g] = alpha[g] * acc_scr[g] + jnp.dot(vat_ref[c, g * LANES:(g + 1) * LANES, :], ps[g].astype(bf16),
                                                         preferred_element_type=f32)
        return carry

    lax.fori_loop(0, n_chunks, att_body, 0)

    for g in range(N_KV_A):
        o = acc_scr[g] / l_scr[g]
        o_ref[:, (2 * g) * LANES:(2 * g + 1) * LANES] = o[:, 0:tq].T.astype(o_ref.dtype)
        o_ref[:, (2 * g + 1) * LANES:(2 * g + 2) * LANES] = o[:, tq:2 * tq].T.astype(o_ref.dtype)


def _dsa_prompt(u_f, u_b, t):
    tq = _pick(t, (128,))
    tk = _pick(t, (512, 256, 128))
    topk = min(TOPK_MAX, t // 4)
    wa = N_HEADS_A * HEAD_DIM_A
    wkv = N_KV_A * HEAD_DIM_A
    ik_b = u_b[:t, C_TAIL:C_TAIL + IDX_DIM]
    ikd = jnp.concatenate([ik_b, ik_b], axis=1)
    vat = u_b[:t, C_VA:C_VA + wkv].reshape(t // tk, tk, wkv).transpose(0, 2, 1)
    kern = functools.partial(_dsa_prompt_kernel, tq=tq, tk=tk, topk=topk)
    one = pl.Buffered(1)
    return pl.pallas_call(
        kern,
        grid=(t // tq,),
        in_specs=[
            pl.BlockSpec((tq, IDX_HEADS * IDX_DIM), lambda i: (i, C_IQ // (IDX_HEADS * IDX_DIM))),
            pl.BlockSpec((tq, LANES), lambda i: (i, C_TAIL // LANES)),
            pl.BlockSpec((tq, wa), lambda i: (i, C_QA // wa)),
            pl.BlockSpec((t, LANES), lambda i: (0, 0), pipeline_mode=one),
            pl.BlockSpec((t, wkv), lambda i: (0, C_KA // wkv), pipeline_mode=one),
            pl.BlockSpec((t // tk, wkv, tk), lambda i: (0, 0, 0), pipeline_mode=one),
        ],
        out_specs=pl.BlockSpec((tq, wa), lambda i: (i, 0)),
        out_shape=jax.ShapeDtypeStruct((t, wa), bf16),
        scratch_shapes=[
            pltpu.VMEM((LANES, IDX_HEADS * tq), bf16),
            pltpu.VMEM((N_KV_A, LANES, 2 * tq), bf16),
            pltpu.VMEM((t // tk, tk, tq), f32),
            pltpu.VMEM((N_KV_A, 1, 2 * tq), f32),
            pltpu.VMEM((N_KV_A, 1, 2 * tq), f32),
            pltpu.VMEM((N_KV_A, LANES, 2 * tq), f32),
        ],
        compiler_params=_cparams(("arbitrary",)),
        name="dsa_prompt",
    )(u_f, u_f, u_f, ikd, u_b, vat)


def _dsa_sample_score_kernel(pt_ref, *refs, tn, ppg, topk, n_groups):
    pages = refs[:ppg]
    iq_ref, tail_ref, sc_ref, thr_ref, qs, wst = refs[ppg:]
    c = pl.program_id(1)
    wscale = IDX_DIM ** -0.5 * IDX_HEADS ** -0.5
    kw = ppg * PAGE_SIZE

    @pl.when(c == 0)
    def _():
        iq = iq_ref[...]
        w = tail_ref[...]
        for h in range(IDX_HEADS):
            qs[h * tn:(h + 1) * tn, :] = iq[:, h * IDX_DIM:(h + 1) * IDX_DIM]
            wst[h * tn:(h + 1) * tn, :] = jnp.broadcast_to(w[:, IDX_DIM + h:IDX_DIM + h + 1] * wscale, (tn, LANES))

    def head_sum(logits):
        width = logits.shape[1]
        acc = jnp.zeros((tn, width), f32)
        for h in range(IDX_HEADS):
            wh = wst[h * tn:(h + 1) * tn, :]
            acc = acc + jnp.maximum(logits[h * tn:(h + 1) * tn, :], 0.0) * jnp.concatenate([wh] * (width // LANES), axis=1)
        return acc

    keys_t = jnp.concatenate([p[...] for p in pages], axis=1).astype(bf16)
    sc_ref[0, c] = head_sum(jnp.dot(qs[...].astype(bf16), keys_t, preferred_element_type=f32))

    @pl.when(c == n_groups - 1)
    def _():
        ik_new = tail_ref[:, 0:IDX_DIM].astype(bf16)
        kpad = jnp.concatenate([ik_new, jnp.zeros((LANES - tn, IDX_DIM), bf16)], axis=0)
        s_new = head_sum(lax.dot_general(qs[...].astype(bf16), kpad, NT_DIMS, preferred_element_type=f32))
        qi = lax.broadcasted_iota(i32, (tn, LANES), 0)
        kj = lax.broadcasted_iota(i32, (tn, LANES), 1)
        snew = jnp.where(kj <= qi, s_new, -jnp.inf)
        sc_ref[0, n_groups] = jnp.concatenate([snew, jnp.full((tn, kw - LANES), -jnp.inf, f32)], axis=1)

        def count_ge(cand):
            cb = jnp.broadcast_to(cand, (tn, LANES))

            def body(cc, acc):
                blk = sc_ref[0, cc]
                for q in range(kw // LANES):
                    acc = acc + jnp.where(blk[:, q * LANES:(q + 1) * LANES] >= cb, 1.0, 0.0)
                return acc

            acc = lax.fori_loop(0, n_groups + 1, body, jnp.zeros((tn, LANES), f32))
            return jnp.sum(acc, axis=1, keepdims=True)

        thr = _kth_threshold(count_ge, (tn, 1), float(topk))
        thr_ref[0] = jnp.broadcast_to(thr, (tn, LANES))


def _dsa_sample_attn_kernel(pt_ref, *refs, tn, ppg, n_groups):
    kpages = refs[:ppg]
    vpages = refs[ppg:2 * ppg]
    qa_ref, kn_ref, vn_ref, sc_ref, scn_ref, thr_ref, o_ref, qg, m_scr, l_scr, acc_scr = refs[2 * ppg:]
    c = pl.program_id(1)

    @pl.when(c == 0)
    def _():
        qa = qa_ref[...]
        for g in range(N_KV_A):
            qg[g, 0:tn, :] = qa[:, (2 * g) * LANES:(2 * g + 1) * LANES]
            qg[g, tn:2 * tn, :] = qa[:, (2 * g + 1) * LANES:(2 * g + 2) * LANES]
        m_scr[...] = jnp.full(m_scr.shape, NEG_BIG, f32)
        l_scr[...] = jnp.zeros(l_scr.shape, f32)
        acc_scr[...] = jnp.zeros(acc_scr.shape, f32)

    thr = thr_ref[0][:, 0:1]
    thr2 = jnp.concatenate([thr, thr], axis=0)

    def attend(ks, vs, sc):
        bias = jnp.where(jnp.concatenate([sc, sc], axis=0) >= thr2, 0.0, NEG_BIG)
        _flash_update([qg[g].astype(bf16) for g in range(N_KV_A)], ks, vs, bias, m_scr, l_scr, acc_scr, batched=True)

    group = lambda pages, g: jnp.concatenate(
        [p[pl.ds(g, PAGE_SIZE, stride=N_KV_A), :] for p in pages], axis=0).astype(bf16)
    attend([group(kpages, g) for g in range(N_KV_A)], [group(vpages, g) for g in range(N_KV_A)], sc_ref[0, 0])

    @pl.when(c == n_groups - 1)
    def _():
        zpad = jnp.zeros((LANES - tn, N_KV_A * HEAD_DIM_A), bf16)
        attend(_lane_blocks(jnp.concatenate([kn_ref[...].astype(bf16), zpad], axis=0)),
               _lane_blocks(jnp.concatenate([vn_ref[...].astype(bf16), zpad], axis=0)),
               scn_ref[0, 0][:, 0:LANES])
        for g in range(N_KV_A):
            o = acc_scr[g] / l_scr[g]
            o_ref[:, (2 * g) * LANES:(2 * g + 1) * LANES] = o[0:tn]
            o_ref[:, (2 * g + 1) * LANES:(2 * g + 2) * LANES] = o[tn:2 * tn]


def _dsa_sample(u_f, row0, bd, tn, cache_k, cache_v, idx_kt, page_table):
    n_pages = page_table.shape[1]
    past = n_pages * PAGE_SIZE
    topk = min(TOPK_MAX, (past + tn) // 4)
    ppa = _pick(n_pages, (16, 8, 4, 2, 1))
    pps = _pick(n_pages, (32, 16, 8, 4, 2, 1))
    nga, ngs = n_pages // ppa, n_pages // pps
    kwa, kws = ppa * PAGE_SIZE, pps * PAGE_SIZE
    per = kws // kwa
    wkv = N_KV_A * HEAD_DIM_A
    wa = N_HEADS_A * HEAD_DIM_A
    rb0 = row0 // tn

    def idx_spec(j):
        return pl.BlockSpec((None, IDX_DIM, PAGE_SIZE), lambda b, c, pt, j=j: (pt[b, c * pps + j], 0, 0))

    def kv_spec(j):
        return pl.BlockSpec((None, PAGE_SIZE * N_KV_A, HEAD_DIM_A), lambda b, c, pt, j=j: (pt[b, c * ppa + j], 0, 0))

    score = pl.pallas_call(
        functools.partial(_dsa_sample_score_kernel, tn=tn, ppg=pps, topk=topk, n_groups=ngs),
        grid_spec=pltpu.PrefetchScalarGridSpec(
            num_scalar_prefetch=1,
            grid=(bd, ngs),
            in_specs=[idx_spec(j) for j in range(pps)] + [
                pl.BlockSpec((tn, IDX_HEADS * IDX_DIM), lambda b, c, pt: (rb0 + b, C_IQ // (IDX_HEADS * IDX_DIM))),
                pl.BlockSpec((tn, LANES), lambda b, c, pt: (rb0 + b, C_TAIL // LANES)),
            ],
            out_specs=[
                pl.BlockSpec((1, ngs + 1, tn, kws), lambda b, c, pt: (b, 0, 0, 0)),
                pl.BlockSpec((1, tn, LANES), lambda b, c, pt: (b, 0, 0)),
            ],
            scratch_shapes=[pltpu.VMEM((IDX_HEADS * tn, IDX_DIM), f32), pltpu.VMEM((IDX_HEADS * tn, LANES), f32)],
        ),
        out_shape=[jax.ShapeDtypeStruct((bd, ngs + 1, tn, kws), f32), jax.ShapeDtypeStruct((bd, tn, LANES), f32)],
        compiler_params=_cparams(("arbitrary", "arbitrary")),
        name="dsa_sample_score",
    )
    scores, thr = score(page_table, *([idx_kt] * pps), u_f, u_f)

    attn = pl.pallas_call(
        functools.partial(_dsa_sample_attn_kernel, tn=tn, ppg=ppa, n_groups=nga),
        grid_spec=pltpu.PrefetchScalarGridSpec(
            num_scalar_prefetch=1,
            grid=(bd, nga),
            in_specs=[kv_spec(j) for j in range(ppa)] + [kv_spec(j) for j in range(ppa)] + [
                pl.BlockSpec((tn, wa), lambda b, c, pt: (rb0 + b, C_QA // wa)),
                pl.BlockSpec((tn, wkv), lambda b, c, pt: (rb0 + b, C_KA // wkv)),
                pl.BlockSpec((tn, wkv), lambda b, c, pt: (rb0 + b, C_VA // wkv)),
                pl.BlockSpec((1, 1, tn, kwa), lambda b, c, pt: (b, c // per, 0, c % per)),
                pl.BlockSpec((1, 1, tn, kwa), lambda b, c, pt: (b, ngs, 0, 0)),
                pl.BlockSpec((1, tn, LANES), lambda b, c, pt: (b, 0, 0)),
            ],
            out_specs=pl.BlockSpec((tn, wa), lambda b, c, pt: (b, 0)),
            scratch_shapes=[
                pltpu.VMEM((N_KV_A, 2 * tn, LANES), f32),
                pltpu.VMEM((N_KV_A, 2 * tn, 1), f32),
                pltpu.VMEM((N_KV_A, 2 * tn, 1), f32),
                pltpu.VMEM((N_KV_A, 2 * tn, LANES), f32),
            ],
        ),
        out_shape=jax.ShapeDtypeStruct((bd * tn, wa), f32),
        compiler_params=_cparams(("arbitrary", "arbitrary")),
        name="dsa_sample_attn",
    )
    return attn(page_table, *([cache_k] * ppa), *([cache_v] * ppa), u_f, u_f, u_f, scores, scores, thr)


def _ret_kernel(q_ref, k_ref, v_ref, g_ref, gnw_ref, dmask_ref, qdec_ref, kdec_ref, sdec_ref, s0_ref,
                o_ref, sout_ref, s_scr, *, n_chunks):
    c = pl.program_id(1)

    @pl.when(c == 0)
    def _():
        s_scr[...] = s0_ref[0]

    for h in range(N_HEADS_B):
        q = q_ref[:, h * QK_DIM_B:(h + 1) * QK_DIM_B]
        k = k_ref[:, h * QK_DIM_B:(h + 1) * QK_DIM_B]
        v = v_ref[:, h * V_DIM_B:(h + 1) * V_DIM_B].astype(bf16)
        gate = g_ref[:, h * V_DIM_B:(h + 1) * V_DIM_B]
        qb = q.astype(bf16)
        att = lax.dot_general(qb, k.astype(bf16), NT_DIMS, preferred_element_type=f32) * dmask_ref[h]
        s_old = s_scr[h]
        o = (jnp.dot(att.astype(bf16), v, preferred_element_type=f32)
             + jnp.dot(qb, s_old.astype(bf16), preferred_element_type=f32) * qdec_ref[h])
        kd = (k * kdec_ref[h]).astype(bf16)
        s_scr[h] = s_old * sdec_ref[h] + lax.dot_general(kd, v, TN_DIMS, preferred_element_type=f32)
        mu = jnp.mean(o, axis=-1, keepdims=True)
        var = jnp.mean(jnp.square(o - mu), axis=-1, keepdims=True)
        rb = (o - mu) * lax.rsqrt(var + LN_EPS) * gnw_ref[:, h * V_DIM_B:(h + 1) * V_DIM_B]
        rb = rb * (gate / (1.0 + jnp.exp(-gate)))
        o_ref[:, h * V_DIM_B:(h + 1) * V_DIM_B] = rb.astype(o_ref.dtype)

    @pl.when(c == n_chunks - 1)
    def _():
        sout_ref[0] = s_scr[...]


def _retention(u_f, row0, nb, t, state0, gn_w, out_dtype):
    ch = min(RET_CHUNK, t)
    if t % ch:
        ch = t
    n = t // ch
    hb = N_HEADS_B
    lg = jnp.log1p(-jnp.exp2(-5.0 - jnp.arange(hb, dtype=f32)))
    i = jnp.arange(ch)
    diff = i[:, None] - i[None, :]
    dmask = jnp.where(diff >= 0, jnp.exp(lg[:, None, None] * jnp.maximum(diff, 0)), 0.0)
    qdec = jnp.broadcast_to(jnp.exp(lg[:, None] * (i + 1))[:, :, None], (hb, ch, V_DIM_B))
    kdec = jnp.broadcast_to(jnp.exp(lg[:, None] * (ch - 1 - i))[:, :, None], (hb, ch, QK_DIM_B))
    sdec = jnp.broadcast_to(jnp.exp(lg * ch)[:, None, None], (hb, 1, V_DIM_B))
    wqk = hb * QK_DIM_B
    wv = hb * V_DIM_B
    rb0 = row0 // ch
    full3 = lambda shp: pl.BlockSpec(shp, lambda b, c: (0, 0, 0))
    return pl.pallas_call(
        functools.partial(_ret_kernel, n_chunks=n),
        grid=(nb, n),
        in_specs=[
            pl.BlockSpec((ch, wqk), lambda b, c: (rb0 + b * n + c, C_QB // wqk)),
            pl.BlockSpec((ch, wqk), lambda b, c: (rb0 + b * n + c, C_KB // wqk)),
            pl.BlockSpec((ch, wv), lambda b, c: (rb0 + b * n + c, C_VB // wv)),
            pl.BlockSpec((ch, wv), lambda b, c: (rb0 + b * n + c, C_GB // wv)),
            pl.BlockSpec((1, wv), lambda b, c: (0, 0)),
            full3((hb, ch, ch)), full3((hb, ch, V_DIM_B)), full3((hb, ch, QK_DIM_B)), full3((hb, 1, V_DIM_B)),
            pl.BlockSpec((1, hb, QK_DIM_B, V_DIM_B), lambda b, c: (b, 0, 0, 0)),
        ],
        out_specs=[
            pl.BlockSpec((ch, wv), lambda b, c: (b * n + c, 0)),
            pl.BlockSpec((1, hb, QK_DIM_B, V_DIM_B), lambda b, c: (b, 0, 0, 0)),
        ],
        out_shape=[jax.ShapeDtypeStruct((nb * t, wv), out_dtype),
                   jax.ShapeDtypeStruct((nb, hb, QK_DIM_B, V_DIM_B), f32)],
        scratch_shapes=[pltpu.VMEM((hb, QK_DIM_B, V_DIM_B), f32)],
        compiler_params=_cparams(("arbitrary", "arbitrary")),
        name="retention",
    )(u_f, u_f, u_f, u_f, gn_w.reshape(1, wv), dmask, qdec, kdec, sdec, state0)


def _split_hi_lo(a):
    hi = a.astype(bf16)
    return hi, (a - hi.astype(f32)).astype(bf16)


def _tail1_kernel(attn_ref, rb_ref, x_ref, wo_ref, g1_ref, b1_ref, wrh_ref, wrl_ref, br_ref,
                  h_ref, hq_ref, eid_ref, gate_ref, *, tm, n_exp, alpha, wa):
    mix = (jnp.dot(attn_ref[...], wo_ref[0:wa, :], preferred_element_type=f32)
           + jnp.dot(rb_ref[...], wo_ref[wa:, :], preferred_element_type=f32))
    z = alpha * x_ref[...] + mix
    mu = jnp.mean(z, axis=-1, keepdims=True)
    var = jnp.mean(jnp.square(z - mu), axis=-1, keepdims=True)
    h = (z - mu) * lax.rsqrt(var + LN_EPS) * g1_ref[...] + b1_ref[...]
    h_ref[...] = h
    nq = h.shape[1] // LANES
    for j in range(nq):
        hq_ref[pl.ds(j, tm, stride=nq), :] = h[:, j * LANES:(j + 1) * LANES]

    hh, hl = _split_hi_lo(h)
    logits = (jnp.dot(hh, wrh_ref[...], preferred_element_type=f32)
              + jnp.dot(hl, wrh_ref[...], preferred_element_type=f32)
              + jnp.dot(hh, wrl_ref[...], preferred_element_type=f32)) + br_ref[...]
    lane = lax.broadcasted_iota(i32, (tm, LANES), 1)
    lanef = lane.astype(f32)
    logits = jnp.where(lane < n_exp, logits, -jnp.inf)
    vals, ids = [], []
    for _ in range(TOP_K):
        m = jnp.max(logits, axis=1, keepdims=True)
        idx = jnp.min(jnp.where(logits == m, lanef, float(LANES)), axis=1, keepdims=True)
        vals.append(m)
        ids.append(idx)
        logits = jnp.where(lanef == idx, -jnp.inf, logits)
    es = [jnp.exp(v - vals[0]) for v in vals]
    den = es[0] + es[1] + es[2] + es[3]
    eid_ref[...] = jnp.concatenate(ids, axis=1).astype(i32)
    gate_ref[...] = jnp.concatenate([e / den for e in es], axis=1)


def _tail1(attn_b, rb_b, x_all, w_o, ln1_g, ln1_b, w_router, b_router, alpha):
    n, d = x_all.shape
    wa = attn_b.shape[1]
    n_exp = w_router.shape[1]
    tm = _pick(n, (256, 128, 64, 32, 16, 8))
    nq = d // LANES
    wr = jnp.zeros((d, LANES), f32).at[:, :n_exp].set(w_router)
    wrh, wrl = _split_hi_lo(wr)
    br = jnp.zeros((1, LANES), f32).at[0, :n_exp].set(b_router)
    row = lambda w: pl.BlockSpec((1, w), lambda i: (0, 0))
    return pl.pallas_call(
        functools.partial(_tail1_kernel, tm=tm, n_exp=n_exp, alpha=alpha, wa=wa),
        grid=(n // tm,),
        in_specs=[
            pl.BlockSpec((tm, wa), lambda i: (i, 0)),
            pl.BlockSpec((tm, rb_b.shape[1]), lambda i: (i, 0)),
            pl.BlockSpec((tm, d), lambda i: (i, 0)),
            pl.BlockSpec(w_o.shape, lambda i: (0, 0), pipeline_mode=pl.Buffered(1)),
            row(d), row(d),
            pl.BlockSpec((d, LANES), lambda i: (0, 0)), pl.BlockSpec((d, LANES), lambda i: (0, 0)), row(LANES),
        ],
        out_specs=[
            pl.BlockSpec((tm, d), lambda i: (i, 0)),
            pl.BlockSpec((tm * nq, LANES), lambda i: (i, 0)),
            pl.BlockSpec((tm, TOP_K), lambda i: (i, 0)),
            pl.BlockSpec((tm, TOP_K), lambda i: (i, 0)),
        ],
        out_shape=[jax.ShapeDtypeStruct((n, d), f32), jax.ShapeDtypeStruct((n * nq, LANES), f32),
                   jax.ShapeDtypeStruct((n, TOP_K), i32), jax.ShapeDtypeStruct((n, TOP_K), f32)],
        compiler_params=_cparams(("arbitrary",)),
        name="tail1",
    )(attn_b, rb_b, x_all, w_o.astype(bf16), ln1_g.reshape(1, d), ln1_b.reshape(1, d), wrh, wrl, br)


MOE_R = 2048
MOE_SUB = 256
MOE_TF = 256
MOE_ISSUE_UNROLL = 8
MOE_TILES = (512, 256, 128)


def _moe_plan(eid, n_exp, r_cap):
    n = eid.shape[0]
    p = n * TOP_K
    flat = eid.reshape(p)
    onehot = (flat[:, None] == jnp.arange(n_exp, dtype=i32)[None, :]).astype(i32)
    csum = jnp.cumsum(onehot, axis=0)
    rank = jnp.sum((csum - onehot) * onehot, axis=1)
    counts = csum[-1]
    ngrp = (counts + r_cap - 1) // r_cap
    gend = jnp.cumsum(ngrp)
    gstart = gend - ngrp
    g_of = gstart[flat] + rank // r_cap
    slot = rank % r_cap
    n_groups = n_exp + p // r_cap
    pair = jnp.full((n_groups, r_cap), -1, i32).at[g_of, slot].set(jnp.arange(p, dtype=i32))
    t_idx, k_idx = pair // TOP_K, pair % TOP_K
    spare = jnp.broadcast_to(p + jnp.arange(r_cap, dtype=i32) % MOE_SUB, (n_groups, r_cap))
    tok = jnp.where(pair >= 0, t_idx, 0)
    dst = jnp.where(pair >= 0, k_idx * n + t_idx, spare)
    gid = jnp.arange(n_groups, dtype=i32)
    total = gend[-1]
    gclamp = jnp.minimum(gid, total - 1)
    g_exp = jnp.sum((gend[None, :] <= gclamp[:, None]).astype(i32), axis=1)
    g_rows = jnp.clip(counts[g_exp] - (gclamp - gstart[g_exp]) * r_cap, 0, r_cap)
    g_rows = jnp.where(gid < total, g_rows, 0).astype(i32)
    return g_exp, g_rows, tok.reshape(n_groups, 1, r_cap), dst.reshape(n_groups, 1, r_cap)


def _moe_kernel(ge_ref, gr_ref, tok_ref, dst_ref, hq_ref, wgu_ref, wd_ref, bgu_ref, bd_ref, pm_ref, y_ref,
                qbuf, xb, acc, wgu_b, wd_b, sem_in, sem_out, *, nq, nj, spare_row0):
    g = pl.program_id(0)
    j = pl.program_id(1)
    rows = gr_ref[g]
    nsub = (rows + MOE_SUB - 1) // MOE_SUB
    active = rows > 0

    def in_copy(s, i, slot):
        t = tok_ref[0, 0, s * MOE_SUB + i]
        return pltpu.make_async_copy(hq_ref.at[pl.ds(pl.multiple_of(t * nq, nq), nq), :],
                                     qbuf.at[slot, pl.ds(pl.multiple_of(i * nq, nq), nq), :], sem_in.at[slot])

    def out_copy(s, i, slot):
        d = dst_ref[0, 0, s * MOE_SUB + i]
        return pltpu.make_async_copy(qbuf.at[slot, pl.ds(pl.multiple_of(i * nq, nq), nq), :],
                                     y_ref.at[pl.ds(pl.multiple_of(d * nq, nq), nq), :], sem_out.at[slot])

    def for_rows(fn):
        def body(b, c):
            for u in range(MOE_ISSUE_UNROLL):
                fn(b * MOE_ISSUE_UNROLL + u)
            return c

        lax.fori_loop(0, MOE_SUB // MOE_ISSUE_UNROLL, body, 0)

    def wait_in(slot):
        pltpu.make_async_copy(hq_ref.at[pl.ds(0, MOE_SUB * nq), :], qbuf.at[slot], sem_in.at[slot]).wait()

    def wait_out(slot):
        pltpu.make_async_copy(qbuf.at[slot], y_ref.at[pl.ds(0, MOE_SUB * nq), :], sem_out.at[slot]).wait()

    def convert(s, slot):
        r0 = pl.multiple_of(s * MOE_SUB, MOE_SUB)
        for jj in range(nq):
            xb[pl.ds(r0, MOE_SUB), jj * LANES:(jj + 1) * LANES] = (
                qbuf[slot, pl.ds(jj, MOE_SUB, stride=nq), :].astype(bf16))
        acc[pl.ds(r0, MOE_SUB), :] = jnp.zeros((MOE_SUB, acc.shape[1]), f32)

    def stage(s, slot):
        r0 = pl.multiple_of(s * MOE_SUB, MOE_SUB)
        a = acc[pl.ds(r0, MOE_SUB), :] + bd_ref[0]
        for jj in range(nq):
            qbuf[slot, pl.ds(jj, MOE_SUB, stride=nq), :] = a[:, jj * LANES:(jj + 1) * LANES]

    npairs = (nsub + 1) // 2

    @pl.when((g == 0) & (j == 0))
    def _():
        qbuf[1] = jnp.zeros(qbuf.shape[1:], f32)
        fill = pltpu.make_async_copy(qbuf.at[1], y_ref.at[pl.ds(spare_row0, MOE_SUB * nq), :], sem_out.at[1])
        fill.start()
        fill.wait()

    def tile(r0, size):
        gu = jnp.dot(xb[pl.ds(r0, size), :], wgu_b[...], preferred_element_type=f32) + bgu_ref[0]
        gub = gu.astype(bf16)
        gates, ups = [], []
        for q in range(2 * MOE_TF // 256):
            de = jnp.dot(gub[:, q * 256:(q + 1) * 256], pm_ref[...], preferred_element_type=f32)
            gates.append(de[:, 0:LANES])
            ups.append(de[:, LANES:2 * LANES])
        gate = jnp.minimum(jnp.concatenate(gates, axis=1), SWIGLU_LIMIT)
        up = jnp.clip(jnp.concatenate(ups, axis=1), -SWIGLU_LIMIT, SWIGLU_LIMIT)
        act = (up + 1.0) * gate * (1.0 / (1.0 + jnp.exp(-SWIGLU_ALPHA * gate)))
        acc[pl.ds(r0, size), :] += jnp.dot(act.astype(bf16), wd_b[...], preferred_element_type=f32)

    sub_tile = lambda s: tile(pl.multiple_of(s * MOE_SUB, MOE_SUB), MOE_SUB)

    @pl.when(active)
    def _():
        wgu_b[...] = wgu_ref[0].astype(bf16)
        wd_b[...] = wd_ref[0].astype(bf16)

    @pl.when(active & (j == 0))
    def _():
        for_rows(lambda i: in_copy(0, i, 0).start())

        @pl.when(nsub > 1)
        def _():
            for_rows(lambda i: in_copy(1, i, 1).start())

        def pair(pp, c):
            for slot in (0, 1):
                s = 2 * pp + slot

                @pl.when(s < nsub)
                def _(s=s, slot=slot):
                    wait_in(slot)
                    convert(s, slot)

                    @pl.when(s + 2 < nsub)
                    def _():
                        for_rows(lambda i: in_copy(s + 2, i, slot).start())

                    sub_tile(s)

            return c

        lax.fori_loop(0, npairs, pair, 0)

    @pl.when(active & (j > 0) & (j < nj - 1))
    def _():
        big, small = MOE_TILES[0], MOE_TILES[-1]
        padded = (rows + small - 1) // small * small
        nbig = padded // big

        def big_tile(s, c):
            tile(pl.multiple_of(s * big, big), big)
            return c

        lax.fori_loop(0, nbig, big_tile, 0)
        done = nbig * big
        for size in MOE_TILES[1:]:
            take = (padded - done) >= size

            @pl.when(take)
            def _(done=done, size=size):
                tile(pl.multiple_of(done, small), size)

            done = done + jnp.where(take, size, 0)

    @pl.when(active & (j == nj - 1))
    def _():
        def pair(pp, c):
            for slot in (0, 1):
                s = 2 * pp + slot

                @pl.when(s < nsub)
                def _(s=s, slot=slot):
                    sub_tile(s)

                    @pl.when(pp > 0)
                    def _():
                        wait_out(slot)

                    stage(s, slot)
                    for_rows(lambda i: out_copy(s, i, slot).start())

            return c

        lax.fori_loop(0, npairs, pair, 0)
        wait_out(0)

        @pl.when(nsub >= 2)
        def _():
            wait_out(1)


def _deinterleave_matrix():
    pm = np.zeros((256, 256), np.float32)
    i = np.arange(LANES)
    pm[2 * i, i] = 1.0
    pm[2 * i + 1, LANES + i] = 1.0
    return jnp.asarray(pm, bf16)


def _moe(hq, eid, w_gate_up, b_gate_up, w_down, b_down, n):
    n_exp, d, f2 = w_gate_up.shape
    dff = f2 // 2
    nq = d // LANES
    nj = dff // MOE_TF
    assert nj >= 2, "the first and last hidden chunks carry the row gather and scatter"
    g_exp, g_rows, tok, dst = _moe_plan(eid, n_exp, MOE_R)
    n_groups = g_exp.shape[0]

    def jeff(g, j, gr):
        return jnp.where(gr[g] > 0, j, nj - 1)

    return pl.pallas_call(
        functools.partial(_moe_kernel, nq=nq, nj=nj, spare_row0=TOP_K * n * nq),
        grid_spec=pltpu.PrefetchScalarGridSpec(
            num_scalar_prefetch=2,
            grid=(n_groups, nj),
            in_specs=[
                pl.BlockSpec((1, 1, MOE_R), lambda g, j, ge, gr: (g, 0, 0), memory_space=pltpu.SMEM),
                pl.BlockSpec((1, 1, MOE_R), lambda g, j, ge, gr: (g, 0, 0), memory_space=pltpu.SMEM),
                pl.BlockSpec(memory_space=pl.ANY),
                pl.BlockSpec((1, d, 2 * MOE_TF), lambda g, j, ge, gr: (ge[g], 0, jeff(g, j, gr))),
                pl.BlockSpec((1, MOE_TF, d), lambda g, j, ge, gr: (ge[g], jeff(g, j, gr), 0)),
                pl.BlockSpec((1, 1, 2 * MOE_TF), lambda g, j, ge, gr: (ge[g], 0, jeff(g, j, gr))),
                pl.BlockSpec((1, 1, d), lambda g, j, ge, gr: (ge[g], 0, 0)),
                pl.BlockSpec((256, 256), lambda g, j, ge, gr: (0, 0)),
            ],
            out_specs=pl.BlockSpec(memory_space=pl.ANY),
            scratch_shapes=[
                pltpu.VMEM((2, MOE_SUB * nq, LANES), f32),
                pltpu.VMEM((MOE_R, d), bf16),
                pltpu.VMEM((MOE_R, d), f32),
                pltpu.VMEM((d, 2 * MOE_TF), bf16),
                pltpu.VMEM((MOE_TF, d), bf16),
                pltpu.SemaphoreType.DMA((2,)),
                pltpu.SemaphoreType.DMA((2,)),
            ],
        ),
        out_shape=jax.ShapeDtypeStruct(((TOP_K * n + MOE_SUB) * nq, LANES), f32),
        compiler_params=_cparams(("arbitrary", "arbitrary")),
        name="moe",
    )(g_exp, g_rows, tok, dst, hq, w_gate_up, w_down, b_gate_up.reshape(n_exp, 1, f2),
      b_down.reshape(n_exp, 1, d), _deinterleave_matrix())


def _final_kernel(h_ref, y0_ref, y1_ref, y2_ref, y3_ref, gate_ref, g2_ref, b2_ref, op_ref, os_ref, *, tm, nq, alpha, nbp):
    gates = gate_ref[...]
    f = jnp.zeros(h_ref.shape, f32)
    for k, y_ref in enumerate((y0_ref, y1_ref, y2_ref, y3_ref)):
        yk = jnp.concatenate([y_ref[pl.ds(jj, tm, stride=nq), :] for jj in range(nq)], axis=1)
        f = f + gates[:, k:k + 1] * yk
    z = alpha * h_ref[...] + f
    mu = jnp.mean(z, axis=-1, keepdims=True)
    var = jnp.mean(jnp.square(z - mu), axis=-1, keepdims=True)
    y = (z - mu) * lax.rsqrt(var + LN_EPS) * g2_ref[...] + b2_ref[...]
    i = pl.program_id(0)

    @pl.when(i < nbp)
    def _():
        op_ref[...] = y

    @pl.when(i >= nbp)
    def _():
        os_ref[...] = y


def _final(h, y4q, gates, ln2_g, ln2_b, alpha, n_prompt):
    n, d = h.shape
    nq = d // LANES
    tm = _pick(np.gcd(n_prompt, n - n_prompt), (256, 128, 64, 32, 16, 8))
    nb, nbp = n // tm, n_prompt // tm
    row = pl.BlockSpec((1, d), lambda i: (0, 0))
    yspec = lambda k: pl.BlockSpec((tm * nq, LANES), lambda i, k=k: (k * nb + i, 0))
    return pl.pallas_call(
        functools.partial(_final_kernel, tm=tm, nq=nq, alpha=alpha, nbp=nbp),
        grid=(nb,),
        in_specs=[pl.BlockSpec((tm, d), lambda i: (i, 0)), yspec(0), yspec(1), yspec(2), yspec(3),
                  pl.BlockSpec((tm, TOP_K), lambda i: (i, 0)), row, row],
        out_specs=[pl.BlockSpec((tm, d), lambda i: (jnp.minimum(i, nbp - 1), 0)),
                   pl.BlockSpec((tm, d), lambda i: (jnp.maximum(i - nbp, 0), 0))],
        out_shape=[jax.ShapeDtypeStruct((n_prompt, d), f32), jax.ShapeDtypeStruct((n - n_prompt, d), f32)],
        compiler_params=_cparams(("arbitrary",)),
        name="final",
    )(h, y4q, y4q, y4q, y4q, gates, ln2_g.reshape(1, d), ln2_b.reshape(1, d))


def kernel(x_prompt, x_sample, cache_k, cache_v, cache_idx_k, state_ret, page_table, w_in, w_o, ret_gn_w,
           ln1_g, ln1_b, w_router, b_router, w_gate_up, b_gate_up, w_down, b_down, ln2_g, ln2_b):
    depth = w_in.shape[0]
    assert depth == 1, "single-layer step"
    bp, t, d = x_prompt.shape
    bd, tn, _ = x_sample.shape
    assert bp == 1
    past = page_table.shape[1] * PAGE_SIZE
    np_, ns = bp * t, bd * tn
    n = np_ + ns
    alpha = (2 * depth) ** 0.25
    wkv = N_KV_A * HEAD_DIM_A
    layer = lambda a: a.reshape(a.shape[1:])

    x_all = jnp.concatenate([x_prompt.reshape(np_, d), x_sample.reshape(ns, d)], axis=0)
    pos_all = jnp.concatenate([jnp.arange(t), jnp.tile(past + jnp.arange(tn), bd)])
    u_f, u_b = _project(x_all, pos_all, layer(w_in))

    attn_p = _dsa_prompt(u_f, u_b, t)
    pages = lambda a: a.reshape(a.shape[1], PAGE_SIZE * N_KV_A, HEAD_DIM_A)
    attn_s = _dsa_sample(u_f, np_, bd, tn, pages(cache_k), pages(cache_v), jnp.swapaxes(layer(cache_idx_k), 1, 2),
                         page_table)

    zero_state = jnp.zeros((bp, N_HEADS_B, QK_DIM_B, V_DIM_B), f32)
    rb_p, s_p = _retention(u_f, 0, bp, t, zero_state, layer(ret_gn_w), bf16)
    rb_s, s_s = _retention(u_f, np_, bd, tn, layer(state_ret), layer(ret_gn_w), f32)

    attn_all = jnp.concatenate([attn_p, attn_s.astype(bf16)], axis=0)
    rb_all = jnp.concatenate([rb_p, rb_s.astype(bf16)], axis=0)
    h, hq, eid, gates = _tail1(attn_all, rb_all, x_all, layer(w_o), layer(ln1_g), layer(ln1_b), layer(w_router),
                               layer(b_router), alpha)
    y4q = _moe(hq, eid, layer(w_gate_up), layer(b_gate_up), layer(w_down), layer(b_down), n)
    y_p, y_s = _final(h, y4q, gates, layer(ln2_g), layer(ln2_b), alpha, np_)

    kv = lambda rows, c0, lead: u_f[rows, c0:c0 + wkv].reshape(lead + (N_KV_A, HEAD_DIM_A))[None]
    ps, ss = slice(0, np_), slice(np_, n)
    return (
        y_p.reshape(bp, t, d), y_s.reshape(bd, tn, d),
        kv(ps, C_KA, (bp, t)), kv(ps, C_VA, (bp, t)), u_f[ps, C_TAIL:C_TAIL + IDX_DIM].reshape(1, bp, t, IDX_DIM),
        s_p[None],
        kv(ss, C_KA, (bd, tn)), kv(ss, C_VA, (bd, tn)), u_f[ss, C_TAIL:C_TAIL + IDX_DIM].reshape(1, bd, tn, IDX_DIM),
        s_s[None],
    )
```

```python
import functools

import numpy as np
import jax
import jax.numpy as jnp
from jax import lax
from jax.experimental import pallas as pl
from jax.experimental.pallas import tpu as pltpu

f32 = jnp.float32
bf16 = jnp.bfloat16
i32 = jnp.int32

PAGE_SIZE = 128
HEAD_DIM_A = 128
N_HEADS_A = 8
N_KV_A = 4
ROPE_DIM_A = 32
ROPE_THETA = 500000.0
IDX_HEADS = 16
IDX_DIM = 64
IDX_ROPE_DIM = 16
TOPK_MAX = 256
V_DIM_B = 128
N_HEADS_B = 8
QK_DIM_B = 64
RET_CHUNK = 128
RET_THETA = 10000.0
TOP_K = 4
SWIGLU_LIMIT = 7.0
SWIGLU_ALPHA = 1.702
LN_EPS = 1e-5

LANES = 128
VMEM_LIMIT = 56 * 1024 * 1024

PROJ_TN = 512
C_QA, C_KA, C_VA, C_IQ, C_QB, C_KB, C_VB, C_GB, C_TAIL, PROJ_W = 0, 1024, 1536, 2048, 3072, 3584, 4096, 5120, 6144, 6656
PROJ_TILE_TYPES = (6, 6, 1, 0, 2, 2, 3, 4, 0, 0, 0, 0, 5)
QA_SCALE = HEAD_DIM_A ** -0.5 * 1.4426950408889634

INT_MIN = -2 ** 31
KEY_NEG_INF = -2139095041
NEG_BIG = -1e30

NT_DIMS = (((1,), (1,)), ((), ()))
TN_DIMS = (((0,), (0,)), ((), ()))


def _pick(n, cands):
    for c in cands:
        if n % c == 0:
            return c
    raise ValueError(f"no tile for {n}")


def _cparams(sem, vmem=VMEM_LIMIT):
    return pltpu.CompilerParams(dimension_semantics=sem, vmem_limit_bytes=vmem)


def _rope_table(pos, rot_dim, theta, period, scale=1.0, active=LANES):
    half = rot_dim // 2
    inv_freq = 1.0 / (theta ** (jnp.arange(half, dtype=f32) / half))
    ang = pos.astype(f32)[:, None] * inv_freq[None, :]
    cos, sin = jnp.cos(ang), jnp.sin(ang)
    lane = np.arange(LANES)
    d = lane % period
    first = (d < half) & (lane < active)
    second = (d >= half) & (d < rot_dim) & (lane < active)
    idx = np.where(first, d, np.where(second, d - half, 0))
    cos_l, sin_l = cos[:, idx], sin[:, idx]
    c = jnp.where(first | second, cos_l, 1.0)
    s1 = jnp.where(second, sin_l, 0.0)
    s2 = jnp.where(first, -sin_l, 0.0)
    return jnp.concatenate([c, s1, s2], axis=1) * scale


def _proj_kernel(tt_ref, x_ref, w_ref, tab_ref, of_ref, ob_ref, xb_scr):
    j = pl.program_id(1)

    @pl.when(j == 0)
    def _():
        xb_scr[...] = x_ref[...].astype(bf16)

    t = tt_ref[j]

    def emit(half):
        piece = 2 * LANES
        for c0 in range(0, PROJ_TN, piece):
            u = jnp.dot(xb_scr[...], w_ref[:, c0:c0 + piece], preferred_element_type=f32)
            if half is not None:
                c = tab_ref[0, :, 0:LANES]
                s1 = tab_ref[0, :, LANES:2 * LANES]
                s2 = tab_ref[0, :, 2 * LANES:3 * LANES]
                outs = []
                for q in range(piece // LANES):
                    uc = u[:, q * LANES:(q + 1) * LANES]
                    outs.append(uc * c + pltpu.roll(uc, half, 1) * s1 + pltpu.roll(uc, LANES - half, 1) * s2)
                u = jnp.concatenate(outs, axis=1)
            of_ref[:, c0:c0 + piece] = u
            ob_ref[:, c0:c0 + piece] = u.astype(bf16)

    @pl.when(t == 0)
    def _():
        emit(None)

    @pl.when((t == 1) | (t == 6))
    def _():
        emit(ROPE_DIM_A // 2)

    @pl.when((t == 2) | (t == 5))
    def _():
        emit(IDX_ROPE_DIM // 2)

    @pl.when((t == 3) | (t == 4))
    def _():
        emit(QK_DIM_B // 2)


def _project(x_all, pos_all, w_in):
    n, d = x_all.shape
    tm = _pick(n, (768, 512, 384, 256, 128, 64, 32, 16, 8))
    o = np.cumsum((0, 1024, 512, 512, 1024, 64, 16, 512, 512, 1024, 1024))
    wp = jnp.concatenate([w_in[:, o[0]:o[4]], w_in[:, o[6]:o[10]], w_in[:, o[4]:o[6]],
                          jnp.zeros((d, PROJ_W - C_TAIL - 80), w_in.dtype)], axis=1).astype(bf16)
    tabs = jnp.stack([
        _rope_table(pos_all, ROPE_DIM_A, ROPE_THETA, HEAD_DIM_A),
        _rope_table(pos_all, ROPE_DIM_A, ROPE_THETA, HEAD_DIM_A),
        _rope_table(pos_all, IDX_ROPE_DIM, ROPE_THETA, IDX_DIM),
        _rope_table(pos_all, QK_DIM_B, RET_THETA, QK_DIM_B),
        _rope_table(pos_all, QK_DIM_B, RET_THETA, QK_DIM_B, scale=QK_DIM_B ** -0.5),
        _rope_table(pos_all, IDX_ROPE_DIM, ROPE_THETA, IDX_DIM, active=IDX_DIM),
        _rope_table(pos_all, ROPE_DIM_A, ROPE_THETA, HEAD_DIM_A, scale=QA_SCALE),
    ])
    tt = jnp.asarray(PROJ_TILE_TYPES, i32)
    nj = PROJ_W // PROJ_TN
    return pl.pallas_call(
        _proj_kernel,
        grid_spec=pltpu.PrefetchScalarGridSpec(
            num_scalar_prefetch=1,
            grid=(n // tm, nj),
            in_specs=[
                pl.BlockSpec((tm, d), lambda i, j, tt: (i, 0)),
                pl.BlockSpec((d, PROJ_TN), lambda i, j, tt: (0, j)),
                pl.BlockSpec((1, tm, 3 * LANES), lambda i, j, tt: (tt[j], i, 0)),
            ],
            out_specs=[
                pl.BlockSpec((tm, PROJ_TN), lambda i, j, tt: (i, j)),
                pl.BlockSpec((tm, PROJ_TN), lambda i, j, tt: (i, j)),
            ],
            scratch_shapes=[pltpu.VMEM((tm, d), bf16)],
        ),
        out_shape=[jax.ShapeDtypeStruct((n, PROJ_W), f32), jax.ShapeDtypeStruct((n, PROJ_W), bf16)],
        compiler_params=_cparams(("arbitrary", "arbitrary")),
        name="proj",
    )(tt, x_all, wp, tabs)


def _key_to_float(key):
    return pltpu.bitcast(key ^ ((key >> 31) & 0x7FFFFFFF), f32)


def _kth_threshold(count_ge, shape, k):
    def body(step, ans):
        cand = ans + jnp.left_shift(jnp.int32(1), 31 - step)
        return jnp.where(count_ge(_key_to_float(cand)) >= k, cand, ans)

    ans = lax.fori_loop(0, 32, body, jnp.full(shape, INT_MIN, i32))
    return _key_to_float(jnp.maximum(ans, KEY_NEG_INF + 1))


def _lane_blocks(x):
    return [x[:, i * LANES:(i + 1) * LANES] for i in range(x.shape[1] // LANES)]


def _flash_update(qs, ks, vs, bias, m_scr, l_scr, acc_scr, batched):
    n = len(qs)
    score = lambda g: lax.dot_general(qs[g], ks[g], NT_DIMS, preferred_element_type=f32) + bias

    def update(g, sm):
        m_old = m_scr[g]
        m_new = jnp.maximum(m_old, jnp.max(functools.reduce(jnp.maximum, _lane_blocks(sm)), axis=1, keepdims=True))
        alpha = jnp.exp2(m_old - m_new)
        p = jnp.exp2(sm - m_new)
        l_scr[g] = alpha * l_scr[g] + jnp.sum(functools.reduce(jnp.add, _lane_blocks(p)), axis=1, keepdims=True)
        m_scr[g] = m_new
        acc_scr[g] = alpha * acc_scr[g] + jnp.dot(p.astype(bf16), vs[g], preferred_element_type=f32)

    if batched:
        sms = [score(g) for g in range(n)]
        for g in range(n):
            update(g, sms[g])
    else:
        for g in range(n):
            update(g, score(g))


def _dsa_prompt_kernel(iq_ref, iw_ref, qa_ref, ikd_ref, ka_ref, vat_ref, o_ref,
                       qst, qgt, sct, m_scr, l_scr, acc_scr, *, tq, tk, topk):
    i = pl.program_id(0)
    n_chunks = (i * tq + tq + tk - 1) // tk
    row = lax.broadcasted_iota(i32, (LANES, tq), 0)

    for p in range(IDX_HEADS // 2):
        blk = iq_ref[:, p * LANES:(p + 1) * LANES].T
        qst[:, (2 * p) * tq:(2 * p + 1) * tq] = jnp.where(row < IDX_DIM, blk, 0.0).astype(bf16)
        qst[:, (2 * p + 1) * tq:(2 * p + 2) * tq] = jnp.where(row >= IDX_DIM, blk, 0.0).astype(bf16)
    wt = iw_ref[...].T * (IDX_DIM ** -0.5 * IDX_HEADS ** -0.5)
    for h in range(N_HEADS_A):
        qgt[h // 2, :, (h % 2) * tq:(h % 2 + 1) * tq] = qa_ref[:, h * LANES:(h + 1) * LANES].T.astype(bf16)

    qpos = i * tq + lax.broadcasted_iota(i32, (tk, tq), 1)

    def score_body(c, carry):
        k0 = pl.multiple_of(c * tk, tk)
        logits = jnp.dot(ikd_ref[pl.ds(k0, tk), :], qst[...], preferred_element_type=f32)
        acc = jnp.zeros((tk, tq), f32)
        for h in range(IDX_HEADS):
            acc = acc + jnp.maximum(logits[:, h * tq:(h + 1) * tq], 0.0) * wt[IDX_DIM + h:IDX_DIM + h + 1, :]
        kpos = k0 + lax.broadcasted_iota(i32, (tk, tq), 0)
        sct[c] = jnp.where(kpos <= qpos, acc, -jnp.inf)
        return carry

    lax.fori_loop(0, n_chunks, score_body, 0)

    def count_ge(cand):
        cb = jnp.broadcast_to(cand, (8, tq))

        def body(c, accs):
            accs = list(accs)
            for r in range(tk // 8):
                accs[r % 4] = accs[r % 4] + jnp.where(sct[c, r * 8:(r + 1) * 8, :] >= cb, 1.0, 0.0)
            return tuple(accs)

        accs = lax.fori_loop(0, n_chunks, body, (jnp.zeros((8, tq), f32),) * 4)
        return jnp.sum((accs[0] + accs[1]) + (accs[2] + accs[3]), axis=0, keepdims=True)

    thr = _kth_threshold(count_ge, (1, tq), float(topk))

    m_scr[...] = jnp.full(m_scr.shape, NEG_BIG, f32)
    l_scr[...] = jnp.zeros(l_scr.shape, f32)
    acc_scr[...] = jnp.zeros(acc_scr.shape, f32)

    def att_body(c, carry):
        k0 = pl.multiple_of(c * tk, tk)
        bias1 = jnp.where(sct[c] >= thr, 0.0, NEG_BIG)
        bias = jnp.concatenate([bias1, bias1], axis=1)
        groups = range(N_KV_A)
        sms = [jnp.dot(ka_ref[pl.ds(k0, tk), g * LANES:(g + 1) * LANES], qgt[g], preferred_element_type=f32) + bias
               for g in groups]
        m_new = [jnp.maximum(m_scr[g], jnp.max(sms[g], axis=0, keepdims=True)) for g in groups]
        alpha = [jnp.exp2(m_scr[g] - m_new[g]) for g in groups]
        ps = [jnp.exp2(sms[g] - m_new[g]) for g in groups]
        for g in groups:
            l_scr[g] = alpha[g] * l_scr[g] + jnp.sum(ps[g], axis=0, keepdims=True)
            m_scr[g] = m_new[g]
        for g in groups:
            acc_scr[g] = alpha[g] * acc_scr[g] + jnp.dot(vat_ref[c, g * LANES:(g + 1) * LANES, :], ps[g].astype(bf16),
                                                         preferred_element_type=f32)
        return carry

    lax.fori_loop(0, n_chunks, att_body, 0)

    for g in range(N_KV_A):
        o = acc_scr[g] / l_scr[g]
        o_ref[:, (2 * g) * LANES:(2 * g + 1) * LANES] = o[:, 0:tq].T.astype(o_ref.dtype)
        o_ref[:, (2 * g + 1) * LANES:(2 * g + 2) * LANES] = o[:, tq:2 * tq].T.astype(o_ref.dtype)


def _dsa_prompt(u_f, u_b, t):
    tq = _pick(t, (128,))
    tk = _pick(t, (512, 256, 128))
    topk = min(TOPK_MAX, t // 4)
    wa = N_HEADS_A * HEAD_DIM_A
    wkv = N_KV_A * HEAD_DIM_A
    ik_b = u_b[:t, C_TAIL:C_TAIL + IDX_DIM]
    ikd = jnp.concatenate([ik_b, ik_b], axis=1)
    vat = u_b[:t, C_VA:C_VA + wkv].reshape(t // tk, tk, wkv).transpose(0, 2, 1)
    kern = functools.partial(_dsa_prompt_kernel, tq=tq, tk=tk, topk=topk)
    one = pl.Buffered(1)
    return pl.pallas_call(
        kern,
        grid=(t // tq,),
        in_specs=[
            pl.BlockSpec((tq, IDX_HEADS * IDX_DIM), lambda i: (i, C_IQ // (IDX_HEADS * IDX_DIM))),
            pl.BlockSpec((tq, LANES), lambda i: (i, C_TAIL // LANES)),
            pl.BlockSpec((tq, wa), lambda i: (i, C_QA // wa)),
            pl.BlockSpec((t, LANES), lambda i: (0, 0), pipeline_mode=one),
            pl.BlockSpec((t, wkv), lambda i: (0, C_KA // wkv), pipeline_mode=one),
            pl.BlockSpec((t // tk, wkv, tk), lambda i: (0, 0, 0), pipeline_mode=one),
        ],
        out_specs=pl.BlockSpec((tq, wa), lambda i: (i, 0)),
        out_shape=jax.ShapeDtypeStruct((t, wa), bf16),
        scratch_shapes=[
            pltpu.VMEM((LANES, IDX_HEADS * tq), bf16),
            pltpu.VMEM((N_KV_A, LANES, 2 * tq), bf16),
            pltpu.VMEM((t // tk, tk, tq), f32),
            pltpu.VMEM((N_KV_A, 1, 2 * tq), f32),
            pltpu.VMEM((N_KV_A, 1, 2 * tq), f32),
            pltpu.VMEM((N_KV_A, LANES, 2 * tq), f32),
        ],
        compiler_params=_cparams(("arbitrary",)),
        name="dsa_prompt",
    )(u_f, u_f, u_f, ikd, u_b, vat)


DSA_SCORE_CHUNK_PAGES = 32
DSA_PAGE_ISSUE_UNROLL = 8


def _dsa_sample_score_kernel(pt_ref, idx_hbm, iq_ref, tail_ref, sc_ref, buf, qs, wst, sem, *, tn, n_pages):
    b = pl.program_id(0)
    nb = pl.num_programs(0)
    cpp = DSA_SCORE_CHUNK_PAGES
    n_chunks = n_pages // cpp
    kw = cpp * PAGE_SIZE
    wscale = IDX_DIM ** -0.5 * IDX_HEADS ** -0.5

    def issue(seq, slot):
        def body(blk, c):
            for u in range(DSA_PAGE_ISSUE_UNROLL):
                p = blk * DSA_PAGE_ISSUE_UNROLL + u
                pltpu.make_async_copy(idx_hbm.at[pt_ref[seq, p]], buf.at[slot, p], sem.at[slot]).start()
            return c

        lax.fori_loop(0, n_pages // DSA_PAGE_ISSUE_UNROLL, body, 0)

    def wait(slot):
        pltpu.make_async_copy(idx_hbm.at[pl.ds(0, n_pages)], buf.at[slot], sem.at[slot]).wait()

    @pl.when(b == 0)
    def _():
        issue(0, 0)

    iq = iq_ref[...]
    w = tail_ref[...]
    for h in range(IDX_HEADS):
        qs[h * tn:(h + 1) * tn, :] = iq[:, h * IDX_DIM:(h + 1) * IDX_DIM]
        wst[h * tn:(h + 1) * tn, :] = jnp.broadcast_to(w[:, IDX_DIM + h:IDX_DIM + h + 1] * wscale, (tn, LANES))
    qsb = qs[...].astype(bf16)

    def head_sum(logits):
        width = logits.shape[1]
        acc = jnp.zeros((tn, width), f32)
        for h in range(IDX_HEADS):
            wh = wst[h * tn:(h + 1) * tn, :]
            acc = acc + jnp.maximum(logits[h * tn:(h + 1) * tn, :], 0.0) * jnp.concatenate([wh] * (width // LANES), axis=1)
        return acc

    def past_scores(slot):
        @pl.when(b + 1 < nb)
        def _():
            issue(b + 1, 1 - slot)

        wait(slot)
        for c in range(n_chunks):
            keys_t = jnp.concatenate([buf[slot, c * cpp + p] for p in range(cpp)], axis=1).astype(bf16)
            sc_ref[0, c] = head_sum(jnp.dot(qsb, keys_t, preferred_element_type=f32))

    @pl.when(b % 2 == 0)
    def _():
        past_scores(0)

    @pl.when(b % 2 == 1)
    def _():
        past_scores(1)

    ik_new = tail_ref[:, 0:IDX_DIM].astype(bf16)
    kpad = jnp.concatenate([ik_new, jnp.zeros((LANES - tn, IDX_DIM), bf16)], axis=0)
    s_new = head_sum(lax.dot_general(qsb, kpad, NT_DIMS, preferred_element_type=f32))
    qi = lax.broadcasted_iota(i32, (tn, LANES), 0)
    kj = lax.broadcasted_iota(i32, (tn, LANES), 1)
    snew = jnp.where(kj <= qi, s_new, -jnp.inf)
    sc_ref[0, n_chunks] = jnp.concatenate([snew, jnp.full((tn, kw - LANES), -jnp.inf, f32)], axis=1)


def _dsa_sample_thr_kernel(sc_ref, thr_ref, *, ns, nc, tn, kw, topk):
    def count_ge(cand):
        out = []
        for s in range(ns):
            cb = jnp.broadcast_to(cand[s], (tn, LANES))
            parts = [jnp.zeros((tn, LANES), f32)] * 4
            for c in range(nc):
                blk = sc_ref[s, c]
                for q in range(kw // LANES):
                    parts[q % 4] = parts[q % 4] + jnp.where(blk[:, q * LANES:(q + 1) * LANES] >= cb, 1.0, 0.0)
            out.append(jnp.sum((parts[0] + parts[1]) + (parts[2] + parts[3]), axis=1, keepdims=True))
        return jnp.stack(out)

    thr = _kth_threshold(count_ge, (ns, tn, 1), float(topk))
    thr_ref[...] = jnp.broadcast_to(thr, (ns, tn, LANES))


def _dsa_sample_attn_kernel(pt_ref, *refs, tn, ppg, n_groups):
    kpages = refs[:ppg]
    vpages = refs[ppg:2 * ppg]
    qa_ref, kn_ref, vn_ref, sc_ref, scn_ref, thr_ref, o_ref, qg, m_scr, l_scr, acc_scr = refs[2 * ppg:]
    c = pl.program_id(1)

    @pl.when(c == 0)
    def _():
        qa = qa_ref[...]
        for g in range(N_KV_A):
            qg[g, 0:tn, :] = qa[:, (2 * g) * LANES:(2 * g + 1) * LANES]
            qg[g, tn:2 * tn, :] = qa[:, (2 * g + 1) * LANES:(2 * g + 2) * LANES]
        m_scr[...] = jnp.full(m_scr.shape, NEG_BIG, f32)
        l_scr[...] = jnp.zeros(l_scr.shape, f32)
        acc_scr[...] = jnp.zeros(acc_scr.shape, f32)

    thr = thr_ref[0][:, 0:1]
    thr2 = jnp.concatenate([thr, thr], axis=0)

    def attend(ks, vs, sc):
        bias = jnp.where(jnp.concatenate([sc, sc], axis=0) >= thr2, 0.0, NEG_BIG)
        _flash_update([qg[g].astype(bf16) for g in range(N_KV_A)], ks, vs, bias, m_scr, l_scr, acc_scr, batched=True)

    group = lambda pages, g: jnp.concatenate(
        [p[pl.ds(g, PAGE_SIZE, stride=N_KV_A), :] for p in pages], axis=0).astype(bf16)
    attend([group(kpages, g) for g in range(N_KV_A)], [group(vpages, g) for g in range(N_KV_A)], sc_ref[0, 0])

    @pl.when(c == n_groups - 1)
    def _():
        zpad = jnp.zeros((LANES - tn, N_KV_A * HEAD_DIM_A), bf16)
        attend(_lane_blocks(jnp.concatenate([kn_ref[...].astype(bf16), zpad], axis=0)),
               _lane_blocks(jnp.concatenate([vn_ref[...].astype(bf16), zpad], axis=0)),
               scn_ref[0, 0][:, 0:LANES])
        for g in range(N_KV_A):
            o = acc_scr[g] / l_scr[g]
            o_ref[:, (2 * g) * LANES:(2 * g + 1) * LANES] = o[0:tn]
            o_ref[:, (2 * g + 1) * LANES:(2 * g + 2) * LANES] = o[tn:2 * tn]


def _dsa_sample(u_f, row0, bd, tn, cache_k, cache_v, idx_kt, page_table):
    n_pages = page_table.shape[1]
    past = n_pages * PAGE_SIZE
    topk = min(TOPK_MAX, (past + tn) // 4)
    ppa = _pick(n_pages, (16, 8, 4, 2, 1))
    pps = _pick(n_pages, (DSA_SCORE_CHUNK_PAGES,))
    nga, ngs = n_pages // ppa, n_pages // pps
    kwa, kws = ppa * PAGE_SIZE, pps * PAGE_SIZE
    per = kws // kwa
    wkv = N_KV_A * HEAD_DIM_A
    wa = N_HEADS_A * HEAD_DIM_A
    rb0 = row0 // tn

    def kv_spec(j):
        return pl.BlockSpec((None, PAGE_SIZE * N_KV_A, HEAD_DIM_A), lambda b, c, pt, j=j: (pt[b, c * ppa + j], 0, 0))

    score = pl.pallas_call(
        functools.partial(_dsa_sample_score_kernel, tn=tn, n_pages=n_pages),
        grid_spec=pltpu.PrefetchScalarGridSpec(
            num_scalar_prefetch=1,
            grid=(bd,),
            in_specs=[
                pl.BlockSpec(memory_space=pl.ANY),
                pl.BlockSpec((tn, IDX_HEADS * IDX_DIM), lambda b, pt: (rb0 + b, C_IQ // (IDX_HEADS * IDX_DIM))),
                pl.BlockSpec((tn, LANES), lambda b, pt: (rb0 + b, C_TAIL // LANES)),
            ],
            out_specs=pl.BlockSpec((1, ngs + 1, tn, kws), lambda b, pt: (b, 0, 0, 0)),
            scratch_shapes=[
                pltpu.VMEM((2, n_pages, IDX_DIM, PAGE_SIZE), f32),
                pltpu.VMEM((IDX_HEADS * tn, IDX_DIM), f32),
                pltpu.VMEM((IDX_HEADS * tn, LANES), f32),
                pltpu.SemaphoreType.DMA((2,)),
            ],
        ),
        out_shape=jax.ShapeDtypeStruct((bd, ngs + 1, tn, kws), f32),
        compiler_params=_cparams(("arbitrary",)),
        name="dsa_sample_score",
    )
    scores = score(page_table, idx_kt, u_f, u_f)

    nst = _pick(bd, (8, 4, 2, 1))
    thr = pl.pallas_call(
        functools.partial(_dsa_sample_thr_kernel, ns=nst, nc=ngs + 1, tn=tn, kw=kws, topk=topk),
        grid=(bd // nst,),
        in_specs=[pl.BlockSpec((nst, ngs + 1, tn, kws), lambda i: (i, 0, 0, 0))],
        out_specs=pl.BlockSpec((nst, tn, LANES), lambda i: (i, 0, 0)),
        out_shape=jax.ShapeDtypeStruct((bd, tn, LANES), f32),
        compiler_params=_cparams(("arbitrary",)),
        name="dsa_sample_thr",
    )(scores)

    attn = pl.pallas_call(
        functools.partial(_dsa_sample_attn_kernel, tn=tn, ppg=ppa, n_groups=nga),
        grid_spec=pltpu.PrefetchScalarGridSpec(
            num_scalar_prefetch=1,
            grid=(bd, nga),
            in_specs=[kv_spec(j) for j in range(ppa)] + [kv_spec(j) for j in range(ppa)] + [
                pl.BlockSpec((tn, wa), lambda b, c, pt: (rb0 + b, C_QA // wa)),
                pl.BlockSpec((tn, wkv), lambda b, c, pt: (rb0 + b, C_KA // wkv)),
                pl.BlockSpec((tn, wkv), lambda b, c, pt: (rb0 + b, C_VA // wkv)),
                pl.BlockSpec((1, 1, tn, kwa), lambda b, c, pt: (b, c // per, 0, c % per)),
                pl.BlockSpec((1, 1, tn, kwa), lambda b, c, pt: (b, ngs, 0, 0)),
                pl.BlockSpec((1, tn, LANES), lambda b, c, pt: (b, 0, 0)),
            ],
            out_specs=pl.BlockSpec((tn, wa), lambda b, c, pt: (b, 0)),
            scratch_shapes=[
                pltpu.VMEM((N_KV_A, 2 * tn, LANES), f32),
                pltpu.VMEM((N_KV_A, 2 * tn, 1), f32),
                pltpu.VMEM((N_KV_A, 2 * tn, 1), f32),
                pltpu.VMEM((N_KV_A, 2 * tn, LANES), f32),
            ],
        ),
        out_shape=jax.ShapeDtypeStruct((bd * tn, wa), f32),
        compiler_params=_cparams(("arbitrary", "arbitrary")),
        name="dsa_sample_attn",
    )
    return attn(page_table, *([cache_k] * ppa), *([cache_v] * ppa), u_f, u_f, u_f, scores, scores, thr)


def _ret_kernel(q_ref, k_ref, v_ref, g_ref, gnw_ref, dmask_ref, qdec_ref, kdec_ref, sdec_ref, s0_ref,
                o_ref, sout_ref, s_scr, *, n_chunks):
    c = pl.program_id(1)

    @pl.when(c == 0)
    def _():
        s_scr[...] = s0_ref[0]

    for h in range(N_HEADS_B):
        q = q_ref[:, h * QK_DIM_B:(h + 1) * QK_DIM_B]
        k = k_ref[:, h * QK_DIM_B:(h + 1) * QK_DIM_B]
        v = v_ref[:, h * V_DIM_B:(h + 1) * V_DIM_B].astype(bf16)
        gate = g_ref[:, h * V_DIM_B:(h + 1) * V_DIM_B]
        qb = q.astype(bf16)
        att = lax.dot_general(qb, k.astype(bf16), NT_DIMS, preferred_element_type=f32) * dmask_ref[h]
        s_old = s_scr[h]
        o = (jnp.dot(att.astype(bf16), v, preferred_element_type=f32)
             + jnp.dot(qb, s_old.astype(bf16), preferred_element_type=f32) * qdec_ref[h])
        kd = (k * kdec_ref[h]).astype(bf16)
        s_scr[h] = s_old * sdec_ref[h] + lax.dot_general(kd, v, TN_DIMS, preferred_element_type=f32)
        mu = jnp.mean(o, axis=-1, keepdims=True)
        var = jnp.mean(jnp.square(o - mu), axis=-1, keepdims=True)
        rb = (o - mu) * lax.rsqrt(var + LN_EPS) * gnw_ref[:, h * V_DIM_B:(h + 1) * V_DIM_B]
        rb = rb * (gate / (1.0 + jnp.exp(-gate)))
        o_ref[:, h * V_DIM_B:(h + 1) * V_DIM_B] = rb.astype(o_ref.dtype)

    @pl.when(c == n_chunks - 1)
    def _():
        sout_ref[0] = s_scr[...]


def _retention(u_f, row0, nb, t, state0, gn_w, out_dtype):
    ch = min(RET_CHUNK, t)
    if t % ch:
        ch = t
    n = t // ch
    hb = N_HEADS_B
    lg = jnp.log1p(-jnp.exp2(-5.0 - jnp.arange(hb, dtype=f32)))
    i = jnp.arange(ch)
    diff = i[:, None] - i[None, :]
    dmask = jnp.where(diff >= 0, jnp.exp(lg[:, None, None] * jnp.maximum(diff, 0)), 0.0)
    qdec = jnp.broadcast_to(jnp.exp(lg[:, None] * (i + 1))[:, :, None], (hb, ch, V_DIM_B))
    kdec = jnp.broadcast_to(jnp.exp(lg[:, None] * (ch - 1 - i))[:, :, None], (hb, ch, QK_DIM_B))
    sdec = jnp.broadcast_to(jnp.exp(lg * ch)[:, None, None], (hb, 1, V_DIM_B))
    wqk = hb * QK_DIM_B
    wv = hb * V_DIM_B
    rb0 = row0 // ch
    full3 = lambda shp: pl.BlockSpec(shp, lambda b, c: (0, 0, 0))
    return pl.pallas_call(
        functools.partial(_ret_kernel, n_chunks=n),
        grid=(nb, n),
        in_specs=[
            pl.BlockSpec((ch, wqk), lambda b, c: (rb0 + b * n + c, C_QB // wqk)),
            pl.BlockSpec((ch, wqk), lambda b, c: (rb0 + b * n + c, C_KB // wqk)),
            pl.BlockSpec((ch, wv), lambda b, c: (rb0 + b * n + c, C_VB // wv)),
            pl.BlockSpec((ch, wv), lambda b, c: (rb0 + b * n + c, C_GB // wv)),
            pl.BlockSpec((1, wv), lambda b, c: (0, 0)),
            full3((hb, ch, ch)), full3((hb, ch, V_DIM_B)), full3((hb, ch, QK_DIM_B)), full3((hb, 1, V_DIM_B)),
            pl.BlockSpec((1, hb, QK_DIM_B, V_DIM_B), lambda b, c: (b, 0, 0, 0)),
        ],
        out_specs=[
            pl.BlockSpec((ch, wv), lambda b, c: (b * n + c, 0)),
            pl.BlockSpec((1, hb, QK_DIM_B, V_DIM_B), lambda b, c: (b, 0, 0, 0)),
        ],
        out_shape=[jax.ShapeDtypeStruct((nb * t, wv), out_dtype),
                   jax.ShapeDtypeStruct((nb, hb, QK_DIM_B, V_DIM_B), f32)],
        scratch_shapes=[pltpu.VMEM((hb, QK_DIM_B, V_DIM_B), f32)],
        compiler_params=_cparams(("arbitrary", "arbitrary")),
        name="retention",
    )(u_f, u_f, u_f, u_f, gn_w.reshape(1, wv), dmask, qdec, kdec, sdec, state0)


def _split_hi_lo(a):
    hi = a.astype(bf16)
    return hi, (a - hi.astype(f32)).astype(bf16)


def _tail1_kernel(attn_ref, rb_ref, x_ref, wo_ref, g1_ref, b1_ref, wrh_ref, wrl_ref, br_ref,
                  h_ref, hq_ref, eid_ref, gate_ref, *, tm, n_exp, alpha, wa):
    mix = (jnp.dot(attn_ref[...], wo_ref[0:wa, :], preferred_element_type=f32)
           + jnp.dot(rb_ref[...], wo_ref[wa:, :], preferred_element_type=f32))
    z = alpha * x_ref[...] + mix
    mu = jnp.mean(z, axis=-1, keepdims=True)
    var = jnp.mean(jnp.square(z - mu), axis=-1, keepdims=True)
    h = (z - mu) * lax.rsqrt(var + LN_EPS) * g1_ref[...] + b1_ref[...]
    h_ref[...] = h
    nq = h.shape[1] // LANES
    for j in range(nq):
        hq_ref[pl.ds(j, tm, stride=nq), :] = h[:, j * LANES:(j + 1) * LANES]

    hh, hl = _split_hi_lo(h)
    logits = (jnp.dot(hh, wrh_ref[...], preferred_element_type=f32)
              + jnp.dot(hl, wrh_ref[...], preferred_element_type=f32)
              + jnp.dot(hh, wrl_ref[...], preferred_element_type=f32)) + br_ref[...]
    lane = lax.broadcasted_iota(i32, (tm, LANES), 1)
    lanef = lane.astype(f32)
    logits = jnp.where(lane < n_exp, logits, -jnp.inf)
    vals, ids = [], []
    for _ in range(TOP_K):
        m = jnp.max(logits, axis=1, keepdims=True)
        idx = jnp.min(jnp.where(logits == m, lanef, float(LANES)), axis=1, keepdims=True)
        vals.append(m)
        ids.append(idx)
        logits = jnp.where(lanef == idx, -jnp.inf, logits)
    es = [jnp.exp(v - vals[0]) for v in vals]
    den = es[0] + es[1] + es[2] + es[3]
    eid_ref[...] = jnp.concatenate(ids, axis=1).astype(i32)
    gate_ref[...] = jnp.concatenate([e / den for e in es], axis=1)


def _tail1(attn_b, rb_b, x_all, w_o, ln1_g, ln1_b, w_router, b_router, alpha):
    n, d = x_all.shape
    wa = attn_b.shape[1]
    n_exp = w_router.shape[1]
    tm = _pick(n, (256, 128, 64, 32, 16, 8))
    nq = d // LANES
    wr = jnp.zeros((d, LANES), f32).at[:, :n_exp].set(w_router)
    wrh, wrl = _split_hi_lo(wr)
    br = jnp.zeros((1, LANES), f32).at[0, :n_exp].set(b_router)
    row = lambda w: pl.BlockSpec((1, w), lambda i: (0, 0))
    return pl.pallas_call(
        functools.partial(_tail1_kernel, tm=tm, n_exp=n_exp, alpha=alpha, wa=wa),
        grid=(n // tm,),
        in_specs=[
            pl.BlockSpec((tm, wa), lambda i: (i, 0)),
            pl.BlockSpec((tm, rb_b.shape[1]), lambda i: (i, 0)),
            pl.BlockSpec((tm, d), lambda i: (i, 0)),
            pl.BlockSpec(w_o.shape, lambda i: (0, 0), pipeline_mode=pl.Buffered(1)),
            row(d), row(d),
            pl.BlockSpec((d, LANES), lambda i: (0, 0)), pl.BlockSpec((d, LANES), lambda i: (0, 0)), row(LANES),
        ],
        out_specs=[
            pl.BlockSpec((tm, d), lambda i: (i, 0)),
            pl.BlockSpec((tm * nq, LANES), lambda i: (i, 0)),
            pl.BlockSpec((tm, TOP_K), lambda i: (i, 0)),
            pl.BlockSpec((tm, TOP_K), lambda i: (i, 0)),
        ],
        out_shape=[jax.ShapeDtypeStruct((n, d), f32), jax.ShapeDtypeStruct((n * nq, LANES), f32),
                   jax.ShapeDtypeStruct((n, TOP_K), i32), jax.ShapeDtypeStruct((n, TOP_K), f32)],
        compiler_params=_cparams(("arbitrary",)),
        name="tail1",
    )(attn_b, rb_b, x_all, w_o.astype(bf16), ln1_g.reshape(1, d), ln1_b.reshape(1, d), wrh, wrl, br)


MOE_R = 2048
MOE_SUB = 256
MOE_TF = 256
MOE_ISSUE_UNROLL = 8
MOE_TILES = (512, 256, 128)


def _moe_plan(eid, n_exp, r_cap):
    n = eid.shape[0]
    p = n * TOP_K
    flat = eid.reshape(p)
    onehot = (flat[:, None] == jnp.arange(n_exp, dtype=i32)[None, :]).astype(i32)
    csum = jnp.cumsum(onehot, axis=0)
    rank = jnp.sum((csum - onehot) * onehot, axis=1)
    counts = csum[-1]
    ngrp = (counts + r_cap - 1) // r_cap
    gend = jnp.cumsum(ngrp)
    gstart = gend - ngrp
    g_of = gstart[flat] + rank // r_cap
    slot = rank % r_cap
    n_groups = n_exp + p // r_cap
    pair = jnp.full((n_groups, r_cap), -1, i32).at[g_of, slot].set(jnp.arange(p, dtype=i32))
    t_idx, k_idx = pair // TOP_K, pair % TOP_K
    spare = jnp.broadcast_to(p + jnp.arange(r_cap, dtype=i32) % MOE_SUB, (n_groups, r_cap))
    tok = jnp.where(pair >= 0, t_idx, 0)
    dst = jnp.where(pair >= 0, k_idx * n + t_idx, spare)
    gid = jnp.arange(n_groups, dtype=i32)
    total = gend[-1]
    gclamp = jnp.minimum(gid, total - 1)
    g_exp = jnp.sum((gend[None, :] <= gclamp[:, None]).astype(i32), axis=1)
    g_rows = jnp.clip(counts[g_exp] - (gclamp - gstart[g_exp]) * r_cap, 0, r_cap)
    g_rows = jnp.where(gid < total, g_rows, 0).astype(i32)
    return g_exp, g_rows, tok.reshape(n_groups, 1, r_cap), dst.reshape(n_groups, 1, r_cap)


def _moe_kernel(ge_ref, gr_ref, tok_ref, dst_ref, hq_ref, wgu_ref, wd_ref, bgu_ref, bd_ref, pm_ref, y_ref,
                qbuf, xb, acc, wgu_b, wd_b, sem_in, sem_out, *, nq, nj, spare_row0):
    g = pl.program_id(0)
    j = pl.program_id(1)
    rows = gr_ref[g]
    nsub = (rows + MOE_SUB - 1) // MOE_SUB
    active = rows > 0

    def in_copy(s, i, slot):
        t = tok_ref[0, 0, s * MOE_SUB + i]
        return pltpu.make_async_copy(hq_ref.at[pl.ds(pl.multiple_of(t * nq, nq), nq), :],
                                     qbuf.at[slot, pl.ds(pl.multiple_of(i * nq, nq), nq), :], sem_in.at[slot])

    def out_copy(s, i, slot):
        d = dst_ref[0, 0, s * MOE_SUB + i]
        return pltpu.make_async_copy(qbuf.at[slot, pl.ds(pl.multiple_of(i * nq, nq), nq), :],
                                     y_ref.at[pl.ds(pl.multiple_of(d * nq, nq), nq), :], sem_out.at[slot])

    def for_rows(fn):
        def body(b, c):
            for u in range(MOE_ISSUE_UNROLL):
                fn(b * MOE_ISSUE_UNROLL + u)
            return c

        lax.fori_loop(0, MOE_SUB // MOE_ISSUE_UNROLL, body, 0)

    def wait_in(slot):
        pltpu.make_async_copy(hq_ref.at[pl.ds(0, MOE_SUB * nq), :], qbuf.at[slot], sem_in.at[slot]).wait()

    def wait_out(slot):
        pltpu.make_async_copy(qbuf.at[slot], y_ref.at[pl.ds(0, MOE_SUB * nq), :], sem_out.at[slot]).wait()

    def convert(s, slot):
        r0 = pl.multiple_of(s * MOE_SUB, MOE_SUB)
        for jj in range(nq):
            xb[pl.ds(r0, MOE_SUB), jj * LANES:(jj + 1) * LANES] = (
                qbuf[slot, pl.ds(jj, MOE_SUB, stride=nq), :].astype(bf16))
        acc[pl.ds(r0, MOE_SUB), :] = jnp.zeros((MOE_SUB, acc.shape[1]), f32)

    def stage(s, slot):
        r0 = pl.multiple_of(s * MOE_SUB, MOE_SUB)
        a = acc[pl.ds(r0, MOE_SUB), :] + bd_ref[0]
        for jj in range(nq):
            qbuf[slot, pl.ds(jj, MOE_SUB, stride=nq), :] = a[:, jj * LANES:(jj + 1) * LANES]

    npairs = (nsub + 1) // 2

    @pl.when((g == 0) & (j == 0))
    def _():
        qbuf[1] = jnp.zeros(qbuf.shape[1:], f32)
        fill = pltpu.make_async_copy(qbuf.at[1], y_ref.at[pl.ds(spare_row0, MOE_SUB * nq), :], sem_out.at[1])
        fill.start()
        fill.wait()

    def tile(r0, size):
        gu = jnp.dot(xb[pl.ds(r0, size), :], wgu_b[...], preferred_element_type=f32) + bgu_ref[0]
        gub = gu.astype(bf16)
        gates, ups = [], []
        for q in range(2 * MOE_TF // 256):
            de = jnp.dot(gub[:, q * 256:(q + 1) * 256], pm_ref[...], preferred_element_type=f32)
            gates.append(de[:, 0:LANES])
            ups.append(de[:, LANES:2 * LANES])
        gate = jnp.minimum(jnp.concatenate(gates, axis=1), SWIGLU_LIMIT)
        up = jnp.clip(jnp.concatenate(ups, axis=1), -SWIGLU_LIMIT, SWIGLU_LIMIT)
        act = (up + 1.0) * gate * (1.0 / (1.0 + jnp.exp(-SWIGLU_ALPHA * gate)))
        acc[pl.ds(r0, size), :] += jnp.dot(act.astype(bf16), wd_b[...], preferred_element_type=f32)

    sub_tile = lambda s: tile(pl.multiple_of(s * MOE_SUB, MOE_SUB), MOE_SUB)

    @pl.when(active)
    def _():
        wgu_b[...] = wgu_ref[0].astype(bf16)
        wd_b[...] = wd_ref[0].astype(bf16)

    @pl.when(active & (j == 0))
    def _():
        for_rows(lambda i: in_copy(0, i, 0).start())

        @pl.when(nsub > 1)
        def _():
            for_rows(lambda i: in_copy(1, i, 1).start())

        def pair(pp, c):
            for slot in (0, 1):
                s = 2 * pp + slot

                @pl.when(s < nsub)
                def _(s=s, slot=slot):
                    wait_in(slot)
                    convert(s, slot)

                    @pl.when(s + 2 < nsub)
                    def _():
                        for_rows(lambda i: in_copy(s + 2, i, slot).start())

                    sub_tile(s)

            return c

        lax.fori_loop(0, npairs, pair, 0)

    @pl.when(active & (j > 0) & (j < nj - 1))
    def _():
        big, small = MOE_TILES[0], MOE_TILES[-1]
        padded = (rows + small - 1) // small * small
        nbig = padded // big

        def big_tile(s, c):
            tile(pl.multiple_of(s * big, big), big)
            return c

        lax.fori_loop(0, nbig, big_tile, 0)
        done = nbig * big
        for size in MOE_TILES[1:]:
            take = (padded - done) >= size

            @pl.when(take)
            def _(done=done, size=size):
                tile(pl.multiple_of(done, small), size)

            done = done + jnp.where(take, size, 0)

    @pl.when(active & (j == nj - 1))
    def _():
        def pair(pp, c):
            for slot in (0, 1):
                s = 2 * pp + slot

                @pl.when(s < nsub)
                def _(s=s, slot=slot):
                    sub_tile(s)

                    @pl.when(pp > 0)
                    def _():
                        wait_out(slot)

                    stage(s, slot)
                    for_rows(lambda i: out_copy(s, i, slot).start())

            return c

        lax.fori_loop(0, npairs, pair, 0)
        wait_out(0)

        @pl.when(nsub >= 2)
        def _():
            wait_out(1)


def _deinterleave_matrix():
    pm = np.zeros((256, 256), np.float32)
    i = np.arange(LANES)
    pm[2 * i, i] = 1.0
    pm[2 * i + 1, LANES + i] = 1.0
    return jnp.asarray(pm, bf16)


def _moe(hq, eid, w_gate_up, b_gate_up, w_down, b_down, n):
    n_exp, d, f2 = w_gate_up.shape
    dff = f2 // 2
    nq = d // LANES
    nj = dff // MOE_TF
    assert nj >= 2, "the first and last hidden chunks carry the row gather and scatter"
    g_exp, g_rows, tok, dst = _moe_plan(eid, n_exp, MOE_R)
    n_groups = g_exp.shape[0]

    def jeff(g, j, gr):
        return jnp.where(gr[g] > 0, j, nj - 1)

    return pl.pallas_call(
        functools.partial(_moe_kernel, nq=nq, nj=nj, spare_row0=TOP_K * n * nq),
        grid_spec=pltpu.PrefetchScalarGridSpec(
            num_scalar_prefetch=2,
            grid=(n_groups, nj),
            in_specs=[
                pl.BlockSpec((1, 1, MOE_R), lambda g, j, ge, gr: (g, 0, 0), memory_space=pltpu.SMEM),
                pl.BlockSpec((1, 1, MOE_R), lambda g, j, ge, gr: (g, 0, 0), memory_space=pltpu.SMEM),
                pl.BlockSpec(memory_space=pl.ANY),
                pl.BlockSpec((1, d, 2 * MOE_TF), lambda g, j, ge, gr: (ge[g], 0, jeff(g, j, gr))),
                pl.BlockSpec((1, MOE_TF, d), lambda g, j, ge, gr: (ge[g], jeff(g, j, gr), 0)),
                pl.BlockSpec((1, 1, 2 * MOE_TF), lambda g, j, ge, gr: (ge[g], 0, jeff(g, j, gr))),
                pl.BlockSpec((1, 1, d), lambda g, j, ge, gr: (ge[g], 0, 0)),
                pl.BlockSpec((256, 256), lambda g, j, ge, gr: (0, 0)),
            ],
            out_specs=pl.BlockSpec(memory_space=pl.ANY),
            scratch_shapes=[
                pltpu.VMEM((2, MOE_SUB * nq, LANES), f32),
                pltpu.VMEM((MOE_R, d), bf16),
                pltpu.VMEM((MOE_R, d), f32),
                pltpu.VMEM((d, 2 * MOE_TF), bf16),
                pltpu.VMEM((MOE_TF, d), bf16),
                pltpu.SemaphoreType.DMA((2,)),
                pltpu.SemaphoreType.DMA((2,)),
            ],
        ),
        out_shape=jax.ShapeDtypeStruct(((TOP_K * n + MOE_SUB) * nq, LANES), f32),
        compiler_params=_cparams(("arbitrary", "arbitrary")),
        name="moe",
    )(g_exp, g_rows, tok, dst, hq, w_gate_up, w_down, b_gate_up.reshape(n_exp, 1, f2),
      b_down.reshape(n_exp, 1, d), _deinterleave_matrix())


def _final_kernel(h_ref, y0_ref, y1_ref, y2_ref, y3_ref, gate_ref, g2_ref, b2_ref, op_ref, os_ref, *, tm, nq, alpha, nbp):
    gates = gate_ref[...]
    f = jnp.zeros(h_ref.shape, f32)
    for k, y_ref in enumerate((y0_ref, y1_ref, y2_ref, y3_ref)):
        yk = jnp.concatenate([y_ref[pl.ds(jj, tm, stride=nq), :] for jj in range(nq)], axis=1)
        f = f + gates[:, k:k + 1] * yk
    z = alpha * h_ref[...] + f
    mu = jnp.mean(z, axis=-1, keepdims=True)
    var = jnp.mean(jnp.square(z - mu), axis=-1, keepdims=True)
    y = (z - mu) * lax.rsqrt(var + LN_EPS) * g2_ref[...] + b2_ref[...]
    i = pl.program_id(0)

    @pl.when(i < nbp)
    def _():
        op_ref[...] = y

    @pl.when(i >= nbp)
    def _():
        os_ref[...] = y


def _final(h, y4q, gates, ln2_g, ln2_b, alpha, n_prompt):
    n, d = h.shape
    nq = d // LANES
    tm = _pick(np.gcd(n_prompt, n - n_prompt), (256, 128, 64, 32, 16, 8))
    nb, nbp = n // tm, n_prompt // tm
    row = pl.BlockSpec((1, d), lambda i: (0, 0))
    yspec = lambda k: pl.BlockSpec((tm * nq, LANES), lambda i, k=k: (k * nb + i, 0))
    return pl.pallas_call(
        functools.partial(_final_kernel, tm=tm, nq=nq, alpha=alpha, nbp=nbp),
        grid=(nb,),
        in_specs=[pl.BlockSpec((tm, d), lambda i: (i, 0)), yspec(0), yspec(1), yspec(2), yspec(3),
                  pl.BlockSpec((tm, TOP_K), lambda i: (i, 0)), row, row],
        out_specs=[pl.BlockSpec((tm, d), lambda i: (jnp.minimum(i, nbp - 1), 0)),
                   pl.BlockSpec((tm, d), lambda i: (jnp.maximum(i - nbp, 0), 0))],
        out_shape=[jax.ShapeDtypeStruct((n_prompt, d), f32), jax.ShapeDtypeStruct((n - n_prompt, d), f32)],
        compiler_params=_cparams(("arbitrary",)),
        name="final",
    )(h, y4q, y4q, y4q, y4q, gates, ln2_g.reshape(1, d), ln2_b.reshape(1, d))


def kernel(x_prompt, x_sample, cache_k, cache_v, cache_idx_k, state_ret, page_table, w_in, w_o, ret_gn_w,
           ln1_g, ln1_b, w_router, b_router, w_gate_up, b_gate_up, w_down, b_down, ln2_g, ln2_b):
    depth = w_in.shape[0]
    assert depth == 1, "single-layer step"
    bp, t, d = x_prompt.shape
    bd, tn, _ = x_sample.shape
    assert bp == 1
    past = page_table.shape[1] * PAGE_SIZE
    np_, ns = bp * t, bd * tn
    n = np_ + ns
    alpha = (2 * depth) ** 0.25
    wkv = N_KV_A * HEAD_DIM_A
    layer = lambda a: a.reshape(a.shape[1:])

    x_all = jnp.concatenate([x_prompt.reshape(np_, d), x_sample.reshape(ns, d)], axis=0)
    pos_all = jnp.concatenate([jnp.arange(t), jnp.tile(past + jnp.arange(tn), bd)])
    u_f, u_b = _project(x_all, pos_all, layer(w_in))

    attn_p = _dsa_prompt(u_f, u_b, t)
    pages = lambda a: a.reshape(a.shape[1], PAGE_SIZE * N_KV_A, HEAD_DIM_A)
    attn_s = _dsa_sample(u_f, np_, bd, tn, pages(cache_k), pages(cache_v), jnp.swapaxes(layer(cache_idx_k), 1, 2),
                         page_table)

    zero_state = jnp.zeros((bp, N_HEADS_B, QK_DIM_B, V_DIM_B), f32)
    rb_p, s_p = _retention(u_f, 0, bp, t, zero_state, layer(ret_gn_w), bf16)
    rb_s, s_s = _retention(u_f, np_, bd, tn, layer(state_ret), layer(ret_gn_w), f32)

    attn_all = jnp.concatenate([attn_p, attn_s.astype(bf16)], axis=0)
    rb_all = jnp.concatenate([rb_p, rb_s.astype(bf16)], axis=0)
    h, hq, eid, gates = _tail1(attn_all, rb_all, x_all, layer(w_o), layer(ln1_g), layer(ln1_b), layer(w_router),
                               layer(b_router), alpha)
    y4q = _moe(hq, eid, layer(w_gate_up), layer(b_gate_up), layer(w_down), layer(b_down), n)
    y_p, y_s = _final(h, y4q, gates, layer(ln2_g), layer(ln2_b), alpha, np_)

    kv = lambda rows, c0, lead: u_f[rows, c0:c0 + wkv].reshape(lead + (N_KV_A, HEAD_DIM_A))[None]
    ps, ss = slice(0, np_), slice(np_, n)
    return (
        y_p.reshape(bp, t, d), y_s.reshape(bd, tn, d),
        kv(ps, C_KA, (bp, t)), kv(ps, C_VA, (bp, t)), u_f[ps, C_TAIL:C_TAIL + IDX_DIM].reshape(1, bp, t, IDX_DIM),
        s_p[None],
        kv(ss, C_KA, (bd, tn)), kv(ss, C_VA, (bd, tn)), u_f[ss, C_TAIL:C_TAIL + IDX_DIM].reshape(1, bd, tn, IDX_DIM),
        s_s[None],
    )
```

```python
import functools

import numpy as np
import jax
import jax.numpy as jnp
from jax import lax
from jax.experimental import pallas as pl
from jax.experimental.pallas import tpu as pltpu

f32 = jnp.float32
bf16 = jnp.bfloat16
i32 = jnp.int32

PAGE_SIZE = 128
HEAD_DIM_A = 128
N_HEADS_A = 8
N_KV_A = 4
ROPE_DIM_A = 32
ROPE_THETA = 500000.0
IDX_HEADS = 16
IDX_DIM = 64
IDX_ROPE_DIM = 16
TOPK_MAX = 256
V_DIM_B = 128
N_HEADS_B = 8
QK_DIM_B = 64
RET_CHUNK = 128
RET_THETA = 10000.0
TOP_K = 4
SWIGLU_LIMIT = 7.0
SWIGLU_ALPHA = 1.702
LN_EPS = 1e-5

LANES = 128
VMEM_LIMIT = 56 * 1024 * 1024

PROJ_TN = 512
C_QA, C_KA, C_VA, C_IQ, C_QB, C_KB, C_VB, C_GB, C_TAIL, PROJ_W = 0, 1024, 1536, 2048, 3072, 3584, 4096, 5120, 6144, 6656
PROJ_TILE_TYPES = (6, 6, 1, 0, 2, 2, 3, 4, 0, 0, 0, 0, 5)
QA_SCALE = HEAD_DIM_A ** -0.5 * 1.4426950408889634

INT_MIN = -2 ** 31
KEY_NEG_INF = -2139095041
NEG_BIG = -1e30

NT_DIMS = (((1,), (1,)), ((), ()))
TN_DIMS = (((0,), (0,)), ((), ()))


def _pick(n, cands):
    for c in cands:
        if n % c == 0:
            return c
    raise ValueError(f"no tile for {n}")


def _cparams(sem, vmem=VMEM_LIMIT):
    return pltpu.CompilerParams(dimension_semantics=sem, vmem_limit_bytes=vmem)


def _rope_table(pos, rot_dim, theta, period, scale=1.0, active=LANES):
    half = rot_dim // 2
    inv_freq = 1.0 / (theta ** (jnp.arange(half, dtype=f32) / half))
    ang = pos.astype(f32)[:, None] * inv_freq[None, :]
    cos, sin = jnp.cos(ang), jnp.sin(ang)
    lane = np.arange(LANES)
    d = lane % period
    first = (d < half) & (lane < active)
    second = (d >= half) & (d < rot_dim) & (lane < active)
    idx = np.where(first, d, np.where(second, d - half, 0))
    cos_l, sin_l = cos[:, idx], sin[:, idx]
    c = jnp.where(first | second, cos_l, 1.0)
    s1 = jnp.where(second, sin_l, 0.0)
    s2 = jnp.where(first, -sin_l, 0.0)
    return jnp.concatenate([c, s1, s2], axis=1) * scale


def _proj_kernel(tt_ref, x_ref, w_ref, tab_ref, of_ref, ob_ref, xb_scr):
    j = pl.program_id(1)

    @pl.when(j == 0)
    def _():
        xb_scr[...] = x_ref[...].astype(bf16)

    t = tt_ref[j]

    def emit(half):
        piece = 2 * LANES
        for c0 in range(0, PROJ_TN, piece):
            u = jnp.dot(xb_scr[...], w_ref[:, c0:c0 + piece], preferred_element_type=f32)
            if half is not None:
                c = tab_ref[0, :, 0:LANES]
                s1 = tab_ref[0, :, LANES:2 * LANES]
                s2 = tab_ref[0, :, 2 * LANES:3 * LANES]
                outs = []
                for q in range(piece // LANES):
                    uc = u[:, q * LANES:(q + 1) * LANES]
                    outs.append(uc * c + pltpu.roll(uc, half, 1) * s1 + pltpu.roll(uc, LANES - half, 1) * s2)
                u = jnp.concatenate(outs, axis=1)
            of_ref[:, c0:c0 + piece] = u
            ob_ref[:, c0:c0 + piece] = u.astype(bf16)

    @pl.when(t == 0)
    def _():
        emit(None)

    @pl.when((t == 1) | (t == 6))
    def _():
        emit(ROPE_DIM_A // 2)

    @pl.when((t == 2) | (t == 5))
    def _():
        emit(IDX_ROPE_DIM // 2)

    @pl.when((t == 3) | (t == 4))
    def _():
        emit(QK_DIM_B // 2)


def _project(x_all, pos_all, w_in):
    n, d = x_all.shape
    tm = _pick(n, (768, 512, 384, 256, 128, 64, 32, 16, 8))
    o = np.cumsum((0, 1024, 512, 512, 1024, 64, 16, 512, 512, 1024, 1024))
    wp = jnp.concatenate([w_in[:, o[0]:o[4]], w_in[:, o[6]:o[10]], w_in[:, o[4]:o[6]],
                          jnp.zeros((d, PROJ_W - C_TAIL - 80), w_in.dtype)], axis=1).astype(bf16)
    tabs = jnp.stack([
        _rope_table(pos_all, ROPE_DIM_A, ROPE_THETA, HEAD_DIM_A),
        _rope_table(pos_all, ROPE_DIM_A, ROPE_THETA, HEAD_DIM_A),
        _rope_table(pos_all, IDX_ROPE_DIM, ROPE_THETA, IDX_DIM),
        _rope_table(pos_all, QK_DIM_B, RET_THETA, QK_DIM_B),
        _rope_table(pos_all, QK_DIM_B, RET_THETA, QK_DIM_B, scale=QK_DIM_B ** -0.5),
        _rope_table(pos_all, IDX_ROPE_DIM, ROPE_THETA, IDX_DIM, active=IDX_DIM),
        _rope_table(pos_all, ROPE_DIM_A, ROPE_THETA, HEAD_DIM_A, scale=QA_SCALE),
    ])
    tt = jnp.asarray(PROJ_TILE_TYPES, i32)
    nj = PROJ_W // PROJ_TN
    return pl.pallas_call(
        _proj_kernel,
        grid_spec=pltpu.PrefetchScalarGridSpec(
            num_scalar_prefetch=1,
            grid=(n // tm, nj),
            in_specs=[
                pl.BlockSpec((tm, d), lambda i, j, tt: (i, 0)),
                pl.BlockSpec((d, PROJ_TN), lambda i, j, tt: (0, j)),
                pl.BlockSpec((1, tm, 3 * LANES), lambda i, j, tt: (tt[j], i, 0)),
            ],
            out_specs=[
                pl.BlockSpec((tm, PROJ_TN), lambda i, j, tt: (i, j)),
                pl.BlockSpec((tm, PROJ_TN), lambda i, j, tt: (i, j)),
            ],
            scratch_shapes=[pltpu.VMEM((tm, d), bf16)],
        ),
        out_shape=[jax.ShapeDtypeStruct((n, PROJ_W), f32), jax.ShapeDtypeStruct((n, PROJ_W), bf16)],
        compiler_params=_cparams(("arbitrary", "arbitrary")),
        name="proj",
    )(tt, x_all, wp, tabs)


def _key_to_float(key):
    return pltpu.bitcast(key ^ ((key >> 31) & 0x7FFFFFFF), f32)


def _kth_threshold(count_ge, shape, k):
    def body(step, ans):
        cand = ans + jnp.left_shift(jnp.int32(1), 31 - step)
        return jnp.where(count_ge(_key_to_float(cand)) >= k, cand, ans)

    ans = lax.fori_loop(0, 32, body, jnp.full(shape, INT_MIN, i32))
    return _key_to_float(jnp.maximum(ans, KEY_NEG_INF + 1))


def _lane_blocks(x):
    return [x[:, i * LANES:(i + 1) * LANES] for i in range(x.shape[1] // LANES)]


def _flash_update(qs, ks, vs, bias, m_scr, l_scr, acc_scr, batched):
    n = len(qs)
    score = lambda g: lax.dot_general(qs[g], ks[g], NT_DIMS, preferred_element_type=f32) + bias

    def update(g, sm):
        m_old = m_scr[g]
        m_new = jnp.maximum(m_old, jnp.max(functools.reduce(jnp.maximum, _lane_blocks(sm)), axis=1, keepdims=True))
        alpha = jnp.exp2(m_old - m_new)
        p = jnp.exp2(sm - m_new)
        l_scr[g] = alpha * l_scr[g] + jnp.sum(functools.reduce(jnp.add, _lane_blocks(p)), axis=1, keepdims=True)
        m_scr[g] = m_new
        acc_scr[g] = alpha * acc_scr[g] + jnp.dot(p.astype(bf16), vs[g], preferred_element_type=f32)

    if batched:
        sms = [score(g) for g in range(n)]
        for g in range(n):
            update(g, sms[g])
    else:
        for g in range(n):
            update(g, score(g))


DSA_GROUP_BATCH = 4


def _dsa_prompt_kernel(iq_ref, iw_ref, qa_ref, ikd_ref, ka_ref, vat_ref, o_ref,
                       qst, qgt, sct, m_scr, l_scr, acc_scr, *, tq, tk, topk):
    i = pl.program_id(0)
    n_chunks = (i * tq + tq + tk - 1) // tk
    row = lax.broadcasted_iota(i32, (LANES, tq), 0)

    for p in range(IDX_HEADS // 2):
        blk = iq_ref[:, p * LANES:(p + 1) * LANES].T
        qst[:, (2 * p) * tq:(2 * p + 1) * tq] = jnp.where(row < IDX_DIM, blk, 0.0).astype(bf16)
        qst[:, (2 * p + 1) * tq:(2 * p + 2) * tq] = jnp.where(row >= IDX_DIM, blk, 0.0).astype(bf16)
    wt = iw_ref[...].T * (IDX_DIM ** -0.5 * IDX_HEADS ** -0.5)
    for h in range(N_HEADS_A):
        qgt[h // 2, :, (h % 2) * tq:(h % 2 + 1) * tq] = qa_ref[:, h * LANES:(h + 1) * LANES].T.astype(bf16)

    qpos = i * tq + lax.broadcasted_iota(i32, (tk, tq), 1)

    def score_body(c, carry):
        k0 = pl.multiple_of(c * tk, tk)
        logits = jnp.dot(ikd_ref[pl.ds(k0, tk), :], qst[...], preferred_element_type=f32)
        acc = jnp.zeros((tk, tq), f32)
        for h in range(IDX_HEADS):
            acc = acc + jnp.maximum(logits[:, h * tq:(h + 1) * tq], 0.0) * wt[IDX_DIM + h:IDX_DIM + h + 1, :]
        kpos = k0 + lax.broadcasted_iota(i32, (tk, tq), 0)
        sct[c] = jnp.where(kpos <= qpos, acc, -jnp.inf)
        return carry

    lax.fori_loop(0, n_chunks, score_body, 0)

    def count_ge(cand):
        cb = jnp.broadcast_to(cand, (8, tq))

        def body(c, accs):
            accs = list(accs)
            for r in range(tk // 8):
                accs[r % 4] = accs[r % 4] + jnp.where(sct[c, r * 8:(r + 1) * 8, :] >= cb, 1.0, 0.0)
            return tuple(accs)

        accs = lax.fori_loop(0, n_chunks, body, (jnp.zeros((8, tq), f32),) * 4)
        return jnp.sum((accs[0] + accs[1]) + (accs[2] + accs[3]), axis=0, keepdims=True)

    thr = _kth_threshold(count_ge, (1, tq), float(topk))

    m_scr[...] = jnp.full(m_scr.shape, NEG_BIG, f32)
    l_scr[...] = jnp.zeros(l_scr.shape, f32)
    acc_scr[...] = jnp.zeros(acc_scr.shape, f32)

    def att_body(c, carry):
        k0 = pl.multiple_of(c * tk, tk)
        bias1 = jnp.where(sct[c] >= thr, 0.0, NEG_BIG)
        bias = jnp.concatenate([bias1, bias1], axis=1)
        for g0 in range(0, N_KV_A, DSA_GROUP_BATCH):
            groups = range(g0, g0 + DSA_GROUP_BATCH)
            sms = {g: jnp.dot(ka_ref[pl.ds(k0, tk), g * LANES:(g + 1) * LANES], qgt[g],
                              preferred_element_type=f32) + bias for g in groups}
            m_new = {g: jnp.maximum(m_scr[g], jnp.max(sms[g], axis=0, keepdims=True)) for g in groups}
            alpha = {g: jnp.exp2(m_scr[g] - m_new[g]) for g in groups}
            ps = {g: jnp.exp2(sms[g] - m_new[g]) for g in groups}
            for g in groups:
                l_scr[g] = alpha[g] * l_scr[g] + jnp.sum(ps[g], axis=0, keepdims=True)
                m_scr[g] = m_new[g]
            for g in groups:
                acc_scr[g] = alpha[g] * acc_scr[g] + jnp.dot(vat_ref[c, g * LANES:(g + 1) * LANES, :],
                                                             ps[g].astype(bf16), preferred_element_type=f32)
        return carry

    lax.fori_loop(0, n_chunks, att_body, 0)

    for g in range(N_KV_A):
        o = acc_scr[g] / l_scr[g]
        o_ref[:, (2 * g) * LANES:(2 * g + 1) * LANES] = o[:, 0:tq].T.astype(o_ref.dtype)
        o_ref[:, (2 * g + 1) * LANES:(2 * g + 2) * LANES] = o[:, tq:2 * tq].T.astype(o_ref.dtype)


def _dsa_prompt(u_f, u_b, t):
    tq = _pick(t, (128,))
    tk = _pick(t, (512, 256, 128))
    topk = min(TOPK_MAX, t // 4)
    wa = N_HEADS_A * HEAD_DIM_A
    wkv = N_KV_A * HEAD_DIM_A
    ik_b = u_b[:t, C_TAIL:C_TAIL + IDX_DIM]
    ikd = jnp.concatenate([ik_b, ik_b], axis=1)
    vat = u_b[:t, C_VA:C_VA + wkv].reshape(t // tk, tk, wkv).transpose(0, 2, 1)
    kern = functools.partial(_dsa_prompt_kernel, tq=tq, tk=tk, topk=topk)
    one = pl.Buffered(1)
    return pl.pallas_call(
        kern,
        grid=(t // tq,),
        in_specs=[
            pl.BlockSpec((tq, IDX_HEADS * IDX_DIM), lambda i: (i, C_IQ // (IDX_HEADS * IDX_DIM))),
            pl.BlockSpec((tq, LANES), lambda i: (i, C_TAIL // LANES)),
            pl.BlockSpec((tq, wa), lambda i: (i, C_QA // wa)),
            pl.BlockSpec((t, LANES), lambda i: (0, 0), pipeline_mode=one),
            pl.BlockSpec((t, wkv), lambda i: (0, C_KA // wkv), pipeline_mode=one),
            pl.BlockSpec((t // tk, wkv, tk), lambda i: (0, 0, 0), pipeline_mode=one),
        ],
        out_specs=pl.BlockSpec((tq, wa), lambda i: (i, 0)),
        out_shape=jax.ShapeDtypeStruct((t, wa), bf16),
        scratch_shapes=[
            pltpu.VMEM((LANES, IDX_HEADS * tq), bf16),
            pltpu.VMEM((N_KV_A, LANES, 2 * tq), bf16),
            pltpu.VMEM((t // tk, tk, tq), f32),
            pltpu.VMEM((N_KV_A, 1, 2 * tq), f32),
            pltpu.VMEM((N_KV_A, 1, 2 * tq), f32),
            pltpu.VMEM((N_KV_A, LANES, 2 * tq), f32),
        ],
        compiler_params=_cparams(("arbitrary",)),
        name="dsa_prompt",
    )(u_f, u_f, u_f, ikd, u_b, vat)


DSA_SCORE_CHUNK_PAGES = 32
DSA_PAGE_ISSUE_UNROLL = 8


def _dsa_sample_score_kernel(pt_ref, idx_hbm, iq_ref, tail_ref, sc_ref, buf, qs, wst, sem, *, tn, n_pages):
    b = pl.program_id(0)
    nb = pl.num_programs(0)
    cpp = DSA_SCORE_CHUNK_PAGES
    n_chunks = n_pages // cpp
    kw = cpp * PAGE_SIZE
    wscale = IDX_DIM ** -0.5 * IDX_HEADS ** -0.5

    def issue(seq, slot):
        def body(blk, c):
            for u in range(DSA_PAGE_ISSUE_UNROLL):
                p = blk * DSA_PAGE_ISSUE_UNROLL + u
                pltpu.make_async_copy(idx_hbm.at[pt_ref[seq, p]], buf.at[slot, p], sem.at[slot]).start()
            return c

        lax.fori_loop(0, n_pages // DSA_PAGE_ISSUE_UNROLL, body, 0)

    def wait(slot):
        pltpu.make_async_copy(idx_hbm.at[pl.ds(0, n_pages)], buf.at[slot], sem.at[slot]).wait()

    @pl.when(b == 0)
    def _():
        issue(0, 0)

    iq = iq_ref[...]
    w = tail_ref[...]
    for h in range(IDX_HEADS):
        qs[h * tn:(h + 1) * tn, :] = iq[:, h * IDX_DIM:(h + 1) * IDX_DIM]
        wst[h * tn:(h + 1) * tn, :] = jnp.broadcast_to(w[:, IDX_DIM + h:IDX_DIM + h + 1] * wscale, (tn, LANES))
    qsb = qs[...].astype(bf16)

    def head_sum(logits):
        width = logits.shape[1]
        acc = jnp.zeros((tn, width), f32)
        for h in range(IDX_HEADS):
            wh = wst[h * tn:(h + 1) * tn, :]
            acc = acc + jnp.maximum(logits[h * tn:(h + 1) * tn, :], 0.0) * jnp.concatenate([wh] * (width // LANES), axis=1)
        return acc

    def past_scores(slot):
        @pl.when(b + 1 < nb)
        def _():
            issue(b + 1, 1 - slot)

        wait(slot)
        for c in range(n_chunks):
            keys_t = jnp.concatenate([buf[slot, c * cpp + p] for p in range(cpp)], axis=1).astype(bf16)
            sc_ref[0, c] = head_sum(jnp.dot(qsb, keys_t, preferred_element_type=f32))

    @pl.when(b % 2 == 0)
    def _():
        past_scores(0)

    @pl.when(b % 2 == 1)
    def _():
        past_scores(1)

    ik_new = tail_ref[:, 0:IDX_DIM].astype(bf16)
    kpad = jnp.concatenate([ik_new, jnp.zeros((LANES - tn, IDX_DIM), bf16)], axis=0)
    s_new = head_sum(lax.dot_general(qsb, kpad, NT_DIMS, preferred_element_type=f32))
    qi = lax.broadcasted_iota(i32, (tn, LANES), 0)
    kj = lax.broadcasted_iota(i32, (tn, LANES), 1)
    snew = jnp.where(kj <= qi, s_new, -jnp.inf)
    sc_ref[0, n_chunks] = jnp.concatenate([snew, jnp.full((tn, kw - LANES), -jnp.inf, f32)], axis=1)


def _dsa_sample_thr_kernel(sc_ref, thr_ref, *, ns, nc, tn, kw, topk):
    def count_ge(cand):
        out = []
        for s in range(ns):
            cb = jnp.broadcast_to(cand[s], (tn, LANES))
            parts = [jnp.zeros((tn, LANES), f32)] * 4
            for c in range(nc):
                blk = sc_ref[s, c]
                for q in range(kw // LANES):
                    parts[q % 4] = parts[q % 4] + jnp.where(blk[:, q * LANES:(q + 1) * LANES] >= cb, 1.0, 0.0)
            out.append(jnp.sum((parts[0] + parts[1]) + (parts[2] + parts[3]), axis=1, keepdims=True))
        return jnp.stack(out)

    thr = _kth_threshold(count_ge, (ns, tn, 1), float(topk))
    thr_ref[...] = jnp.broadcast_to(thr, (ns, tn, LANES))


def _dsa_sample_attn_kernel(pt_ref, *refs, tn, ppg, n_groups):
    kpages = refs[:ppg]
    vpages = refs[ppg:2 * ppg]
    qa_ref, kn_ref, vn_ref, sc_ref, scn_ref, thr_ref, o_ref, qg, m_scr, l_scr, acc_scr = refs[2 * ppg:]
    c = pl.program_id(1)

    @pl.when(c == 0)
    def _():
        qa = qa_ref[...]
        for g in range(N_KV_A):
            qg[g, 0:tn, :] = qa[:, (2 * g) * LANES:(2 * g + 1) * LANES]
            qg[g, tn:2 * tn, :] = qa[:, (2 * g + 1) * LANES:(2 * g + 2) * LANES]
        m_scr[...] = jnp.full(m_scr.shape, NEG_BIG, f32)
        l_scr[...] = jnp.zeros(l_scr.shape, f32)
        acc_scr[...] = jnp.zeros(acc_scr.shape, f32)

    thr = thr_ref[0][:, 0:1]
    thr2 = jnp.concatenate([thr, thr], axis=0)

    def attend(ks, vs, sc):
        bias = jnp.where(jnp.concatenate([sc, sc], axis=0) >= thr2, 0.0, NEG_BIG)
        _flash_update([qg[g].astype(bf16) for g in range(N_KV_A)], ks, vs, bias, m_scr, l_scr, acc_scr, batched=True)

    group = lambda pages, g: jnp.concatenate(
        [p[pl.ds(g, PAGE_SIZE, stride=N_KV_A), :] for p in pages], axis=0).astype(bf16)
    attend([group(kpages, g) for g in range(N_KV_A)], [group(vpages, g) for g in range(N_KV_A)], sc_ref[0, 0])

    @pl.when(c == n_groups - 1)
    def _():
        zpad = jnp.zeros((LANES - tn, N_KV_A * HEAD_DIM_A), bf16)
        attend(_lane_blocks(jnp.concatenate([kn_ref[...].astype(bf16), zpad], axis=0)),
               _lane_blocks(jnp.concatenate([vn_ref[...].astype(bf16), zpad], axis=0)),
               scn_ref[0, 0][:, 0:LANES])
        for g in range(N_KV_A):
            o = acc_scr[g] / l_scr[g]
            o_ref[:, (2 * g) * LANES:(2 * g + 1) * LANES] = o[0:tn]
            o_ref[:, (2 * g + 1) * LANES:(2 * g + 2) * LANES] = o[tn:2 * tn]


def _dsa_sample(u_f, row0, bd, tn, cache_k, cache_v, idx_kt, page_table):
    n_pages = page_table.shape[1]
    past = n_pages * PAGE_SIZE
    topk = min(TOPK_MAX, (past + tn) // 4)
    ppa = _pick(n_pages, (16, 8, 4, 2, 1))
    pps = _pick(n_pages, (DSA_SCORE_CHUNK_PAGES,))
    nga, ngs = n_pages // ppa, n_pages // pps
    kwa, kws = ppa * PAGE_SIZE, pps * PAGE_SIZE
    per = kws // kwa
    wkv = N_KV_A * HEAD_DIM_A
    wa = N_HEADS_A * HEAD_DIM_A
    rb0 = row0 // tn

    def kv_spec(j):
        return pl.BlockSpec((None, PAGE_SIZE * N_KV_A, HEAD_DIM_A), lambda b, c, pt, j=j: (pt[b, c * ppa + j], 0, 0))

    score = pl.pallas_call(
        functools.partial(_dsa_sample_score_kernel, tn=tn, n_pages=n_pages),
        grid_spec=pltpu.PrefetchScalarGridSpec(
            num_scalar_prefetch=1,
            grid=(bd,),
            in_specs=[
                pl.BlockSpec(memory_space=pl.ANY),
                pl.BlockSpec((tn, IDX_HEADS * IDX_DIM), lambda b, pt: (rb0 + b, C_IQ // (IDX_HEADS * IDX_DIM))),
                pl.BlockSpec((tn, LANES), lambda b, pt: (rb0 + b, C_TAIL // LANES)),
            ],
            out_specs=pl.BlockSpec((1, ngs + 1, tn, kws), lambda b, pt: (b, 0, 0, 0)),
            scratch_shapes=[
                pltpu.VMEM((2, n_pages, IDX_DIM, PAGE_SIZE), f32),
                pltpu.VMEM((IDX_HEADS * tn, IDX_DIM), f32),
                pltpu.VMEM((IDX_HEADS * tn, LANES), f32),
                pltpu.SemaphoreType.DMA((2,)),
            ],
        ),
        out_shape=jax.ShapeDtypeStruct((bd, ngs + 1, tn, kws), f32),
        compiler_params=_cparams(("arbitrary",)),
        name="dsa_sample_score",
    )
    scores = score(page_table, idx_kt, u_f, u_f)

    nst = _pick(bd, (8, 4, 2, 1))
    thr = pl.pallas_call(
        functools.partial(_dsa_sample_thr_kernel, ns=nst, nc=ngs + 1, tn=tn, kw=kws, topk=topk),
        grid=(bd // nst,),
        in_specs=[pl.BlockSpec((nst, ngs + 1, tn, kws), lambda i: (i, 0, 0, 0))],
        out_specs=pl.BlockSpec((nst, tn, LANES), lambda i: (i, 0, 0)),
        out_shape=jax.ShapeDtypeStruct((bd, tn, LANES), f32),
        compiler_params=_cparams(("arbitrary",)),
        name="dsa_sample_thr",
    )(scores)

    attn = pl.pallas_call(
        functools.partial(_dsa_sample_attn_kernel, tn=tn, ppg=ppa, n_groups=nga),
        grid_spec=pltpu.PrefetchScalarGridSpec(
            num_scalar_prefetch=1,
            grid=(bd, nga),
            in_specs=[kv_spec(j) for j in range(ppa)] + [kv_spec(j) for j in range(ppa)] + [
                pl.BlockSpec((tn, wa), lambda b, c, pt: (rb0 + b, C_QA // wa)),
                pl.BlockSpec((tn, wkv), lambda b, c, pt: (rb0 + b, C_KA // wkv)),
                pl.BlockSpec((tn, wkv), lambda b, c, pt: (rb0 + b, C_VA // wkv)),
                pl.BlockSpec((1, 1, tn, kwa), lambda b, c, pt: (b, c // per, 0, c % per)),
                pl.BlockSpec((1, 1, tn, kwa), lambda b, c, pt: (b, ngs, 0, 0)),
                pl.BlockSpec((1, tn, LANES), lambda b, c, pt: (b, 0, 0)),
            ],
            out_specs=pl.BlockSpec((tn, wa), lambda b, c, pt: (b, 0)),
            scratch_shapes=[
                pltpu.VMEM((N_KV_A, 2 * tn, LANES), f32),
                pltpu.VMEM((N_KV_A, 2 * tn, 1), f32),
                pltpu.VMEM((N_KV_A, 2 * tn, 1), f32),
                pltpu.VMEM((N_KV_A, 2 * tn, LANES), f32),
            ],
        ),
        out_shape=jax.ShapeDtypeStruct((bd * tn, wa), f32),
        compiler_params=_cparams(("arbitrary", "arbitrary")),
        name="dsa_sample_attn",
    )
    return attn(page_table, *([cache_k] * ppa), *([cache_v] * ppa), u_f, u_f, u_f, scores, scores, thr)


def _ret_kernel(q_ref, k_ref, v_ref, g_ref, gnw_ref, dmask_ref, qdec_ref, kdec_ref, sdec_ref, s0_ref,
                o_ref, sout_ref, s_scr, *, n_chunks):
    c = pl.program_id(1)

    @pl.when(c == 0)
    def _():
        s_scr[...] = s0_ref[0]

    for h in range(N_HEADS_B):
        q = q_ref[:, h * QK_DIM_B:(h + 1) * QK_DIM_B]
        k = k_ref[:, h * QK_DIM_B:(h + 1) * QK_DIM_B]
        v = v_ref[:, h * V_DIM_B:(h + 1) * V_DIM_B].astype(bf16)
        gate = g_ref[:, h * V_DIM_B:(h + 1) * V_DIM_B]
        qb = q.astype(bf16)
        att = lax.dot_general(qb, k.astype(bf16), NT_DIMS, preferred_element_type=f32) * dmask_ref[h]
        s_old = s_scr[h]
        o = (jnp.dot(att.astype(bf16), v, preferred_element_type=f32)
             + jnp.dot(qb, s_old.astype(bf16), preferred_element_type=f32) * qdec_ref[h])
        kd = (k * kdec_ref[h]).astype(bf16)
        s_scr[h] = s_old * sdec_ref[h] + lax.dot_general(kd, v, TN_DIMS, preferred_element_type=f32)
        mu = jnp.mean(o, axis=-1, keepdims=True)
        var = jnp.mean(jnp.square(o - mu), axis=-1, keepdims=True)
        rb = (o - mu) * lax.rsqrt(var + LN_EPS) * gnw_ref[:, h * V_DIM_B:(h + 1) * V_DIM_B]
        rb = rb * (gate / (1.0 + jnp.exp(-gate)))
        o_ref[:, h * V_DIM_B:(h + 1) * V_DIM_B] = rb.astype(o_ref.dtype)

    @pl.when(c == n_chunks - 1)
    def _():
        sout_ref[0] = s_scr[...]


def _retention(u_f, row0, nb, t, state0, gn_w, out_dtype):
    ch = min(RET_CHUNK, t)
    if t % ch:
        ch = t
    n = t // ch
    hb = N_HEADS_B
    lg = jnp.log1p(-jnp.exp2(-5.0 - jnp.arange(hb, dtype=f32)))
    i = jnp.arange(ch)
    diff = i[:, None] - i[None, :]
    dmask = jnp.where(diff >= 0, jnp.exp(lg[:, None, None] * jnp.maximum(diff, 0)), 0.0)
    qdec = jnp.broadcast_to(jnp.exp(lg[:, None] * (i + 1))[:, :, None], (hb, ch, V_DIM_B))
    kdec = jnp.broadcast_to(jnp.exp(lg[:, None] * (ch - 1 - i))[:, :, None], (hb, ch, QK_DIM_B))
    sdec = jnp.broadcast_to(jnp.exp(lg * ch)[:, None, None], (hb, 1, V_DIM_B))
    wqk = hb * QK_DIM_B
    wv = hb * V_DIM_B
    rb0 = row0 // ch
    full3 = lambda shp: pl.BlockSpec(shp, lambda b, c: (0, 0, 0))
    return pl.pallas_call(
        functools.partial(_ret_kernel, n_chunks=n),
        grid=(nb, n),
        in_specs=[
            pl.BlockSpec((ch, wqk), lambda b, c: (rb0 + b * n + c, C_QB // wqk)),
            pl.BlockSpec((ch, wqk), lambda b, c: (rb0 + b * n + c, C_KB // wqk)),
            pl.BlockSpec((ch, wv), lambda b, c: (rb0 + b * n + c, C_VB // wv)),
            pl.BlockSpec((ch, wv), lambda b, c: (rb0 + b * n + c, C_GB // wv)),
            pl.BlockSpec((1, wv), lambda b, c: (0, 0)),
            full3((hb, ch, ch)), full3((hb, ch, V_DIM_B)), full3((hb, ch, QK_DIM_B)), full3((hb, 1, V_DIM_B)),
            pl.BlockSpec((1, hb, QK_DIM_B, V_DIM_B), lambda b, c: (b, 0, 0, 0)),
        ],
        out_specs=[
            pl.BlockSpec((ch, wv), lambda b, c: (b * n + c, 0)),
            pl.BlockSpec((1, hb, QK_DIM_B, V_DIM_B), lambda b, c: (b, 0, 0, 0)),
        ],
        out_shape=[jax.ShapeDtypeStruct((nb * t, wv), out_dtype),
                   jax.ShapeDtypeStruct((nb, hb, QK_DIM_B, V_DIM_B), f32)],
        scratch_shapes=[pltpu.VMEM((hb, QK_DIM_B, V_DIM_B), f32)],
        compiler_params=_cparams(("arbitrary", "arbitrary")),
        name="retention",
    )(u_f, u_f, u_f, u_f, gn_w.reshape(1, wv), dmask, qdec, kdec, sdec, state0)


def _split_hi_lo(a):
    hi = a.astype(bf16)
    return hi, (a - hi.astype(f32)).astype(bf16)


def _tail1_kernel(attn_ref, rb_ref, x_ref, wo_ref, g1_ref, b1_ref, wrh_ref, wrl_ref, br_ref,
                  h_ref, hq_ref, eid_ref, gate_ref, *, tm, n_exp, alpha, wa):
    mix = (jnp.dot(attn_ref[...], wo_ref[0:wa, :], preferred_element_type=f32)
           + jnp.dot(rb_ref[...], wo_ref[wa:, :], preferred_element_type=f32))
    z = alpha * x_ref[...] + mix
    mu = jnp.mean(z, axis=-1, keepdims=True)
    var = jnp.mean(jnp.square(z - mu), axis=-1, keepdims=True)
    h = (z - mu) * lax.rsqrt(var + LN_EPS) * g1_ref[...] + b1_ref[...]
    h_ref[...] = h
    nq = h.shape[1] // LANES
    for j in range(nq):
        hq_ref[pl.ds(j, tm, stride=nq), :] = h[:, j * LANES:(j + 1) * LANES]

    hh, hl = _split_hi_lo(h)
    logits = (jnp.dot(hh, wrh_ref[...], preferred_element_type=f32)
              + jnp.dot(hl, wrh_ref[...], preferred_element_type=f32)
              + jnp.dot(hh, wrl_ref[...], preferred_element_type=f32)) + br_ref[...]
    lane = lax.broadcasted_iota(i32, (tm, LANES), 1)
    lanef = lane.astype(f32)
    logits = jnp.where(lane < n_exp, logits, -jnp.inf)
    vals, ids = [], []
    for _ in range(TOP_K):
        m = jnp.max(logits, axis=1, keepdims=True)
        idx = jnp.min(jnp.where(logits == m, lanef, float(LANES)), axis=1, keepdims=True)
        vals.append(m)
        ids.append(idx)
        logits = jnp.where(lanef == idx, -jnp.inf, logits)
    es = [jnp.exp(v - vals[0]) for v in vals]
    den = es[0] + es[1] + es[2] + es[3]
    eid_ref[...] = jnp.concatenate(ids, axis=1).astype(i32)
    gate_ref[...] = jnp.concatenate([e / den for e in es], axis=1)


def _tail1(attn_b, rb_b, x_all, w_o, ln1_g, ln1_b, w_router, b_router, alpha):
    n, d = x_all.shape
    wa = attn_b.shape[1]
    n_exp = w_router.shape[1]
    tm = _pick(n, (256, 128, 64, 32, 16, 8))
    nq = d // LANES
    wr = jnp.zeros((d, LANES), f32).at[:, :n_exp].set(w_router)
    wrh, wrl = _split_hi_lo(wr)
    br = jnp.zeros((1, LANES), f32).at[0, :n_exp].set(b_router)
    row = lambda w: pl.BlockSpec((1, w), lambda i: (0, 0))
    return pl.pallas_call(
        functools.partial(_tail1_kernel, tm=tm, n_exp=n_exp, alpha=alpha, wa=wa),
        grid=(n // tm,),
        in_specs=[
            pl.BlockSpec((tm, wa), lambda i: (i, 0)),
            pl.BlockSpec((tm, rb_b.shape[1]), lambda i: (i, 0)),
            pl.BlockSpec((tm, d), lambda i: (i, 0)),
            pl.BlockSpec(w_o.shape, lambda i: (0, 0), pipeline_mode=pl.Buffered(1)),
            row(d), row(d),
            pl.BlockSpec((d, LANES), lambda i: (0, 0)), pl.BlockSpec((d, LANES), lambda i: (0, 0)), row(LANES),
        ],
        out_specs=[
            pl.BlockSpec((tm, d), lambda i: (i, 0)),
            pl.BlockSpec((tm * nq, LANES), lambda i: (i, 0)),
            pl.BlockSpec((tm, TOP_K), lambda i: (i, 0)),
            pl.BlockSpec((tm, TOP_K), lambda i: (i, 0)),
        ],
        out_shape=[jax.ShapeDtypeStruct((n, d), f32), jax.ShapeDtypeStruct((n * nq, LANES), f32),
                   jax.ShapeDtypeStruct((n, TOP_K), i32), jax.ShapeDtypeStruct((n, TOP_K), f32)],
        compiler_params=_cparams(("arbitrary",)),
        name="tail1",
    )(attn_b, rb_b, x_all, w_o.astype(bf16), ln1_g.reshape(1, d), ln1_b.reshape(1, d), wrh, wrl, br)


MOE_R = 2048
MOE_SUB = 256
MOE_TF = 256
MOE_ISSUE_UNROLL = 8
MOE_TILES = (512, 256, 128)


def _moe_plan(eid, n_exp, r_cap):
    n = eid.shape[0]
    p = n * TOP_K
    flat = eid.reshape(p)
    onehot = (flat[:, None] == jnp.arange(n_exp, dtype=i32)[None, :]).astype(i32)
    csum = jnp.cumsum(onehot, axis=0)
    rank = jnp.sum((csum - onehot) * onehot, axis=1)
    counts = csum[-1]
    ngrp = (counts + r_cap - 1) // r_cap
    gend = jnp.cumsum(ngrp)
    gstart = gend - ngrp
    g_of = gstart[flat] + rank // r_cap
    slot = rank % r_cap
    n_groups = n_exp + p // r_cap
    pair = jnp.full((n_groups, r_cap), -1, i32).at[g_of, slot].set(jnp.arange(p, dtype=i32))
    t_idx, k_idx = pair // TOP_K, pair % TOP_K
    spare = jnp.broadcast_to(p + jnp.arange(r_cap, dtype=i32) % MOE_SUB, (n_groups, r_cap))
    tok = jnp.where(pair >= 0, t_idx, 0)
    dst = jnp.where(pair >= 0, k_idx * n + t_idx, spare)
    gid = jnp.arange(n_groups, dtype=i32)
    total = gend[-1]
    gclamp = jnp.minimum(gid, total - 1)
    g_exp = jnp.sum((gend[None, :] <= gclamp[:, None]).astype(i32), axis=1)
    g_rows = jnp.clip(counts[g_exp] - (gclamp - gstart[g_exp]) * r_cap, 0, r_cap)
    g_rows = jnp.where(gid < total, g_rows, 0).astype(i32)
    return g_exp, g_rows, tok.reshape(n_groups, 1, r_cap), dst.reshape(n_groups, 1, r_cap)


def _moe_kernel(ge_ref, gr_ref, tok_ref, dst_ref, hq_ref, wgu_ref, wd_ref, bgu_ref, bd_ref, pm_ref, y_ref,
                qbuf, xb, acc, wgu_b, wd_b, sem_in, sem_out, *, nq, nj, spare_row0):
    g = pl.program_id(0)
    j = pl.program_id(1)
    rows = gr_ref[g]
    nsub = (rows + MOE_SUB - 1) // MOE_SUB
    active = rows > 0

    def in_copy(s, i, slot):
        t = tok_ref[0, 0, s * MOE_SUB + i]
        return pltpu.make_async_copy(hq_ref.at[pl.ds(pl.multiple_of(t * nq, nq), nq), :],
                                     qbuf.at[slot, pl.ds(pl.multiple_of(i * nq, nq), nq), :], sem_in.at[slot])

    def out_copy(s, i, slot):
        d = dst_ref[0, 0, s * MOE_SUB + i]
        return pltpu.make_async_copy(qbuf.at[slot, pl.ds(pl.multiple_of(i * nq, nq), nq), :],
                                     y_ref.at[pl.ds(pl.multiple_of(d * nq, nq), nq), :], sem_out.at[slot])

    def for_rows(fn):
        def body(b, c):
            for u in range(MOE_ISSUE_UNROLL):
                fn(b * MOE_ISSUE_UNROLL + u)
            return c

        lax.fori_loop(0, MOE_SUB // MOE_ISSUE_UNROLL, body, 0)

    def wait_in(slot):
        pltpu.make_async_copy(hq_ref.at[pl.ds(0, MOE_SUB * nq), :], qbuf.at[slot], sem_in.at[slot]).wait()

    def wait_out(slot):
        pltpu.make_async_copy(qbuf.at[slot], y_ref.at[pl.ds(0, MOE_SUB * nq), :], sem_out.at[slot]).wait()

    def convert(s, slot):
        r0 = pl.multiple_of(s * MOE_SUB, MOE_SUB)
        for jj in range(nq):
            xb[pl.ds(r0, MOE_SUB), jj * LANES:(jj + 1) * LANES] = (
                qbuf[slot, pl.ds(jj, MOE_SUB, stride=nq), :].astype(bf16))
        acc[pl.ds(r0, MOE_SUB), :] = jnp.zeros((MOE_SUB, acc.shape[1]), f32)

    def stage(s, slot):
        r0 = pl.multiple_of(s * MOE_SUB, MOE_SUB)
        a = acc[pl.ds(r0, MOE_SUB), :] + bd_ref[0]
        for jj in range(nq):
            qbuf[slot, pl.ds(jj, MOE_SUB, stride=nq), :] = a[:, jj * LANES:(jj + 1) * LANES]

    npairs = (nsub + 1) // 2

    @pl.when((g == 0) & (j == 0))
    def _():
        qbuf[1] = jnp.zeros(qbuf.shape[1:], f32)
        fill = pltpu.make_async_copy(qbuf.at[1], y_ref.at[pl.ds(spare_row0, MOE_SUB * nq), :], sem_out.at[1])
        fill.start()
        fill.wait()

    def tile(r0, size):
        gu = jnp.dot(xb[pl.ds(r0, size), :], wgu_b[...], preferred_element_type=f32) + bgu_ref[0]
        gub = gu.astype(bf16)
        gates, ups = [], []
        for q in range(2 * MOE_TF // 256):
            de = jnp.dot(gub[:, q * 256:(q + 1) * 256], pm_ref[...], preferred_element_type=f32)
            gates.append(de[:, 0:LANES])
            ups.append(de[:, LANES:2 * LANES])
        gate = jnp.minimum(jnp.concatenate(gates, axis=1), SWIGLU_LIMIT)
        up = jnp.clip(jnp.concatenate(ups, axis=1), -SWIGLU_LIMIT, SWIGLU_LIMIT)
        act = (up + 1.0) * gate * (1.0 / (1.0 + jnp.exp(-SWIGLU_ALPHA * gate)))
        acc[pl.ds(r0, size), :] += jnp.dot(act.astype(bf16), wd_b[...], preferred_element_type=f32)

    first = j == 0
    last = j == nj - 1
    pair_rows = 2 * MOE_SUB
    assert MOE_TILES == (pair_rows, MOE_SUB, MOE_SUB // 2)
    padded = (rows + MOE_TILES[-1] - 1) // MOE_TILES[-1] * MOE_TILES[-1]

    @pl.when(active)
    def _():
        wgu_b[...] = wgu_ref[0].astype(bf16)
        wd_b[...] = wd_ref[0].astype(bf16)

    @pl.when(active & first)
    def _():
        for_rows(lambda i: in_copy(0, i, 0).start())

        @pl.when(nsub > 1)
        def _():
            for_rows(lambda i: in_copy(1, i, 1).start())

    def pair(pp, c):
        s0 = 2 * pp

        @pl.when(first)
        def _():
            for slot in (0, 1):
                s = s0 + slot

                @pl.when(s < nsub)
                def _(s=s, slot=slot):
                    wait_in(slot)
                    convert(s, slot)

                    @pl.when(s + 2 < nsub)
                    def _():
                        for_rows(lambda i: in_copy(s + 2, i, slot).start())

        r0 = pl.multiple_of(s0 * MOE_SUB, pair_rows)
        left = jnp.minimum(padded - r0, pair_rows)

        @pl.when(left == pair_rows)
        def _():
            tile(r0, pair_rows)

        @pl.when((left >= MOE_SUB) & (left < pair_rows))
        def _():
            tile(r0, MOE_SUB)

        @pl.when(left % MOE_SUB != 0)
        def _():
            tile(pl.multiple_of(r0 + left // MOE_SUB * MOE_SUB, MOE_TILES[-1]), MOE_TILES[-1])

        @pl.when(last)
        def _():
            for slot in (0, 1):
                s = s0 + slot

                @pl.when(s < nsub)
                def _(s=s, slot=slot):
                    @pl.when(pp > 0)
                    def _():
                        wait_out(slot)

                    stage(s, slot)
                    for_rows(lambda i: out_copy(s, i, slot).start())

        return c

    @pl.when(active)
    def _():
        lax.fori_loop(0, npairs, pair, 0)

    @pl.when(active & last)
    def _():
        wait_out(0)

        @pl.when(nsub >= 2)
        def _():
            wait_out(1)


def _deinterleave_matrix():
    pm = np.zeros((256, 256), np.float32)
    i = np.arange(LANES)
    pm[2 * i, i] = 1.0
    pm[2 * i + 1, LANES + i] = 1.0
    return jnp.asarray(pm, bf16)


def _moe(hq, eid, w_gate_up, b_gate_up, w_down, b_down, n):
    n_exp, d, f2 = w_gate_up.shape
    dff = f2 // 2
    nq = d // LANES
    nj = dff // MOE_TF
    assert nj >= 2, "the first and last hidden chunks carry the row gather and scatter"
    g_exp, g_rows, tok, dst = _moe_plan(eid, n_exp, MOE_R)
    n_groups = g_exp.shape[0]

    def jeff(g, j, gr):
        return jnp.where(gr[g] > 0, j, nj - 1)

    return pl.pallas_call(
        functools.partial(_moe_kernel, nq=nq, nj=nj, spare_row0=TOP_K * n * nq),
        grid_spec=pltpu.PrefetchScalarGridSpec(
            num_scalar_prefetch=2,
            grid=(n_groups, nj),
            in_specs=[
                pl.BlockSpec((1, 1, MOE_R), lambda g, j, ge, gr: (g, 0, 0), memory_space=pltpu.SMEM),
                pl.BlockSpec((1, 1, MOE_R), lambda g, j, ge, gr: (g, 0, 0), memory_space=pltpu.SMEM),
                pl.BlockSpec(memory_space=pl.ANY),
                pl.BlockSpec((1, d, 2 * MOE_TF), lambda g, j, ge, gr: (ge[g], 0, jeff(g, j, gr))),
                pl.BlockSpec((1, MOE_TF, d), lambda g, j, ge, gr: (ge[g], jeff(g, j, gr), 0)),
                pl.BlockSpec((1, 1, 2 * MOE_TF), lambda g, j, ge, gr: (ge[g], 0, jeff(g, j, gr))),
                pl.BlockSpec((1, 1, d), lambda g, j, ge, gr: (ge[g], 0, 0)),
                pl.BlockSpec((256, 256), lambda g, j, ge, gr: (0, 0)),
            ],
            out_specs=pl.BlockSpec(memory_space=pl.ANY),
            scratch_shapes=[
                pltpu.VMEM((2, MOE_SUB * nq, LANES), f32),
                pltpu.VMEM((MOE_R, d), bf16),
                pltpu.VMEM((MOE_R, d), f32),
                pltpu.VMEM((d, 2 * MOE_TF), bf16),
                pltpu.VMEM((MOE_TF, d), bf16),
                pltpu.SemaphoreType.DMA((2,)),
                pltpu.SemaphoreType.DMA((2,)),
            ],
        ),
        out_shape=jax.ShapeDtypeStruct(((TOP_K * n + MOE_SUB) * nq, LANES), f32),
        compiler_params=_cparams(("arbitrary", "arbitrary")),
        name="moe",
    )(g_exp, g_rows, tok, dst, hq, w_gate_up, w_down, b_gate_up.reshape(n_exp, 1, f2),
      b_down.reshape(n_exp, 1, d), _deinterleave_matrix())


def _final_kernel(h_ref, y0_ref, y1_ref, y2_ref, y3_ref, gate_ref, g2_ref, b2_ref, op_ref, os_ref, *, tm, nq, alpha, nbp):
    gates = gate_ref[...]
    f = jnp.zeros(h_ref.shape, f32)
    for k, y_ref in enumerate((y0_ref, y1_ref, y2_ref, y3_ref)):
        yk = jnp.concatenate([y_ref[pl.ds(jj, tm, stride=nq), :] for jj in range(nq)], axis=1)
        f = f + gates[:, k:k + 1] * yk
    z = alpha * h_ref[...] + f
    mu = jnp.mean(z, axis=-1, keepdims=True)
    var = jnp.mean(jnp.square(z - mu), axis=-1, keepdims=True)
    y = (z - mu) * lax.rsqrt(var + LN_EPS) * g2_ref[...] + b2_ref[...]
    i = pl.program_id(0)

    @pl.when(i < nbp)
    def _():
        op_ref[...] = y

    @pl.when(i >= nbp)
    def _():
        os_ref[...] = y


def _final(h, y4q, gates, ln2_g, ln2_b, alpha, n_prompt):
    n, d = h.shape
    nq = d // LANES
    tm = _pick(np.gcd(n_prompt, n - n_prompt), (256, 128, 64, 32, 16, 8))
    nb, nbp = n // tm, n_prompt // tm
    row = pl.BlockSpec((1, d), lambda i: (0, 0))
    yspec = lambda k: pl.BlockSpec((tm * nq, LANES), lambda i, k=k: (k * nb + i, 0))
    return pl.pallas_call(
        functools.partial(_final_kernel, tm=tm, nq=nq, alpha=alpha, nbp=nbp),
        grid=(nb,),
        in_specs=[pl.BlockSpec((tm, d), lambda i: (i, 0)), yspec(0), yspec(1), yspec(2), yspec(3),
                  pl.BlockSpec((tm, TOP_K), lambda i: (i, 0)), row, row],
        out_specs=[pl.BlockSpec((tm, d), lambda i: (jnp.minimum(i, nbp - 1), 0)),
                   pl.BlockSpec((tm, d), lambda i: (jnp.maximum(i - nbp, 0), 0))],
        out_shape=[jax.ShapeDtypeStruct((n_prompt, d), f32), jax.ShapeDtypeStruct((n - n_prompt, d), f32)],
        compiler_params=_cparams(("arbitrary",)),
        name="final",
    )(h, y4q, y4q, y4q, y4q, gates, ln2_g.reshape(1, d), ln2_b.reshape(1, d))


def kernel(x_prompt, x_sample, cache_k, cache_v, cache_idx_k, state_ret, page_table, w_in, w_o, ret_gn_w,
           ln1_g, ln1_b, w_router, b_router, w_gate_up, b_gate_up, w_down, b_down, ln2_g, ln2_b):
    depth = w_in.shape[0]
    assert depth == 1, "single-layer step"
    bp, t, d = x_prompt.shape
    bd, tn, _ = x_sample.shape
    assert bp == 1
    past = page_table.shape[1] * PAGE_SIZE
    np_, ns = bp * t, bd * tn
    n = np_ + ns
    alpha = (2 * depth) ** 0.25
    wkv = N_KV_A * HEAD_DIM_A
    layer = lambda a: a.reshape(a.shape[1:])

    x_all = jnp.concatenate([x_prompt.reshape(np_, d), x_sample.reshape(ns, d)], axis=0)
    pos_all = jnp.concatenate([jnp.arange(t), jnp.tile(past + jnp.arange(tn), bd)])
    u_f, u_b = _project(x_all, pos_all, layer(w_in))

    attn_p = _dsa_prompt(u_f, u_b, t)
    pages = lambda a: a.reshape(a.shape[1], PAGE_SIZE * N_KV_A, HEAD_DIM_A)
    attn_s = _dsa_sample(u_f, np_, bd, tn, pages(cache_k), pages(cache_v), jnp.swapaxes(layer(cache_idx_k), 1, 2),
                         page_table)

    zero_state = jnp.zeros((bp, N_HEADS_B, QK_DIM_B, V_DIM_B), f32)
    rb_p, s_p = _retention(u_f, 0, bp, t, zero_state, layer(ret_gn_w), bf16)
    rb_s, s_s = _retention(u_f, np_, bd, tn, layer(state_ret), layer(ret_gn_w), f32)

    attn_all = jnp.concatenate([attn_p, attn_s.astype(bf16)], axis=0)
    rb_all = jnp.concatenate([rb_p, rb_s.astype(bf16)], axis=0)
    h, hq, eid, gates = _tail1(attn_all, rb_all, x_all, layer(w_o), layer(ln1_g), layer(ln1_b), layer(w_router),
                               layer(b_router), alpha)
    y4q = _moe(hq, eid, layer(w_gate_up), layer(b_gate_up), layer(w_down), layer(b_down), n)
    y_p, y_s = _final(h, y4q, gates, layer(ln2_g), layer(ln2_b), alpha, np_)

    kv = lambda rows, c0, lead: u_f[rows, c0:c0 + wkv].reshape(lead + (N_KV_A, HEAD_DIM_A))[None]
    ps, ss = slice(0, np_), slice(np_, n)
    return (
        y_p.reshape(bp, t, d), y_s.reshape(bd, tn, d),
        kv(ps, C_KA, (bp, t)), kv(ps, C_VA, (bp, t)), u_f[ps, C_TAIL:C_TAIL + IDX_DIM].reshape(1, bp, t, IDX_DIM),
        s_p[None],
        kv(ss, C_KA, (bd, tn)), kv(ss, C_VA, (bd, tn)), u_f[ss, C_TAIL:C_TAIL + IDX_DIM].reshape(1, bd, tn, IDX_DIM),
        s_s[None],
    )
```

```python
import functools

import numpy as np
import jax
import jax.numpy as jnp
from jax import lax
from jax.experimental import pallas as pl
from jax.experimental.pallas import tpu as pltpu

f32 = jnp.float32
bf16 = jnp.bfloat16
i32 = jnp.int32

PAGE_SIZE = 128
HEAD_DIM_A = 128
N_HEADS_A = 8
N_KV_A = 4
ROPE_DIM_A = 32
ROPE_THETA = 500000.0
IDX_HEADS = 16
IDX_DIM = 64
IDX_ROPE_DIM = 16
TOPK_MAX = 256
V_DIM_B = 128
N_HEADS_B = 8
QK_DIM_B = 64
RET_CHUNK = 128
RET_THETA = 10000.0
TOP_K = 4
SWIGLU_LIMIT = 7.0
SWIGLU_ALPHA = 1.702
LN_EPS = 1e-5

LANES = 128
TOK_PITCH = 24
VMEM_LIMIT = 56 * 1024 * 1024

PROJ_TN = 512
C_QA, C_KA, C_VA, C_IQ, C_QB, C_KB, C_VB, C_GB, C_TAIL, PROJ_W = 0, 1024, 1536, 2048, 3072, 3584, 4096, 5120, 6144, 6656
PROJ_TILE_TYPES = (6, 6, 1, 0, 2, 2, 3, 4, 0, 0, 0, 0, 5)
QA_SCALE = HEAD_DIM_A ** -0.5 * 1.4426950408889634

INT_MIN = -2 ** 31
KEY_NEG_INF = -2139095041
NEG_BIG = -1e30

NT_DIMS = (((1,), (1,)), ((), ()))
TN_DIMS = (((0,), (0,)), ((), ()))


def _pick(n, cands):
    for c in cands:
        if n % c == 0:
            return c
    raise ValueError(f"no tile for {n}")


def _cparams(sem, vmem=VMEM_LIMIT):
    return pltpu.CompilerParams(dimension_semantics=sem, vmem_limit_bytes=vmem)


def _rope_table(pos, rot_dim, theta, period, scale=1.0, active=LANES):
    half = rot_dim // 2
    inv_freq = 1.0 / (theta ** (jnp.arange(half, dtype=f32) / half))
    ang = pos.astype(f32)[:, None] * inv_freq[None, :]
    cos, sin = jnp.cos(ang), jnp.sin(ang)
    lane = np.arange(LANES)
    d = lane % period
    first = (d < half) & (lane < active)
    second = (d >= half) & (d < rot_dim) & (lane < active)
    idx = np.where(first, d, np.where(second, d - half, 0))
    cos_l, sin_l = cos[:, idx], sin[:, idx]
    c = jnp.where(first | second, cos_l, 1.0)
    s1 = jnp.where(second, sin_l, 0.0)
    s2 = jnp.where(first, -sin_l, 0.0)
    return jnp.concatenate([c, s1, s2], axis=1) * scale


def _proj_kernel(tt_ref, x_ref, w_ref, tab_ref, of_ref, ob_ref, xb_scr):
    j = pl.program_id(1)

    @pl.when(j == 0)
    def _():
        xb_scr[...] = x_ref[...].astype(bf16)

    t = tt_ref[j]

    def emit(half):
        piece = 2 * LANES
        for c0 in range(0, PROJ_TN, piece):
            u = jnp.dot(xb_scr[...], w_ref[:, c0:c0 + piece], preferred_element_type=f32)
            if half is not None:
                c = tab_ref[0, :, 0:LANES]
                s1 = tab_ref[0, :, LANES:2 * LANES]
                s2 = tab_ref[0, :, 2 * LANES:3 * LANES]
                outs = []
                for q in range(piece // LANES):
                    uc = u[:, q * LANES:(q + 1) * LANES]
                    outs.append(uc * c + pltpu.roll(uc, half, 1) * s1 + pltpu.roll(uc, LANES - half, 1) * s2)
                u = jnp.concatenate(outs, axis=1)
            of_ref[:, c0:c0 + piece] = u
            ob_ref[:, c0:c0 + piece] = u.astype(bf16)

    @pl.when(t == 0)
    def _():
        emit(None)

    @pl.when((t == 1) | (t == 6))
    def _():
        emit(ROPE_DIM_A // 2)

    @pl.when((t == 2) | (t == 5))
    def _():
        emit(IDX_ROPE_DIM // 2)

    @pl.when((t == 3) | (t == 4))
    def _():
        emit(QK_DIM_B // 2)


def _project(x_all, pos_all, w_in):
    n, d = x_all.shape
    tm = _pick(n, (768, 512, 384, 256, 128, 64, 32, 16, 8))
    o = np.cumsum((0, 1024, 512, 512, 1024, 64, 16, 512, 512, 1024, 1024))
    wp = jnp.concatenate([w_in[:, o[0]:o[4]], w_in[:, o[6]:o[10]], w_in[:, o[4]:o[6]],
                          jnp.zeros((d, PROJ_W - C_TAIL - 80), w_in.dtype)], axis=1).astype(bf16)
    tabs = jnp.stack([
        _rope_table(pos_all, ROPE_DIM_A, ROPE_THETA, HEAD_DIM_A),
        _rope_table(pos_all, ROPE_DIM_A, ROPE_THETA, HEAD_DIM_A),
        _rope_table(pos_all, IDX_ROPE_DIM, ROPE_THETA, IDX_DIM),
        _rope_table(pos_all, QK_DIM_B, RET_THETA, QK_DIM_B),
        _rope_table(pos_all, QK_DIM_B, RET_THETA, QK_DIM_B, scale=QK_DIM_B ** -0.5),
        _rope_table(pos_all, IDX_ROPE_DIM, ROPE_THETA, IDX_DIM, active=IDX_DIM),
        _rope_table(pos_all, ROPE_DIM_A, ROPE_THETA, HEAD_DIM_A, scale=QA_SCALE),
    ])
    tt = jnp.asarray(PROJ_TILE_TYPES, i32)
    nj = PROJ_W // PROJ_TN
    return pl.pallas_call(
        _proj_kernel,
        grid_spec=pltpu.PrefetchScalarGridSpec(
            num_scalar_prefetch=1,
            grid=(n // tm, nj),
            in_specs=[
                pl.BlockSpec((tm, d), lambda i, j, tt: (i, 0)),
                pl.BlockSpec((d, PROJ_TN), lambda i, j, tt: (0, j)),
                pl.BlockSpec((1, tm, 3 * LANES), lambda i, j, tt: (tt[j], i, 0)),
            ],
            out_specs=[
                pl.BlockSpec((tm, PROJ_TN), lambda i, j, tt: (i, j)),
                pl.BlockSpec((tm, PROJ_TN), lambda i, j, tt: (i, j)),
            ],
            scratch_shapes=[pltpu.VMEM((tm, d), bf16)],
        ),
        out_shape=[jax.ShapeDtypeStruct((n, PROJ_W), f32), jax.ShapeDtypeStruct((n, PROJ_W), bf16)],
        compiler_params=_cparams(("arbitrary", "arbitrary")),
        name="proj",
    )(tt, x_all, wp, tabs)


def _key_to_float(key):
    return pltpu.bitcast(key ^ ((key >> 31) & 0x7FFFFFFF), f32)


def _kth_threshold(count_ge, shape, k):
    def body(step, ans):
        cand = ans + jnp.left_shift(jnp.int32(1), 31 - step)
        return jnp.where(count_ge(_key_to_float(cand)) >= k, cand, ans)

    ans = lax.fori_loop(0, 32, body, jnp.full(shape, INT_MIN, i32))
    return _key_to_float(jnp.maximum(ans, KEY_NEG_INF + 1))


def _lane_blocks(x):
    return [x[:, i * LANES:(i + 1) * LANES] for i in range(x.shape[1] // LANES)]


def _flash_update(qs, ks, vs, bias, m_scr, l_scr, acc_scr, batched):
    n = len(qs)
    score = lambda g: lax.dot_general(qs[g], ks[g], NT_DIMS, preferred_element_type=f32) + bias

    def update(g, sm):
        m_old = m_scr[g]
        m_new = jnp.maximum(m_old, jnp.max(functools.reduce(jnp.maximum, _lane_blocks(sm)), axis=1, keepdims=True))
        alpha = jnp.exp2(m_old - m_new)
        p = jnp.exp2(sm - m_new)
        l_scr[g] = alpha * l_scr[g] + jnp.sum(functools.reduce(jnp.add, _lane_blocks(p)), axis=1, keepdims=True)
        m_scr[g] = m_new
        acc_scr[g] = alpha * acc_scr[g] + jnp.dot(p.astype(bf16), vs[g], preferred_element_type=f32)

    if batched:
        sms = [score(g) for g in range(n)]
        for g in range(n):
            update(g, sms[g])
    else:
        for g in range(n):
            update(g, score(g))


DSA_GROUP_BATCH = 4


def _dsa_prompt_kernel(iq_ref, iw_ref, qa_ref, ikd_ref, ka_ref, vat_ref, o_ref,
                       qst, qgt, sct, m_scr, l_scr, acc_scr, *, tq, tk, topk):
    i = pl.program_id(0)
    n_chunks = (i * tq + tq + tk - 1) // tk
    row = lax.broadcasted_iota(i32, (LANES, tq), 0)

    for p in range(IDX_HEADS // 2):
        blk = iq_ref[:, p * LANES:(p + 1) * LANES].T
        qst[:, (2 * p) * tq:(2 * p + 1) * tq] = jnp.where(row < IDX_DIM, blk, 0.0).astype(bf16)
        qst[:, (2 * p + 1) * tq:(2 * p + 2) * tq] = jnp.where(row >= IDX_DIM, blk, 0.0).astype(bf16)
    wt = iw_ref[...].T * (IDX_DIM ** -0.5 * IDX_HEADS ** -0.5)
    for h in range(N_HEADS_A):
        qgt[h // 2, :, (h % 2) * tq:(h % 2 + 1) * tq] = qa_ref[:, h * LANES:(h + 1) * LANES].T.astype(bf16)

    qpos = i * tq + lax.broadcasted_iota(i32, (tk, tq), 1)

    def score_body(c, carry):
        k0 = pl.multiple_of(c * tk, tk)
        logits = jnp.dot(ikd_ref[pl.ds(k0, tk), :], qst[...], preferred_element_type=f32)
        acc = jnp.zeros((tk, tq), f32)
        for h in range(IDX_HEADS):
            acc = acc + jnp.maximum(logits[:, h * tq:(h + 1) * tq], 0.0) * wt[IDX_DIM + h:IDX_DIM + h + 1, :]
        kpos = k0 + lax.broadcasted_iota(i32, (tk, tq), 0)
        sct[c] = jnp.where(kpos <= qpos, acc, -jnp.inf)
        return carry

    lax.fori_loop(0, n_chunks, score_body, 0)

    def count_ge(cand):
        cb = jnp.broadcast_to(cand, (8, tq))

        def body(c, accs):
            accs = list(accs)
            for r in range(tk // 8):
                accs[r % 4] = accs[r % 4] + jnp.where(sct[c, r * 8:(r + 1) * 8, :] >= cb, 1.0, 0.0)
            return tuple(accs)

        accs = lax.fori_loop(0, n_chunks, body, (jnp.zeros((8, tq), f32),) * 4)
        return jnp.sum((accs[0] + accs[1]) + (accs[2] + accs[3]), axis=0, keepdims=True)

    thr = _kth_threshold(count_ge, (1, tq), float(topk))

    m_scr[...] = jnp.full(m_scr.shape, NEG_BIG, f32)
    l_scr[...] = jnp.zeros(l_scr.shape, f32)
    acc_scr[...] = jnp.zeros(acc_scr.shape, f32)

    def att_body(c, carry):
        k0 = pl.multiple_of(c * tk, tk)
        bias1 = jnp.where(sct[c] >= thr, 0.0, NEG_BIG)
        bias = jnp.concatenate([bias1, bias1], axis=1)
        for g0 in range(0, N_KV_A, DSA_GROUP_BATCH):
            groups = range(g0, g0 + DSA_GROUP_BATCH)
            sms = {g: jnp.dot(ka_ref[pl.ds(k0, tk), g * LANES:(g + 1) * LANES], qgt[g],
                              preferred_element_type=f32) + bias for g in groups}
            m_new = {g: jnp.maximum(m_scr[g], jnp.max(sms[g], axis=0, keepdims=True)) for g in groups}
            alpha = {g: jnp.exp2(m_scr[g] - m_new[g]) for g in groups}
            ps = {g: jnp.exp2(sms[g] - m_new[g]) for g in groups}
            for g in groups:
                l_scr[g] = alpha[g] * l_scr[g] + jnp.sum(ps[g], axis=0, keepdims=True)
                m_scr[g] = m_new[g]
            for g in groups:
                acc_scr[g] = alpha[g] * acc_scr[g] + jnp.dot(vat_ref[c, g * LANES:(g + 1) * LANES, :],
                                                             ps[g].astype(bf16), preferred_element_type=f32)
        return carry

    lax.fori_loop(0, n_chunks, att_body, 0)

    for g in range(N_KV_A):
        o = acc_scr[g] / l_scr[g]
        o_ref[:, (2 * g) * LANES:(2 * g + 1) * LANES] = o[:, 0:tq].T.astype(o_ref.dtype)
        o_ref[:, (2 * g + 1) * LANES:(2 * g + 2) * LANES] = o[:, tq:2 * tq].T.astype(o_ref.dtype)


def _dsa_prompt(u_f, u_b, t):
    tq = _pick(t, (128,))
    tk = _pick(t, (512, 256, 128))
    topk = min(TOPK_MAX, t // 4)
    wa = N_HEADS_A * HEAD_DIM_A
    wkv = N_KV_A * HEAD_DIM_A
    ik_b = u_b[:t, C_TAIL:C_TAIL + IDX_DIM]
    ikd = jnp.concatenate([ik_b, ik_b], axis=1)
    vat = u_b[:t, C_VA:C_VA + wkv].reshape(t // tk, tk, wkv).transpose(0, 2, 1)
    kern = functools.partial(_dsa_prompt_kernel, tq=tq, tk=tk, topk=topk)
    one = pl.Buffered(1)
    return pl.pallas_call(
        kern,
        grid=(t // tq,),
        in_specs=[
            pl.BlockSpec((tq, IDX_HEADS * IDX_DIM), lambda i: (i, C_IQ // (IDX_HEADS * IDX_DIM))),
            pl.BlockSpec((tq, LANES), lambda i: (i, C_TAIL // LANES)),
            pl.BlockSpec((tq, wa), lambda i: (i, C_QA // wa)),
            pl.BlockSpec((t, LANES), lambda i: (0, 0), pipeline_mode=one),
            pl.BlockSpec((t, wkv), lambda i: (0, C_KA // wkv), pipeline_mode=one),
            pl.BlockSpec((t // tk, wkv, tk), lambda i: (0, 0, 0), pipeline_mode=one),
        ],
        out_specs=pl.BlockSpec((tq, wa), lambda i: (i, 0)),
        out_shape=jax.ShapeDtypeStruct((t, wa), bf16),
        scratch_shapes=[
            pltpu.VMEM((LANES, IDX_HEADS * tq), bf16),
            pltpu.VMEM((N_KV_A, LANES, 2 * tq), bf16),
            pltpu.VMEM((t // tk, tk, tq), f32),
            pltpu.VMEM((N_KV_A, 1, 2 * tq), f32),
            pltpu.VMEM((N_KV_A, 1, 2 * tq), f32),
            pltpu.VMEM((N_KV_A, LANES, 2 * tq), f32),
        ],
        compiler_params=_cparams(("arbitrary",)),
        name="dsa_prompt",
    )(u_f, u_f, u_f, ikd, u_b, vat)


DSA_SCORE_CHUNK_PAGES = 32
DSA_PAGE_ISSUE_UNROLL = 8


def _dsa_sample_score_kernel(pt_ref, idx_hbm, iq_ref, tail_ref, sc_ref, buf, qs, wst, sem, *, tn, n_pages):
    b = pl.program_id(0)
    nb = pl.num_programs(0)
    cpp = DSA_SCORE_CHUNK_PAGES
    n_chunks = n_pages // cpp
    kw = cpp * PAGE_SIZE
    wscale = IDX_DIM ** -0.5 * IDX_HEADS ** -0.5

    def issue(seq, slot):
        def body(blk, c):
            for u in range(DSA_PAGE_ISSUE_UNROLL):
                p = blk * DSA_PAGE_ISSUE_UNROLL + u
                pltpu.make_async_copy(idx_hbm.at[pt_ref[seq, p]], buf.at[slot, p], sem.at[slot]).start()
            return c

        lax.fori_loop(0, n_pages // DSA_PAGE_ISSUE_UNROLL, body, 0)

    def wait(slot):
        pltpu.make_async_copy(idx_hbm.at[pl.ds(0, n_pages)], buf.at[slot], sem.at[slot]).wait()

    @pl.when(b == 0)
    def _():
        issue(0, 0)

    iq = iq_ref[...]
    w = tail_ref[...]
    for h in range(IDX_HEADS):
        qs[h * tn:(h + 1) * tn, :] = iq[:, h * IDX_DIM:(h + 1) * IDX_DIM]
        wst[h * tn:(h + 1) * tn, :] = jnp.broadcast_to(w[:, IDX_DIM + h:IDX_DIM + h + 1] * wscale, (tn, LANES))
    qsb = qs[...].astype(bf16)

    def head_sum(logits):
        width = logits.shape[1]
        acc = jnp.zeros((tn, width), f32)
        for h in range(IDX_HEADS):
            wh = wst[h * tn:(h + 1) * tn, :]
            acc = acc + jnp.maximum(logits[h * tn:(h + 1) * tn, :], 0.0) * jnp.concatenate([wh] * (width // LANES), axis=1)
        return acc

    def past_scores(slot):
        @pl.when(b + 1 < nb)
        def _():
            issue(b + 1, 1 - slot)

        wait(slot)
        for c in range(n_chunks):
            keys_t = jnp.concatenate([buf[slot, c * cpp + p] for p in range(cpp)], axis=1).astype(bf16)
            sc_ref[0, c] = head_sum(jnp.dot(qsb, keys_t, preferred_element_type=f32))

    @pl.when(b % 2 == 0)
    def _():
        past_scores(0)

    @pl.when(b % 2 == 1)
    def _():
        past_scores(1)

    ik_new = tail_ref[:, 0:IDX_DIM].astype(bf16)
    kpad = jnp.concatenate([ik_new, jnp.zeros((LANES - tn, IDX_DIM), bf16)], axis=0)
    s_new = head_sum(lax.dot_general(qsb, kpad, NT_DIMS, preferred_element_type=f32))
    qi = lax.broadcasted_iota(i32, (tn, LANES), 0)
    kj = lax.broadcasted_iota(i32, (tn, LANES), 1)
    snew = jnp.where(kj <= qi, s_new, -jnp.inf)
    sc_ref[0, n_chunks] = jnp.concatenate([snew, jnp.full((tn, kw - LANES), -jnp.inf, f32)], axis=1)


def _dsa_sample_thr_kernel(sc_ref, thr_ref, *, ns, nc, tn, kw, topk):
    def count_ge(cand):
        out = []
        for s in range(ns):
            cb = jnp.broadcast_to(cand[s], (tn, LANES))
            parts = [jnp.zeros((tn, LANES), f32)] * 4
            for c in range(nc):
                blk = sc_ref[s, c]
                for q in range(kw // LANES):
                    parts[q % 4] = parts[q % 4] + jnp.where(blk[:, q * LANES:(q + 1) * LANES] >= cb, 1.0, 0.0)
            out.append(jnp.sum((parts[0] + parts[1]) + (parts[2] + parts[3]), axis=1, keepdims=True))
        return jnp.stack(out)

    thr = _kth_threshold(count_ge, (ns, tn, 1), float(topk))
    thr_ref[...] = jnp.broadcast_to(thr, (ns, tn, LANES))


def _dsa_sample_attn_kernel(pt_ref, *refs, tn, ppg, n_groups):
    kpages = refs[:ppg]
    vpages = refs[ppg:2 * ppg]
    qa_ref, kn_ref, vn_ref, sc_ref, scn_ref, thr_ref, o_ref, qg, m_scr, l_scr, acc_scr = refs[2 * ppg:]
    c = pl.program_id(1)

    @pl.when(c == 0)
    def _():
        qa = qa_ref[...]
        for g in range(N_KV_A):
            qg[g, 0:tn, :] = qa[:, (2 * g) * LANES:(2 * g + 1) * LANES]
            qg[g, tn:2 * tn, :] = qa[:, (2 * g + 1) * LANES:(2 * g + 2) * LANES]
        m_scr[...] = jnp.full(m_scr.shape, NEG_BIG, f32)
        l_scr[...] = jnp.zeros(l_scr.shape, f32)
        acc_scr[...] = jnp.zeros(acc_scr.shape, f32)

    thr = thr_ref[0][:, 0:1]
    thr2 = jnp.concatenate([thr, thr], axis=0)

    def attend(ks, vs, sc):
        bias = jnp.where(jnp.concatenate([sc, sc], axis=0) >= thr2, 0.0, NEG_BIG)
        _flash_update([qg[g].astype(bf16) for g in range(N_KV_A)], ks, vs, bias, m_scr, l_scr, acc_scr, batched=True)

    group = lambda pages, g: jnp.concatenate(
        [p[pl.ds(g, PAGE_SIZE, stride=N_KV_A), :] for p in pages], axis=0).astype(bf16)
    attend([group(kpages, g) for g in range(N_KV_A)], [group(vpages, g) for g in range(N_KV_A)], sc_ref[0, 0])

    @pl.when(c == n_groups - 1)
    def _():
        zpad = jnp.zeros((LANES - tn, N_KV_A * HEAD_DIM_A), bf16)
        attend(_lane_blocks(jnp.concatenate([kn_ref[...].astype(bf16), zpad], axis=0)),
               _lane_blocks(jnp.concatenate([vn_ref[...].astype(bf16), zpad], axis=0)),
               scn_ref[0, 0][:, 0:LANES])
        for g in range(N_KV_A):
            o = acc_scr[g] / l_scr[g]
            o_ref[:, (2 * g) * LANES:(2 * g + 1) * LANES] = o[0:tn]
            o_ref[:, (2 * g + 1) * LANES:(2 * g + 2) * LANES] = o[tn:2 * tn]


def _dsa_sample(u_f, row0, bd, tn, cache_k, cache_v, idx_kt, page_table):
    n_pages = page_table.shape[1]
    past = n_pages * PAGE_SIZE
    topk = min(TOPK_MAX, (past + tn) // 4)
    ppa = _pick(n_pages, (16, 8, 4, 2, 1))
    pps = _pick(n_pages, (DSA_SCORE_CHUNK_PAGES,))
    nga, ngs = n_pages // ppa, n_pages // pps
    kwa, kws = ppa * PAGE_SIZE, pps * PAGE_SIZE
    per = kws // kwa
    wkv = N_KV_A * HEAD_DIM_A
    wa = N_HEADS_A * HEAD_DIM_A
    rb0 = row0 // tn

    def kv_spec(j):
        return pl.BlockSpec((None, PAGE_SIZE * N_KV_A, HEAD_DIM_A), lambda b, c, pt, j=j: (pt[b, c * ppa + j], 0, 0))

    score = pl.pallas_call(
        functools.partial(_dsa_sample_score_kernel, tn=tn, n_pages=n_pages),
        grid_spec=pltpu.PrefetchScalarGridSpec(
            num_scalar_prefetch=1,
            grid=(bd,),
            in_specs=[
                pl.BlockSpec(memory_space=pl.ANY),
                pl.BlockSpec((tn, IDX_HEADS * IDX_DIM), lambda b, pt: (rb0 + b, C_IQ // (IDX_HEADS * IDX_DIM))),
                pl.BlockSpec((tn, LANES), lambda b, pt: (rb0 + b, C_TAIL // LANES)),
            ],
            out_specs=pl.BlockSpec((1, ngs + 1, tn, kws), lambda b, pt: (b, 0, 0, 0)),
            scratch_shapes=[
                pltpu.VMEM((2, n_pages, IDX_DIM, PAGE_SIZE), f32),
                pltpu.VMEM((IDX_HEADS * tn, IDX_DIM), f32),
                pltpu.VMEM((IDX_HEADS * tn, LANES), f32),
                pltpu.SemaphoreType.DMA((2,)),
            ],
        ),
        out_shape=jax.ShapeDtypeStruct((bd, ngs + 1, tn, kws), f32),
        compiler_params=_cparams(("arbitrary",)),
        name="dsa_sample_score",
    )
    scores = score(page_table, idx_kt, u_f, u_f)

    nst = _pick(bd, (8, 4, 2, 1))
    thr = pl.pallas_call(
        functools.partial(_dsa_sample_thr_kernel, ns=nst, nc=ngs + 1, tn=tn, kw=kws, topk=topk),
        grid=(bd // nst,),
        in_specs=[pl.BlockSpec((nst, ngs + 1, tn, kws), lambda i: (i, 0, 0, 0))],
        out_specs=pl.BlockSpec((nst, tn, LANES), lambda i: (i, 0, 0)),
        out_shape=jax.ShapeDtypeStruct((bd, tn, LANES), f32),
        compiler_params=_cparams(("arbitrary",)),
        name="dsa_sample_thr",
    )(scores)

    attn = pl.pallas_call(
        functools.partial(_dsa_sample_attn_kernel, tn=tn, ppg=ppa, n_groups=nga),
        grid_spec=pltpu.PrefetchScalarGridSpec(
            num_scalar_prefetch=1,
            grid=(bd, nga),
            in_specs=[kv_spec(j) for j in range(ppa)] + [kv_spec(j) for j in range(ppa)] + [
                pl.BlockSpec((tn, wa), lambda b, c, pt: (rb0 + b, C_QA // wa)),
                pl.BlockSpec((tn, wkv), lambda b, c, pt: (rb0 + b, C_KA // wkv)),
                pl.BlockSpec((tn, wkv), lambda b, c, pt: (rb0 + b, C_VA // wkv)),
                pl.BlockSpec((1, 1, tn, kwa), lambda b, c, pt: (b, c // per, 0, c % per)),
                pl.BlockSpec((1, 1, tn, kwa), lambda b, c, pt: (b, ngs, 0, 0)),
                pl.BlockSpec((1, tn, LANES), lambda b, c, pt: (b, 0, 0)),
            ],
            out_specs=pl.BlockSpec((tn, wa), lambda b, c, pt: (b, 0)),
            scratch_shapes=[
                pltpu.VMEM((N_KV_A, 2 * tn, LANES), f32),
                pltpu.VMEM((N_KV_A, 2 * tn, 1), f32),
                pltpu.VMEM((N_KV_A, 2 * tn, 1), f32),
                pltpu.VMEM((N_KV_A, 2 * tn, LANES), f32),
            ],
        ),
        out_shape=jax.ShapeDtypeStruct((bd * tn, wa), f32),
        compiler_params=_cparams(("arbitrary", "arbitrary")),
        name="dsa_sample_attn",
    )
    return attn(page_table, *([cache_k] * ppa), *([cache_v] * ppa), u_f, u_f, u_f, scores, scores, thr)


def _ret_kernel(q_ref, k_ref, v_ref, g_ref, gnw_ref, dmask_ref, qdec_ref, kdec_ref, sdec_ref, s0_ref,
                o_ref, sout_ref, s_scr, *, n_chunks):
    c = pl.program_id(1)

    @pl.when(c == 0)
    def _():
        s_scr[...] = s0_ref[0]

    for h in range(N_HEADS_B):
        q = q_ref[:, h * QK_DIM_B:(h + 1) * QK_DIM_B]
        k = k_ref[:, h * QK_DIM_B:(h + 1) * QK_DIM_B]
        v = v_ref[:, h * V_DIM_B:(h + 1) * V_DIM_B].astype(bf16)
        gate = g_ref[:, h * V_DIM_B:(h + 1) * V_DIM_B]
        qb = q.astype(bf16)
        att = lax.dot_general(qb, k.astype(bf16), NT_DIMS, preferred_element_type=f32) * dmask_ref[h]
        s_old = s_scr[h]
        o = (jnp.dot(att.astype(bf16), v, preferred_element_type=f32)
             + jnp.dot(qb, s_old.astype(bf16), preferred_element_type=f32) * qdec_ref[h])
        kd = (k * kdec_ref[h]).astype(bf16)
        s_scr[h] = s_old * sdec_ref[h] + lax.dot_general(kd, v, TN_DIMS, preferred_element_type=f32)
        mu = jnp.mean(o, axis=-1, keepdims=True)
        var = jnp.mean(jnp.square(o - mu), axis=-1, keepdims=True)
        rb = (o - mu) * lax.rsqrt(var + LN_EPS) * gnw_ref[:, h * V_DIM_B:(h + 1) * V_DIM_B]
        rb = rb * (gate / (1.0 + jnp.exp(-gate)))
        o_ref[:, h * V_DIM_B:(h + 1) * V_DIM_B] = rb.astype(o_ref.dtype)

    @pl.when(c == n_chunks - 1)
    def _():
        sout_ref[0] = s_scr[...]


def _retention(u_f, row0, nb, t, state0, gn_w, out_dtype):
    ch = min(RET_CHUNK, t)
    if t % ch:
        ch = t
    n = t // ch
    hb = N_HEADS_B
    lg = jnp.log1p(-jnp.exp2(-5.0 - jnp.arange(hb, dtype=f32)))
    i = jnp.arange(ch)
    diff = i[:, None] - i[None, :]
    dmask = jnp.where(diff >= 0, jnp.exp(lg[:, None, None] * jnp.maximum(diff, 0)), 0.0)
    qdec = jnp.broadcast_to(jnp.exp(lg[:, None] * (i + 1))[:, :, None], (hb, ch, V_DIM_B))
    kdec = jnp.broadcast_to(jnp.exp(lg[:, None] * (ch - 1 - i))[:, :, None], (hb, ch, QK_DIM_B))
    sdec = jnp.broadcast_to(jnp.exp(lg * ch)[:, None, None], (hb, 1, V_DIM_B))
    wqk = hb * QK_DIM_B
    wv = hb * V_DIM_B
    rb0 = row0 // ch
    full3 = lambda shp: pl.BlockSpec(shp, lambda b, c: (0, 0, 0))
    return pl.pallas_call(
        functools.partial(_ret_kernel, n_chunks=n),
        grid=(nb, n),
        in_specs=[
            pl.BlockSpec((ch, wqk), lambda b, c: (rb0 + b * n + c, C_QB // wqk)),
            pl.BlockSpec((ch, wqk), lambda b, c: (rb0 + b * n + c, C_KB // wqk)),
            pl.BlockSpec((ch, wv), lambda b, c: (rb0 + b * n + c, C_VB // wv)),
            pl.BlockSpec((ch, wv), lambda b, c: (rb0 + b * n + c, C_GB // wv)),
            pl.BlockSpec((1, wv), lambda b, c: (0, 0)),
            full3((hb, ch, ch)), full3((hb, ch, V_DIM_B)), full3((hb, ch, QK_DIM_B)), full3((hb, 1, V_DIM_B)),
            pl.BlockSpec((1, hb, QK_DIM_B, V_DIM_B), lambda b, c: (b, 0, 0, 0)),
        ],
        out_specs=[
            pl.BlockSpec((ch, wv), lambda b, c: (b * n + c, 0)),
            pl.BlockSpec((1, hb, QK_DIM_B, V_DIM_B), lambda b, c: (b, 0, 0, 0)),
        ],
        out_shape=[jax.ShapeDtypeStruct((nb * t, wv), out_dtype),
                   jax.ShapeDtypeStruct((nb, hb, QK_DIM_B, V_DIM_B), f32)],
        scratch_shapes=[pltpu.VMEM((hb, QK_DIM_B, V_DIM_B), f32)],
        compiler_params=_cparams(("arbitrary", "arbitrary")),
        name="retention",
    )(u_f, u_f, u_f, u_f, gn_w.reshape(1, wv), dmask, qdec, kdec, sdec, state0)


def _split_hi_lo(a):
    hi = a.astype(bf16)
    return hi, (a - hi.astype(f32)).astype(bf16)


def _tail1_kernel(attn_ref, rb_ref, x_ref, wo_ref, g1_ref, b1_ref, wrh_ref, wrl_ref, br_ref,
                  h_ref, hq_ref, eid_ref, gate_ref, *, tm, n_exp, alpha, wa):
    mix = (jnp.dot(attn_ref[...], wo_ref[0:wa, :], preferred_element_type=f32)
           + jnp.dot(rb_ref[...], wo_ref[wa:, :], preferred_element_type=f32))
    z = alpha * x_ref[...] + mix
    mu = jnp.mean(z, axis=-1, keepdims=True)
    var = jnp.mean(jnp.square(z - mu), axis=-1, keepdims=True)
    h = (z - mu) * lax.rsqrt(var + LN_EPS) * g1_ref[...] + b1_ref[...]
    h_ref[...] = h
    nq = h.shape[1] // LANES
    for j in range(TOK_PITCH):
        slab = h[:, j * LANES:(j + 1) * LANES] if j < nq else jnp.zeros((tm, LANES), f32)
        hq_ref[pl.ds(j, tm, stride=TOK_PITCH), :] = slab

    hh, hl = _split_hi_lo(h)
    logits = (jnp.dot(hh, wrh_ref[...], preferred_element_type=f32)
              + jnp.dot(hl, wrh_ref[...], preferred_element_type=f32)
              + jnp.dot(hh, wrl_ref[...], preferred_element_type=f32)) + br_ref[...]
    lane = lax.broadcasted_iota(i32, (tm, LANES), 1)
    lanef = lane.astype(f32)
    logits = jnp.where(lane < n_exp, logits, -jnp.inf)
    vals, ids = [], []
    for _ in range(TOP_K):
        m = jnp.max(logits, axis=1, keepdims=True)
        idx = jnp.min(jnp.where(logits == m, lanef, float(LANES)), axis=1, keepdims=True)
        vals.append(m)
        ids.append(idx)
        logits = jnp.where(lanef == idx, -jnp.inf, logits)
    es = [jnp.exp(v - vals[0]) for v in vals]
    den = es[0] + es[1] + es[2] + es[3]
    eid_ref[...] = jnp.concatenate(ids, axis=1).astype(i32)
    gate_ref[...] = jnp.concatenate([e / den for e in es], axis=1)


def _tail1(attn_b, rb_b, x_all, w_o, ln1_g, ln1_b, w_router, b_router, alpha):
    n, d = x_all.shape
    wa = attn_b.shape[1]
    n_exp = w_router.shape[1]
    tm = _pick(n, (256, 128, 64, 32, 16, 8))
    nq = d // LANES
    wr = jnp.zeros((d, LANES), f32).at[:, :n_exp].set(w_router)
    wrh, wrl = _split_hi_lo(wr)
    br = jnp.zeros((1, LANES), f32).at[0, :n_exp].set(b_router)
    row = lambda w: pl.BlockSpec((1, w), lambda i: (0, 0))
    return pl.pallas_call(
        functools.partial(_tail1_kernel, tm=tm, n_exp=n_exp, alpha=alpha, wa=wa),
        grid=(n // tm,),
        in_specs=[
            pl.BlockSpec((tm, wa), lambda i: (i, 0)),
            pl.BlockSpec((tm, rb_b.shape[1]), lambda i: (i, 0)),
            pl.BlockSpec((tm, d), lambda i: (i, 0)),
            pl.BlockSpec(w_o.shape, lambda i: (0, 0), pipeline_mode=pl.Buffered(1)),
            row(d), row(d),
            pl.BlockSpec((d, LANES), lambda i: (0, 0)), pl.BlockSpec((d, LANES), lambda i: (0, 0)), row(LANES),
        ],
        out_specs=[
            pl.BlockSpec((tm, d), lambda i: (i, 0)),
            pl.BlockSpec((tm * TOK_PITCH, LANES), lambda i: (i, 0)),
            pl.BlockSpec((tm, TOP_K), lambda i: (i, 0)),
            pl.BlockSpec((tm, TOP_K), lambda i: (i, 0)),
        ],
        out_shape=[jax.ShapeDtypeStruct((n, d), f32), jax.ShapeDtypeStruct((n * TOK_PITCH, LANES), f32),
                   jax.ShapeDtypeStruct((n, TOP_K), i32), jax.ShapeDtypeStruct((n, TOP_K), f32)],
        compiler_params=_cparams(("arbitrary",)),
        name="tail1",
    )(attn_b, rb_b, x_all, w_o.astype(bf16), ln1_g.reshape(1, d), ln1_b.reshape(1, d), wrh, wrl, br)


MOE_R = 2048
MOE_SUB = 256
MOE_TF = 256
MOE_ISSUE_UNROLL = 8
MOE_TILES = (512, 256, 128)


def _moe_plan(eid, n_exp, r_cap):
    n = eid.shape[0]
    p = n * TOP_K
    flat = eid.reshape(p)
    onehot = (flat[:, None] == jnp.arange(n_exp, dtype=i32)[None, :]).astype(i32)
    csum = jnp.cumsum(onehot, axis=0)
    rank = jnp.sum((csum - onehot) * onehot, axis=1)
    counts = csum[-1]
    ngrp = (counts + r_cap - 1) // r_cap
    gend = jnp.cumsum(ngrp)
    gstart = gend - ngrp
    g_of = gstart[flat] + rank // r_cap
    slot = rank % r_cap
    n_groups = n_exp + p // r_cap
    pair = jnp.full((n_groups, r_cap), -1, i32).at[g_of, slot].set(jnp.arange(p, dtype=i32))
    t_idx, k_idx = pair // TOP_K, pair % TOP_K
    spare = jnp.broadcast_to(p + jnp.arange(r_cap, dtype=i32) % MOE_SUB, (n_groups, r_cap))
    tok = jnp.where(pair >= 0, t_idx, 0)
    dst = jnp.where(pair >= 0, k_idx * n + t_idx, spare)
    gid = jnp.arange(n_groups, dtype=i32)
    total = gend[-1]
    gclamp = jnp.minimum(gid, total - 1)
    g_exp = jnp.sum((gend[None, :] <= gclamp[:, None]).astype(i32), axis=1)
    g_rows = jnp.clip(counts[g_exp] - (gclamp - gstart[g_exp]) * r_cap, 0, r_cap)
    g_rows = jnp.where(gid < total, g_rows, 0).astype(i32)
    return g_exp, g_rows, tok.reshape(n_groups, 1, r_cap), dst.reshape(n_groups, 1, r_cap)


def _moe_kernel(ge_ref, gr_ref, tok_ref, dst_ref, hq_ref, wgu_ref, wd_ref, bgu_ref, bd_ref, pm_ref, y_ref,
                qbuf, xb, acc, wgu_b, wd_b, sem_in, sem_out, *, nq, nj, spare_row0):
    pitch = TOK_PITCH
    g = pl.program_id(0)
    j = pl.program_id(1)
    rows = gr_ref[g]
    nsub = (rows + MOE_SUB - 1) // MOE_SUB
    active = rows > 0

    def in_copy(s, i, slot):
        t = tok_ref[0, 0, s * MOE_SUB + i]
        return pltpu.make_async_copy(hq_ref.at[pl.ds(pl.multiple_of(t * pitch, pitch), pitch), :],
                                     qbuf.at[slot, pl.ds(pl.multiple_of(i * pitch, pitch), pitch), :], sem_in.at[slot])

    def out_copy(s, i, slot):
        d = dst_ref[0, 0, s * MOE_SUB + i]
        return pltpu.make_async_copy(qbuf.at[slot, pl.ds(pl.multiple_of(i * pitch, pitch), pitch), :],
                                     y_ref.at[pl.ds(pl.multiple_of(d * pitch, pitch), pitch), :], sem_out.at[slot])

    def for_rows(fn):
        def body(b, c):
            for u in range(MOE_ISSUE_UNROLL):
                fn(b * MOE_ISSUE_UNROLL + u)
            return c

        lax.fori_loop(0, MOE_SUB // MOE_ISSUE_UNROLL, body, 0)

    def wait_in(slot):
        pltpu.make_async_copy(hq_ref.at[pl.ds(0, MOE_SUB * pitch), :], qbuf.at[slot], sem_in.at[slot]).wait()

    def wait_out(slot):
        pltpu.make_async_copy(qbuf.at[slot], y_ref.at[pl.ds(0, MOE_SUB * pitch), :], sem_out.at[slot]).wait()

    def convert(s, slot):
        r0 = pl.multiple_of(s * MOE_SUB, MOE_SUB)
        for jj in range(nq):
            xb[pl.ds(r0, MOE_SUB), jj * LANES:(jj + 1) * LANES] = (
                qbuf[slot, pl.ds(jj, MOE_SUB, stride=pitch), :].astype(bf16))
        acc[pl.ds(r0, MOE_SUB), :] = jnp.zeros((MOE_SUB, acc.shape[1]), f32)

    def stage(s, slot):
        r0 = pl.multiple_of(s * MOE_SUB, MOE_SUB)
        a = acc[pl.ds(r0, MOE_SUB), :] + bd_ref[0]
        for jj in range(nq):
            qbuf[slot, pl.ds(jj, MOE_SUB, stride=pitch), :] = a[:, jj * LANES:(jj + 1) * LANES]

    npairs = (nsub + 1) // 2

    @pl.when((g == 0) & (j == 0))
    def _():
        qbuf[1] = jnp.zeros(qbuf.shape[1:], f32)
        fill = pltpu.make_async_copy(qbuf.at[1], y_ref.at[pl.ds(spare_row0, MOE_SUB * pitch), :], sem_out.at[1])
        fill.start()
        fill.wait()

    def tile(r0, size):
        gu = jnp.dot(xb[pl.ds(r0, size), :], wgu_b[...], preferred_element_type=f32) + bgu_ref[0]
        gub = gu.astype(bf16)
        gates, ups = [], []
        for q in range(2 * MOE_TF // 256):
            de = jnp.dot(gub[:, q * 256:(q + 1) * 256], pm_ref[...], preferred_element_type=f32)
            gates.append(de[:, 0:LANES])
            ups.append(de[:, LANES:2 * LANES])
        gate = jnp.minimum(jnp.concatenate(gates, axis=1), SWIGLU_LIMIT)
        up = jnp.clip(jnp.concatenate(ups, axis=1), -SWIGLU_LIMIT, SWIGLU_LIMIT)
        act = (up + 1.0) * gate * (1.0 / (1.0 + jnp.exp(-SWIGLU_ALPHA * gate)))
        acc[pl.ds(r0, size), :] += jnp.dot(act.astype(bf16), wd_b[...], preferred_element_type=f32)

    first = j == 0
    last = j == nj - 1
    pair_rows = 2 * MOE_SUB
    assert MOE_TILES == (pair_rows, MOE_SUB, MOE_SUB // 2)
    padded = (rows + MOE_TILES[-1] - 1) // MOE_TILES[-1] * MOE_TILES[-1]

    @pl.when(active)
    def _():
        wgu_b[...] = wgu_ref[0].astype(bf16)
        wd_b[...] = wd_ref[0].astype(bf16)

    @pl.when(active & first)
    def _():
        for_rows(lambda i: in_copy(0, i, 0).start())

        @pl.when(nsub > 1)
        def _():
            for_rows(lambda i: in_copy(1, i, 1).start())

    def pair(pp, c):
        s0 = 2 * pp

        @pl.when(first)
        def _():
            for slot in (0, 1):
                s = s0 + slot

                @pl.when(s < nsub)
                def _(s=s, slot=slot):
                    wait_in(slot)
                    convert(s, slot)

                    @pl.when(s + 2 < nsub)
                    def _():
                        for_rows(lambda i: in_copy(s + 2, i, slot).start())

        r0 = pl.multiple_of(s0 * MOE_SUB, pair_rows)
        left = jnp.minimum(padded - r0, pair_rows)

        @pl.when(left == pair_rows)
        def _():
            tile(r0, pair_rows)

        @pl.when((left >= MOE_SUB) & (left < pair_rows))
        def _():
            tile(r0, MOE_SUB)

        @pl.when(left % MOE_SUB != 0)
        def _():
            tile(pl.multiple_of(r0 + left // MOE_SUB * MOE_SUB, MOE_TILES[-1]), MOE_TILES[-1])

        @pl.when(last)
        def _():
            for slot in (0, 1):
                s = s0 + slot

                @pl.when(s < nsub)
                def _(s=s, slot=slot):
                    @pl.when(pp > 0)
                    def _():
                        wait_out(slot)

                    stage(s, slot)
                    for_rows(lambda i: out_copy(s, i, slot).start())

        return c

    @pl.when(active)
    def _():
        lax.fori_loop(0, npairs, pair, 0)

    @pl.when(active & last)
    def _():
        wait_out(0)

        @pl.when(nsub >= 2)
        def _():
            wait_out(1)


def _deinterleave_matrix():
    pm = np.zeros((256, 256), np.float32)
    i = np.arange(LANES)
    pm[2 * i, i] = 1.0
    pm[2 * i + 1, LANES + i] = 1.0
    return jnp.asarray(pm, bf16)


def _moe(hq, eid, w_gate_up, b_gate_up, w_down, b_down, n):
    n_exp, d, f2 = w_gate_up.shape
    dff = f2 // 2
    nq = d // LANES
    nj = dff // MOE_TF
    assert nj >= 2, "the first and last hidden chunks carry the row gather and scatter"
    g_exp, g_rows, tok, dst = _moe_plan(eid, n_exp, MOE_R)
    n_groups = g_exp.shape[0]

    def jeff(g, j, gr):
        return jnp.where(gr[g] > 0, j, nj - 1)

    return pl.pallas_call(
        functools.partial(_moe_kernel, nq=nq, nj=nj, spare_row0=TOP_K * n * TOK_PITCH),
        grid_spec=pltpu.PrefetchScalarGridSpec(
            num_scalar_prefetch=2,
            grid=(n_groups, nj),
            in_specs=[
                pl.BlockSpec((1, 1, MOE_R), lambda g, j, ge, gr: (g, 0, 0), memory_space=pltpu.SMEM),
                pl.BlockSpec((1, 1, MOE_R), lambda g, j, ge, gr: (g, 0, 0), memory_space=pltpu.SMEM),
                pl.BlockSpec(memory_space=pl.ANY),
                pl.BlockSpec((1, d, 2 * MOE_TF), lambda g, j, ge, gr: (ge[g], 0, jeff(g, j, gr))),
                pl.BlockSpec((1, MOE_TF, d), lambda g, j, ge, gr: (ge[g], jeff(g, j, gr), 0)),
                pl.BlockSpec((1, 1, 2 * MOE_TF), lambda g, j, ge, gr: (ge[g], 0, jeff(g, j, gr))),
                pl.BlockSpec((1, 1, d), lambda g, j, ge, gr: (ge[g], 0, 0)),
                pl.BlockSpec((256, 256), lambda g, j, ge, gr: (0, 0)),
            ],
            out_specs=pl.BlockSpec(memory_space=pl.ANY),
            scratch_shapes=[
                pltpu.VMEM((2, MOE_SUB * TOK_PITCH, LANES), f32),
                pltpu.VMEM((MOE_R, d), bf16),
                pltpu.VMEM((MOE_R, d), f32),
                pltpu.VMEM((d, 2 * MOE_TF), bf16),
                pltpu.VMEM((MOE_TF, d), bf16),
                pltpu.SemaphoreType.DMA((2,)),
                pltpu.SemaphoreType.DMA((2,)),
            ],
        ),
        out_shape=jax.ShapeDtypeStruct(((TOP_K * n + MOE_SUB) * TOK_PITCH, LANES), f32),
        compiler_params=_cparams(("arbitrary", "arbitrary")),
        name="moe",
    )(g_exp, g_rows, tok, dst, hq, w_gate_up, w_down, b_gate_up.reshape(n_exp, 1, f2),
      b_down.reshape(n_exp, 1, d), _deinterleave_matrix())


def _final_kernel(h_ref, y0_ref, y1_ref, y2_ref, y3_ref, gate_ref, g2_ref, b2_ref, op_ref, os_ref, *, tm, nq, alpha, nbp):
    gates = gate_ref[...]
    f = jnp.zeros(h_ref.shape, f32)
    for k, y_ref in enumerate((y0_ref, y1_ref, y2_ref, y3_ref)):
        yk = jnp.concatenate([y_ref[pl.ds(jj, tm, stride=TOK_PITCH), :] for jj in range(nq)], axis=1)
        f = f + gates[:, k:k + 1] * yk
    z = alpha * h_ref[...] + f
    mu = jnp.mean(z, axis=-1, keepdims=True)
    var = jnp.mean(jnp.square(z - mu), axis=-1, keepdims=True)
    y = (z - mu) * lax.rsqrt(var + LN_EPS) * g2_ref[...] + b2_ref[...]
    i = pl.program_id(0)

    @pl.when(i < nbp)
    def _():
        op_ref[...] = y

    @pl.when(i >= nbp)
    def _():
        os_ref[...] = y


def _final(h, y4q, gates, ln2_g, ln2_b, alpha, n_prompt):
    n, d = h.shape
    nq = d // LANES
    tm = _pick(np.gcd(n_prompt, n - n_prompt), (256, 128, 64, 32, 16, 8))
    nb, nbp = n // tm, n_prompt // tm
    row = pl.BlockSpec((1, d), lambda i: (0, 0))
    yspec = lambda k: pl.BlockSpec((tm * TOK_PITCH, LANES), lambda i, k=k: (k * nb + i, 0))
    return pl.pallas_call(
        functools.partial(_final_kernel, tm=tm, nq=nq, alpha=alpha, nbp=nbp),
        grid=(nb,),
        in_specs=[pl.BlockSpec((tm, d), lambda i: (i, 0)), yspec(0), yspec(1), yspec(2), yspec(3),
                  pl.BlockSpec((tm, TOP_K), lambda i: (i, 0)), row, row],
        out_specs=[pl.BlockSpec((tm, d), lambda i: (jnp.minimum(i, nbp - 1), 0)),
                   pl.BlockSpec((tm, d), lambda i: (jnp.maximum(i - nbp, 0), 0))],
        out_shape=[jax.ShapeDtypeStruct((n_prompt, d), f32), jax.ShapeDtypeStruct((n - n_prompt, d), f32)],
        compiler_params=_cparams(("arbitrary",)),
        name="final",
    )(h, y4q, y4q, y4q, y4q, gates, ln2_g.reshape(1, d), ln2_b.reshape(1, d))


def kernel(x_prompt, x_sample, cache_k, cache_v, cache_idx_k, state_ret, page_table, w_in, w_o, ret_gn_w,
           ln1_g, ln1_b, w_router, b_router, w_gate_up, b_gate_up, w_down, b_down, ln2_g, ln2_b):
    depth = w_in.shape[0]
    assert depth == 1, "single-layer step"
    bp, t, d = x_prompt.shape
    bd, tn, _ = x_sample.shape
    assert bp == 1
    past = page_table.shape[1] * PAGE_SIZE
    np_, ns = bp * t, bd * tn
    n = np_ + ns
    alpha = (2 * depth) ** 0.25
    wkv = N_KV_A * HEAD_DIM_A
    layer = lambda a: a.reshape(a.shape[1:])

    x_all = jnp.concatenate([x_prompt.reshape(np_, d), x_sample.reshape(ns, d)], axis=0)
    pos_all = jnp.concatenate([jnp.arange(t), jnp.tile(past + jnp.arange(tn), bd)])
    u_f, u_b = _project(x_all, pos_all, layer(w_in))

    attn_p = _dsa_prompt(u_f, u_b, t)
    pages = lambda a: a.reshape(a.shape[1], PAGE_SIZE * N_KV_A, HEAD_DIM_A)
    attn_s = _dsa_sample(u_f, np_, bd, tn, pages(cache_k), pages(cache_v), jnp.swapaxes(layer(cache_idx_k), 1, 2),
                         page_table)

    zero_state = jnp.zeros((bp, N_HEADS_B, QK_DIM_B, V_DIM_B), f32)
    rb_p, s_p = _retention(u_f, 0, bp, t, zero_state, layer(ret_gn_w), bf16)
    rb_s, s_s = _retention(u_f, np_, bd, tn, layer(state_ret), layer(ret_gn_w), f32)

    attn_all = jnp.concatenate([attn_p, attn_s.astype(bf16)], axis=0)
    rb_all = jnp.concatenate([rb_p, rb_s.astype(bf16)], axis=0)
    h, hq, eid, gates = _tail1(attn_all, rb_all, x_all, layer(w_o), layer(ln1_g), layer(ln1_b), layer(w_router),
                               layer(b_router), alpha)
    y4q = _moe(hq, eid, layer(w_gate_up), layer(b_gate_up), layer(w_down), layer(b_down), n)
    y_p, y_s = _final(h, y4q, gates, layer(ln2_g), layer(ln2_b), alpha, np_)

    kv = lambda rows, c0, lead: u_f[rows, c0:c0 + wkv].reshape(lead + (N_KV_A, HEAD_DIM_A))[None]
    ps, ss = slice(0, np_), slice(np_, n)
    return (
        y_p.reshape(bp, t, d), y_s.reshape(bd, tn, d),
        kv(ps, C_KA, (bp, t)), kv(ps, C_VA, (bp, t)), u_f[ps, C_TAIL:C_TAIL + IDX_DIM].reshape(1, bp, t, IDX_DIM),
        s_p[None],
        kv(ss, C_KA, (bd, tn)), kv(ss, C_VA, (bd, tn)), u_f[ss, C_TAIL:C_TAIL + IDX_DIM].reshape(1, bd, tn, IDX_DIM),
        s_s[None],
    )
```

```python
import functools

import numpy as np
import jax
import jax.numpy as jnp
from jax import lax
from jax.experimental import pallas as pl
from jax.experimental.pallas import tpu as pltpu

f32 = jnp.float32
bf16 = jnp.bfloat16
i32 = jnp.int32

PAGE_SIZE = 128
HEAD_DIM_A = 128
N_HEADS_A = 8
N_KV_A = 4
ROPE_DIM_A = 32
ROPE_THETA = 500000.0
IDX_HEADS = 16
IDX_DIM = 64
IDX_ROPE_DIM = 16
TOPK_MAX = 256
V_DIM_B = 128
N_HEADS_B = 8
QK_DIM_B = 64
RET_CHUNK = 128
RET_THETA = 10000.0
TOP_K = 4
SWIGLU_LIMIT = 7.0
SWIGLU_ALPHA = 1.702
LN_EPS = 1e-5

LANES = 128
TOK_PITCH = 24
VMEM_LIMIT = 56 * 1024 * 1024

PROJ_TN = 512
C_QA, C_KA, C_VA, C_IQ, C_QB, C_KB, C_VB, C_GB, C_TAIL, PROJ_W = 0, 1024, 1536, 2048, 3072, 3584, 4096, 5120, 6144, 6656
PROJ_TILE_TYPES = (6, 6, 1, 0, 2, 2, 3, 4, 0, 0, 0, 0, 5)
QA_SCALE = HEAD_DIM_A ** -0.5 * 1.4426950408889634

INT_MIN = -2 ** 31
KEY_NEG_INF = -2139095041
NEG_BIG = -1e30

NT_DIMS = (((1,), (1,)), ((), ()))
TN_DIMS = (((0,), (0,)), ((), ()))


def _pick(n, cands):
    for c in cands:
        if n % c == 0:
            return c
    raise ValueError(f"no tile for {n}")


def _cparams(sem, vmem=VMEM_LIMIT):
    return pltpu.CompilerParams(dimension_semantics=sem, vmem_limit_bytes=vmem)


def _rope_table(pos, rot_dim, theta, period, scale=1.0, active=LANES):
    half = rot_dim // 2
    inv_freq = 1.0 / (theta ** (jnp.arange(half, dtype=f32) / half))
    ang = pos.astype(f32)[:, None] * inv_freq[None, :]
    cos, sin = jnp.cos(ang), jnp.sin(ang)
    lane = np.arange(LANES)
    d = lane % period
    first = (d < half) & (lane < active)
    second = (d >= half) & (d < rot_dim) & (lane < active)
    idx = np.where(first, d, np.where(second, d - half, 0))
    cos_l, sin_l = cos[:, idx], sin[:, idx]
    c = jnp.where(first | second, cos_l, 1.0)
    s1 = jnp.where(second, sin_l, 0.0)
    s2 = jnp.where(first, -sin_l, 0.0)
    return jnp.concatenate([c, s1, s2], axis=1) * scale


def _proj_kernel(tt_ref, x_ref, w_ref, tab_ref, of_ref, ob_ref, xb_scr):
    j = pl.program_id(1)

    @pl.when(j == 0)
    def _():
        xb_scr[...] = x_ref[...].astype(bf16)

    t = tt_ref[j]

    def emit(half):
        piece = 2 * LANES
        for c0 in range(0, PROJ_TN, piece):
            u = jnp.dot(xb_scr[...], w_ref[:, c0:c0 + piece], preferred_element_type=f32)
            if half is not None:
                c = tab_ref[0, :, 0:LANES]
                s1 = tab_ref[0, :, LANES:2 * LANES]
                s2 = tab_ref[0, :, 2 * LANES:3 * LANES]
                outs = []
                for q in range(piece // LANES):
                    uc = u[:, q * LANES:(q + 1) * LANES]
                    outs.append(uc * c + pltpu.roll(uc, half, 1) * s1 + pltpu.roll(uc, LANES - half, 1) * s2)
                u = jnp.concatenate(outs, axis=1)
            of_ref[:, c0:c0 + piece] = u
            ob_ref[:, c0:c0 + piece] = u.astype(bf16)

    @pl.when(t == 0)
    def _():
        emit(None)

    @pl.when((t == 1) | (t == 6))
    def _():
        emit(ROPE_DIM_A // 2)

    @pl.when((t == 2) | (t == 5))
    def _():
        emit(IDX_ROPE_DIM // 2)

    @pl.when((t == 3) | (t == 4))
    def _():
        emit(QK_DIM_B // 2)


def _project(x_all, pos_all, w_in):
    n, d = x_all.shape
    tm = _pick(n, (768, 512, 384, 256, 128, 64, 32, 16, 8))
    o = np.cumsum((0, 1024, 512, 512, 1024, 64, 16, 512, 512, 1024, 1024))
    wp = jnp.concatenate([w_in[:, o[0]:o[4]], w_in[:, o[6]:o[10]], w_in[:, o[4]:o[6]],
                          jnp.zeros((d, PROJ_W - C_TAIL - 80), w_in.dtype)], axis=1).astype(bf16)
    tabs = jnp.stack([
        _rope_table(pos_all, ROPE_DIM_A, ROPE_THETA, HEAD_DIM_A),
        _rope_table(pos_all, ROPE_DIM_A, ROPE_THETA, HEAD_DIM_A),
        _rope_table(pos_all, IDX_ROPE_DIM, ROPE_THETA, IDX_DIM),
        _rope_table(pos_all, QK_DIM_B, RET_THETA, QK_DIM_B),
        _rope_table(pos_all, QK_DIM_B, RET_THETA, QK_DIM_B, scale=QK_DIM_B ** -0.5),
        _rope_table(pos_all, IDX_ROPE_DIM, ROPE_THETA, IDX_DIM, active=IDX_DIM),
        _rope_table(pos_all, ROPE_DIM_A, ROPE_THETA, HEAD_DIM_A, scale=QA_SCALE),
    ])
    tt = jnp.asarray(PROJ_TILE_TYPES, i32)
    nj = PROJ_W // PROJ_TN
    return pl.pallas_call(
        _proj_kernel,
        grid_spec=pltpu.PrefetchScalarGridSpec(
            num_scalar_prefetch=1,
            grid=(n // tm, nj),
            in_specs=[
                pl.BlockSpec((tm, d), lambda i, j, tt: (i, 0)),
                pl.BlockSpec((d, PROJ_TN), lambda i, j, tt: (0, j)),
                pl.BlockSpec((1, tm, 3 * LANES), lambda i, j, tt: (tt[j], i, 0)),
            ],
            out_specs=[
                pl.BlockSpec((tm, PROJ_TN), lambda i, j, tt: (i, j)),
                pl.BlockSpec((tm, PROJ_TN), lambda i, j, tt: (i, j)),
            ],
            scratch_shapes=[pltpu.VMEM((tm, d), bf16)],
        ),
        out_shape=[jax.ShapeDtypeStruct((n, PROJ_W), f32), jax.ShapeDtypeStruct((n, PROJ_W), bf16)],
        compiler_params=_cparams(("arbitrary", "arbitrary")),
        name="proj",
    )(tt, x_all, wp, tabs)


def _key_to_float(key):
    return pltpu.bitcast(key ^ ((key >> 31) & 0x7FFFFFFF), f32)


def _kth_threshold(count_ge, shape, k):
    def body(step, ans):
        cand = ans + jnp.left_shift(jnp.int32(1), 31 - step)
        return jnp.where(count_ge(_key_to_float(cand)) >= k, cand, ans)

    ans = lax.fori_loop(0, 32, body, jnp.full(shape, INT_MIN, i32))
    return _key_to_float(jnp.maximum(ans, KEY_NEG_INF + 1))


def _lane_blocks(x):
    return [x[:, i * LANES:(i + 1) * LANES] for i in range(x.shape[1] // LANES)]


def _flash_update(qs, ks, vs, bias, m_scr, l_scr, acc_scr, batched):
    n = len(qs)
    score = lambda g: lax.dot_general(qs[g], ks[g], NT_DIMS, preferred_element_type=f32) + bias

    def update(g, sm):
        m_old = m_scr[g]
        m_new = jnp.maximum(m_old, jnp.max(functools.reduce(jnp.maximum, _lane_blocks(sm)), axis=1, keepdims=True))
        alpha = jnp.exp2(m_old - m_new)
        p = jnp.exp2(sm - m_new)
        l_scr[g] = alpha * l_scr[g] + jnp.sum(functools.reduce(jnp.add, _lane_blocks(p)), axis=1, keepdims=True)
        m_scr[g] = m_new
        acc_scr[g] = alpha * acc_scr[g] + jnp.dot(p.astype(bf16), vs[g], preferred_element_type=f32)

    if batched:
        sms = [score(g) for g in range(n)]
        for g in range(n):
            update(g, sms[g])
    else:
        for g in range(n):
            update(g, score(g))


DSA_GROUP_BATCH = 4


def _dsa_prompt_kernel(iq_ref, iw_ref, qa_ref, ikd_ref, ka_ref, vat_ref, o_ref,
                       qst, qgt, sct, m_scr, l_scr, acc_scr, *, tq, tk, topk):
    i = pl.program_id(0)
    n_chunks = (i * tq + tq + tk - 1) // tk
    row = lax.broadcasted_iota(i32, (LANES, tq), 0)

    for p in range(IDX_HEADS // 2):
        blk = iq_ref[:, p * LANES:(p + 1) * LANES].T
        qst[:, (2 * p) * tq:(2 * p + 1) * tq] = jnp.where(row < IDX_DIM, blk, 0.0).astype(bf16)
        qst[:, (2 * p + 1) * tq:(2 * p + 2) * tq] = jnp.where(row >= IDX_DIM, blk, 0.0).astype(bf16)
    wt = iw_ref[...].T * (IDX_DIM ** -0.5 * IDX_HEADS ** -0.5)
    for h in range(N_HEADS_A):
        qgt[h // 2, :, (h % 2) * tq:(h % 2 + 1) * tq] = qa_ref[:, h * LANES:(h + 1) * LANES].T.astype(bf16)

    qpos = i * tq + lax.broadcasted_iota(i32, (tk, tq), 1)

    def score_body(c, carry):
        k0 = pl.multiple_of(c * tk, tk)
        logits = jnp.dot(ikd_ref[pl.ds(k0, tk), :], qst[...], preferred_element_type=f32)
        acc = jnp.zeros((tk, tq), f32)
        for h in range(IDX_HEADS):
            acc = acc + jnp.maximum(logits[:, h * tq:(h + 1) * tq], 0.0) * wt[IDX_DIM + h:IDX_DIM + h + 1, :]
        kpos = k0 + lax.broadcasted_iota(i32, (tk, tq), 0)
        sct[c] = jnp.where(kpos <= qpos, acc, -jnp.inf)
        return carry

    lax.fori_loop(0, n_chunks, score_body, 0)

    def count_ge(cand):
        cb = jnp.broadcast_to(cand, (8, tq))

        def body(c, accs):
            accs = list(accs)
            for r in range(tk // 8):
                accs[r % 4] = accs[r % 4] + jnp.where(sct[c, r * 8:(r + 1) * 8, :] >= cb, 1.0, 0.0)
            return tuple(accs)

        accs = lax.fori_loop(0, n_chunks, body, (jnp.zeros((8, tq), f32),) * 4)
        return jnp.sum((accs[0] + accs[1]) + (accs[2] + accs[3]), axis=0, keepdims=True)

    thr = _kth_threshold(count_ge, (1, tq), float(topk))

    m_scr[...] = jnp.full(m_scr.shape, NEG_BIG, f32)
    l_scr[...] = jnp.zeros(l_scr.shape, f32)
    acc_scr[...] = jnp.zeros(acc_scr.shape, f32)

    def att_body(c, carry):
        k0 = pl.multiple_of(c * tk, tk)
        bias1 = jnp.where(sct[c] >= thr, 0.0, NEG_BIG)
        bias = jnp.concatenate([bias1, bias1], axis=1)
        for g0 in range(0, N_KV_A, DSA_GROUP_BATCH):
            groups = range(g0, g0 + DSA_GROUP_BATCH)
            sms = {g: jnp.dot(ka_ref[pl.ds(k0, tk), g * LANES:(g + 1) * LANES], qgt[g],
                              preferred_element_type=f32) + bias for g in groups}
            m_new = {g: jnp.maximum(m_scr[g], jnp.max(sms[g], axis=0, keepdims=True)) for g in groups}
            alpha = {g: jnp.exp2(m_scr[g] - m_new[g]) for g in groups}
            ps = {g: jnp.exp2(sms[g] - m_new[g]) for g in groups}
            for g in groups:
                l_scr[g] = alpha[g] * l_scr[g] + jnp.sum(ps[g], axis=0, keepdims=True)
                m_scr[g] = m_new[g]
            for g in groups:
                acc_scr[g] = alpha[g] * acc_scr[g] + jnp.dot(vat_ref[c, g * LANES:(g + 1) * LANES, :],
                                                             ps[g].astype(bf16), preferred_element_type=f32)
        return carry

    lax.fori_loop(0, n_chunks, att_body, 0)

    for g in range(N_KV_A):
        o = acc_scr[g] / l_scr[g]
        o_ref[:, (2 * g) * LANES:(2 * g + 1) * LANES] = o[:, 0:tq].T.astype(o_ref.dtype)
        o_ref[:, (2 * g + 1) * LANES:(2 * g + 2) * LANES] = o[:, tq:2 * tq].T.astype(o_ref.dtype)


def _dsa_prompt(u_f, u_b, t):
    tq = _pick(t, (128,))
    tk = _pick(t, (512, 256, 128))
    topk = min(TOPK_MAX, t // 4)
    wa = N_HEADS_A * HEAD_DIM_A
    wkv = N_KV_A * HEAD_DIM_A
    ik_b = u_b[:t, C_TAIL:C_TAIL + IDX_DIM]
    ikd = jnp.concatenate([ik_b, ik_b], axis=1)
    vat = u_b[:t, C_VA:C_VA + wkv].reshape(t // tk, tk, wkv).transpose(0, 2, 1)
    kern = functools.partial(_dsa_prompt_kernel, tq=tq, tk=tk, topk=topk)
    one = pl.Buffered(1)
    return pl.pallas_call(
        kern,
        grid=(t // tq,),
        in_specs=[
            pl.BlockSpec((tq, IDX_HEADS * IDX_DIM), lambda i: (i, C_IQ // (IDX_HEADS * IDX_DIM))),
            pl.BlockSpec((tq, LANES), lambda i: (i, C_TAIL // LANES)),
            pl.BlockSpec((tq, wa), lambda i: (i, C_QA // wa)),
            pl.BlockSpec((t, LANES), lambda i: (0, 0), pipeline_mode=one),
            pl.BlockSpec((t, wkv), lambda i: (0, C_KA // wkv), pipeline_mode=one),
            pl.BlockSpec((t // tk, wkv, tk), lambda i: (0, 0, 0), pipeline_mode=one),
        ],
        out_specs=pl.BlockSpec((tq, wa), lambda i: (i, 0)),
        out_shape=jax.ShapeDtypeStruct((t, wa), bf16),
        scratch_shapes=[
            pltpu.VMEM((LANES, IDX_HEADS * tq), bf16),
            pltpu.VMEM((N_KV_A, LANES, 2 * tq), bf16),
            pltpu.VMEM((t // tk, tk, tq), f32),
            pltpu.VMEM((N_KV_A, 1, 2 * tq), f32),
            pltpu.VMEM((N_KV_A, 1, 2 * tq), f32),
            pltpu.VMEM((N_KV_A, LANES, 2 * tq), f32),
        ],
        compiler_params=_cparams(("arbitrary",)),
        name="dsa_prompt",
    )(u_f, u_f, u_f, ikd, u_b, vat)


DSA_SCORE_CHUNK_PAGES = 32
DSA_PAGE_ISSUE_UNROLL = 8


def _dsa_sample_score_kernel(pt_ref, idx_hbm, iq_ref, tail_ref, sc_ref, buf, qs, wst, sem, *, tn, n_pages):
    b = pl.program_id(0)
    nb = pl.num_programs(0)
    cpp = DSA_SCORE_CHUNK_PAGES
    n_chunks = n_pages // cpp
    kw = cpp * PAGE_SIZE
    wscale = IDX_DIM ** -0.5 * IDX_HEADS ** -0.5

    def issue(seq, slot):
        def body(blk, c):
            for u in range(DSA_PAGE_ISSUE_UNROLL):
                p = blk * DSA_PAGE_ISSUE_UNROLL + u
                pltpu.make_async_copy(idx_hbm.at[pt_ref[seq, p]], buf.at[slot, p], sem.at[slot]).start()
            return c

        lax.fori_loop(0, n_pages // DSA_PAGE_ISSUE_UNROLL, body, 0)

    def wait(slot):
        pltpu.make_async_copy(idx_hbm.at[pl.ds(0, n_pages)], buf.at[slot], sem.at[slot]).wait()

    @pl.when(b == 0)
    def _():
        issue(0, 0)

    iq = iq_ref[...]
    w = tail_ref[...]
    for h in range(IDX_HEADS):
        qs[h * tn:(h + 1) * tn, :] = iq[:, h * IDX_DIM:(h + 1) * IDX_DIM]
        wst[h * tn:(h + 1) * tn, :] = jnp.broadcast_to(w[:, IDX_DIM + h:IDX_DIM + h + 1] * wscale, (tn, LANES))
    qsb = qs[...].astype(bf16)

    def head_sum(logits):
        width = logits.shape[1]
        acc = jnp.zeros((tn, width), f32)
        for h in range(IDX_HEADS):
            wh = wst[h * tn:(h + 1) * tn, :]
            acc = acc + jnp.maximum(logits[h * tn:(h + 1) * tn, :], 0.0) * jnp.concatenate([wh] * (width // LANES), axis=1)
        return acc

    def past_scores(slot):
        @pl.when(b + 1 < nb)
        def _():
            issue(b + 1, 1 - slot)

        wait(slot)
        for c in range(n_chunks):
            keys_t = jnp.concatenate([buf[slot, c * cpp + p] for p in range(cpp)], axis=1).astype(bf16)
            sc_ref[0, c] = head_sum(jnp.dot(qsb, keys_t, preferred_element_type=f32))

    @pl.when(b % 2 == 0)
    def _():
        past_scores(0)

    @pl.when(b % 2 == 1)
    def _():
        past_scores(1)

    ik_new = tail_ref[:, 0:IDX_DIM].astype(bf16)
    kpad = jnp.concatenate([ik_new, jnp.zeros((LANES - tn, IDX_DIM), bf16)], axis=0)
    s_new = head_sum(lax.dot_general(qsb, kpad, NT_DIMS, preferred_element_type=f32))
    qi = lax.broadcasted_iota(i32, (tn, LANES), 0)
    kj = lax.broadcasted_iota(i32, (tn, LANES), 1)
    snew = jnp.where(kj <= qi, s_new, -jnp.inf)
    sc_ref[0, n_chunks] = jnp.concatenate([snew, jnp.full((tn, kw - LANES), -jnp.inf, f32)], axis=1)


def _dsa_sample_thr_kernel(sc_ref, thr_ref, *, ns, nc, tn, kw, topk):
    def count_ge(cand):
        out = []
        for s in range(ns):
            cb = jnp.broadcast_to(cand[s], (tn, LANES))
            parts = [jnp.zeros((tn, LANES), f32)] * 4
            for c in range(nc):
                blk = sc_ref[s, c]
                for q in range(kw // LANES):
                    parts[q % 4] = parts[q % 4] + jnp.where(blk[:, q * LANES:(q + 1) * LANES] >= cb, 1.0, 0.0)
            out.append(jnp.sum((parts[0] + parts[1]) + (parts[2] + parts[3]), axis=1, keepdims=True))
        return jnp.stack(out)

    thr = _kth_threshold(count_ge, (ns, tn, 1), float(topk))
    thr_ref[...] = jnp.broadcast_to(thr, (ns, tn, LANES))


def _dsa_sample_attn_kernel(pt_ref, *refs, tn, ppg, n_groups):
    kpages = refs[:ppg]
    vpages = refs[ppg:2 * ppg]
    qa_ref, kn_ref, vn_ref, sc_ref, scn_ref, thr_ref, o_ref, qg, m_scr, l_scr, acc_scr = refs[2 * ppg:]
    c = pl.program_id(1)

    @pl.when(c == 0)
    def _():
        qa = qa_ref[...]
        for g in range(N_KV_A):
            qg[g, 0:tn, :] = qa[:, (2 * g) * LANES:(2 * g + 1) * LANES]
            qg[g, tn:2 * tn, :] = qa[:, (2 * g + 1) * LANES:(2 * g + 2) * LANES]
        m_scr[...] = jnp.full(m_scr.shape, NEG_BIG, f32)
        l_scr[...] = jnp.zeros(l_scr.shape, f32)
        acc_scr[...] = jnp.zeros(acc_scr.shape, f32)

    thr = thr_ref[0][:, 0:1]
    thr2 = jnp.concatenate([thr, thr], axis=0)

    def attend(ks, vs, sc):
        bias = jnp.where(jnp.concatenate([sc, sc], axis=0) >= thr2, 0.0, NEG_BIG)
        _flash_update([qg[g].astype(bf16) for g in range(N_KV_A)], ks, vs, bias, m_scr, l_scr, acc_scr, batched=True)

    group = lambda pages, g: jnp.concatenate(
        [p[pl.ds(g, PAGE_SIZE, stride=N_KV_A), :] for p in pages], axis=0).astype(bf16)
    attend([group(kpages, g) for g in range(N_KV_A)], [group(vpages, g) for g in range(N_KV_A)], sc_ref[0, 0])

    @pl.when(c == n_groups - 1)
    def _():
        zpad = jnp.zeros((LANES - tn, N_KV_A * HEAD_DIM_A), bf16)
        attend(_lane_blocks(jnp.concatenate([kn_ref[...].astype(bf16), zpad], axis=0)),
               _lane_blocks(jnp.concatenate([vn_ref[...].astype(bf16), zpad], axis=0)),
               scn_ref[0, 0][:, 0:LANES])
        for g in range(N_KV_A):
            o = acc_scr[g] / l_scr[g]
            o_ref[:, (2 * g) * LANES:(2 * g + 1) * LANES] = o[0:tn]
            o_ref[:, (2 * g + 1) * LANES:(2 * g + 2) * LANES] = o[tn:2 * tn]


def _dsa_sample(u_f, row0, bd, tn, cache_k, cache_v, idx_kt, page_table):
    n_pages = page_table.shape[1]
    past = n_pages * PAGE_SIZE
    topk = min(TOPK_MAX, (past + tn) // 4)
    ppa = _pick(n_pages, (16, 8, 4, 2, 1))
    pps = _pick(n_pages, (DSA_SCORE_CHUNK_PAGES,))
    nga, ngs = n_pages // ppa, n_pages // pps
    kwa, kws = ppa * PAGE_SIZE, pps * PAGE_SIZE
    per = kws // kwa
    wkv = N_KV_A * HEAD_DIM_A
    wa = N_HEADS_A * HEAD_DIM_A
    rb0 = row0 // tn

    def kv_spec(j):
        return pl.BlockSpec((None, PAGE_SIZE * N_KV_A, HEAD_DIM_A), lambda b, c, pt, j=j: (pt[b, c * ppa + j], 0, 0))

    score = pl.pallas_call(
        functools.partial(_dsa_sample_score_kernel, tn=tn, n_pages=n_pages),
        grid_spec=pltpu.PrefetchScalarGridSpec(
            num_scalar_prefetch=1,
            grid=(bd,),
            in_specs=[
                pl.BlockSpec(memory_space=pl.ANY),
                pl.BlockSpec((tn, IDX_HEADS * IDX_DIM), lambda b, pt: (rb0 + b, C_IQ // (IDX_HEADS * IDX_DIM))),
                pl.BlockSpec((tn, LANES), lambda b, pt: (rb0 + b, C_TAIL // LANES)),
            ],
            out_specs=pl.BlockSpec((1, ngs + 1, tn, kws), lambda b, pt: (b, 0, 0, 0)),
            scratch_shapes=[
                pltpu.VMEM((2, n_pages, IDX_DIM, PAGE_SIZE), f32),
                pltpu.VMEM((IDX_HEADS * tn, IDX_DIM), f32),
                pltpu.VMEM((IDX_HEADS * tn, LANES), f32),
                pltpu.SemaphoreType.DMA((2,)),
            ],
        ),
        out_shape=jax.ShapeDtypeStruct((bd, ngs + 1, tn, kws), f32),
        compiler_params=_cparams(("arbitrary",)),
        name="dsa_sample_score",
    )
    scores = score(page_table, idx_kt, u_f, u_f)

    nst = _pick(bd, (8, 4, 2, 1))
    thr = pl.pallas_call(
        functools.partial(_dsa_sample_thr_kernel, ns=nst, nc=ngs + 1, tn=tn, kw=kws, topk=topk),
        grid=(bd // nst,),
        in_specs=[pl.BlockSpec((nst, ngs + 1, tn, kws), lambda i: (i, 0, 0, 0))],
        out_specs=pl.BlockSpec((nst, tn, LANES), lambda i: (i, 0, 0)),
        out_shape=jax.ShapeDtypeStruct((bd, tn, LANES), f32),
        compiler_params=_cparams(("arbitrary",)),
        name="dsa_sample_thr",
    )(scores)

    attn = pl.pallas_call(
        functools.partial(_dsa_sample_attn_kernel, tn=tn, ppg=ppa, n_groups=nga),
        grid_spec=pltpu.PrefetchScalarGridSpec(
            num_scalar_prefetch=1,
            grid=(bd, nga),
            in_specs=[kv_spec(j) for j in range(ppa)] + [kv_spec(j) for j in range(ppa)] + [
                pl.BlockSpec((tn, wa), lambda b, c, pt: (rb0 + b, C_QA // wa)),
                pl.BlockSpec((tn, wkv), lambda b, c, pt: (rb0 + b, C_KA // wkv)),
                pl.BlockSpec((tn, wkv), lambda b, c, pt: (rb0 + b, C_VA // wkv)),
                pl.BlockSpec((1, 1, tn, kwa), lambda b, c, pt: (b, c // per, 0, c % per)),
                pl.BlockSpec((1, 1, tn, kwa), lambda b, c, pt: (b, ngs, 0, 0)),
                pl.BlockSpec((1, tn, LANES), lambda b, c, pt: (b, 0, 0)),
            ],
            out_specs=pl.BlockSpec((tn, wa), lambda b, c, pt: (b, 0)),
            scratch_shapes=[
                pltpu.VMEM((N_KV_A, 2 * tn, LANES), f32),
                pltpu.VMEM((N_KV_A, 2 * tn, 1), f32),
                pltpu.VMEM((N_KV_A, 2 * tn, 1), f32),
                pltpu.VMEM((N_KV_A, 2 * tn, LANES), f32),
            ],
        ),
        out_shape=jax.ShapeDtypeStruct((bd * tn, wa), f32),
        compiler_params=_cparams(("arbitrary", "arbitrary")),
        name="dsa_sample_attn",
    )
    return attn(page_table, *([cache_k] * ppa), *([cache_v] * ppa), u_f, u_f, u_f, scores, scores, thr)


def _ret_kernel(q_ref, k_ref, v_ref, g_ref, gnw_ref, dmask_ref, qdec_ref, kdec_ref, sdec_ref, s0_ref,
                o_ref, sout_ref, s_scr, *, n_chunks):
    c = pl.program_id(1)

    @pl.when(c == 0)
    def _():
        s_scr[...] = s0_ref[0]

    for h in range(N_HEADS_B):
        q = q_ref[:, h * QK_DIM_B:(h + 1) * QK_DIM_B]
        k = k_ref[:, h * QK_DIM_B:(h + 1) * QK_DIM_B]
        v = v_ref[:, h * V_DIM_B:(h + 1) * V_DIM_B].astype(bf16)
        gate = g_ref[:, h * V_DIM_B:(h + 1) * V_DIM_B]
        qb = q.astype(bf16)
        att = lax.dot_general(qb, k.astype(bf16), NT_DIMS, preferred_element_type=f32) * dmask_ref[h]
        s_old = s_scr[h]
        o = (jnp.dot(att.astype(bf16), v, preferred_element_type=f32)
             + jnp.dot(qb, s_old.astype(bf16), preferred_element_type=f32) * qdec_ref[h])
        kd = (k * kdec_ref[h]).astype(bf16)
        s_scr[h] = s_old * sdec_ref[h] + lax.dot_general(kd, v, TN_DIMS, preferred_element_type=f32)
        mu = jnp.mean(o, axis=-1, keepdims=True)
        var = jnp.mean(jnp.square(o - mu), axis=-1, keepdims=True)
        rb = (o - mu) * lax.rsqrt(var + LN_EPS) * gnw_ref[:, h * V_DIM_B:(h + 1) * V_DIM_B]
        rb = rb * (gate / (1.0 + jnp.exp(-gate)))
        o_ref[:, h * V_DIM_B:(h + 1) * V_DIM_B] = rb.astype(o_ref.dtype)

    @pl.when(c == n_chunks - 1)
    def _():
        sout_ref[0] = s_scr[...]


def _retention(u_f, row0, nb, t, state0, gn_w, out_dtype):
    ch = min(RET_CHUNK, t)
    if t % ch:
        ch = t
    n = t // ch
    hb = N_HEADS_B
    lg = jnp.log1p(-jnp.exp2(-5.0 - jnp.arange(hb, dtype=f32)))
    i = jnp.arange(ch)
    diff = i[:, None] - i[None, :]
    dmask = jnp.where(diff >= 0, jnp.exp(lg[:, None, None] * jnp.maximum(diff, 0)), 0.0)
    qdec = jnp.broadcast_to(jnp.exp(lg[:, None] * (i + 1))[:, :, None], (hb, ch, V_DIM_B))
    kdec = jnp.broadcast_to(jnp.exp(lg[:, None] * (ch - 1 - i))[:, :, None], (hb, ch, QK_DIM_B))
    sdec = jnp.broadcast_to(jnp.exp(lg * ch)[:, None, None], (hb, 1, V_DIM_B))
    wqk = hb * QK_DIM_B
    wv = hb * V_DIM_B
    rb0 = row0 // ch
    full3 = lambda shp: pl.BlockSpec(shp, lambda b, c: (0, 0, 0))
    return pl.pallas_call(
        functools.partial(_ret_kernel, n_chunks=n),
        grid=(nb, n),
        in_specs=[
            pl.BlockSpec((ch, wqk), lambda b, c: (rb0 + b * n + c, C_QB // wqk)),
            pl.BlockSpec((ch, wqk), lambda b, c: (rb0 + b * n + c, C_KB // wqk)),
            pl.BlockSpec((ch, wv), lambda b, c: (rb0 + b * n + c, C_VB // wv)),
            pl.BlockSpec((ch, wv), lambda b, c: (rb0 + b * n + c, C_GB // wv)),
            pl.BlockSpec((1, wv), lambda b, c: (0, 0)),
            full3((hb, ch, ch)), full3((hb, ch, V_DIM_B)), full3((hb, ch, QK_DIM_B)), full3((hb, 1, V_DIM_B)),
            pl.BlockSpec((1, hb, QK_DIM_B, V_DIM_B), lambda b, c: (b, 0, 0, 0)),
        ],
        out_specs=[
            pl.BlockSpec((ch, wv), lambda b, c: (b * n + c, 0)),
            pl.BlockSpec((1, hb, QK_DIM_B, V_DIM_B), lambda b, c: (b, 0, 0, 0)),
        ],
        out_shape=[jax.ShapeDtypeStruct((nb * t, wv), out_dtype),
                   jax.ShapeDtypeStruct((nb, hb, QK_DIM_B, V_DIM_B), f32)],
        scratch_shapes=[pltpu.VMEM((hb, QK_DIM_B, V_DIM_B), f32)],
        compiler_params=_cparams(("arbitrary", "arbitrary")),
        name="retention",
    )(u_f, u_f, u_f, u_f, gn_w.reshape(1, wv), dmask, qdec, kdec, sdec, state0)


def _split_hi_lo(a):
    hi = a.astype(bf16)
    return hi, (a - hi.astype(f32)).astype(bf16)


def _tail1_kernel(attn_p_ref, attn_s_ref, rb_p_ref, rb_s_ref, x_ref, wo_ref, g1_ref, b1_ref, wrh_ref, wrl_ref, br_ref,
                  h_ref, hq_ref, eid_ref, gate_ref, *, tm, n_exp, alpha, wa, nbp):
    is_prompt = pl.program_id(0) < nbp
    attn = jnp.where(is_prompt, attn_p_ref[...], attn_s_ref[...].astype(bf16))
    rb = jnp.where(is_prompt, rb_p_ref[...], rb_s_ref[...].astype(bf16))
    mix = (jnp.dot(attn, wo_ref[0:wa, :], preferred_element_type=f32)
           + jnp.dot(rb, wo_ref[wa:, :], preferred_element_type=f32))
    z = alpha * x_ref[...] + mix
    mu = jnp.mean(z, axis=-1, keepdims=True)
    var = jnp.mean(jnp.square(z - mu), axis=-1, keepdims=True)
    h = (z - mu) * lax.rsqrt(var + LN_EPS) * g1_ref[...] + b1_ref[...]
    h_ref[...] = h
    nq = h.shape[1] // LANES
    for j in range(TOK_PITCH):
        slab = h[:, j * LANES:(j + 1) * LANES] if j < nq else jnp.zeros((tm, LANES), f32)
        hq_ref[pl.ds(j, tm, stride=TOK_PITCH), :] = slab

    hh, hl = _split_hi_lo(h)
    logits = (jnp.dot(hh, wrh_ref[...], preferred_element_type=f32)
              + jnp.dot(hl, wrh_ref[...], preferred_element_type=f32)
              + jnp.dot(hh, wrl_ref[...], preferred_element_type=f32)) + br_ref[...]
    lane = lax.broadcasted_iota(i32, (tm, LANES), 1)
    lanef = lane.astype(f32)
    logits = jnp.where(lane < n_exp, logits, -jnp.inf)
    vals, ids = [], []
    for _ in range(TOP_K):
        m = jnp.max(logits, axis=1, keepdims=True)
        idx = jnp.min(jnp.where(logits == m, lanef, float(LANES)), axis=1, keepdims=True)
        vals.append(m)
        ids.append(idx)
        logits = jnp.where(lanef == idx, -jnp.inf, logits)
    es = [jnp.exp(v - vals[0]) for v in vals]
    den = es[0] + es[1] + es[2] + es[3]
    eid_ref[...] = jnp.concatenate(ids, axis=1).astype(i32)
    gate_ref[...] = jnp.concatenate([e / den for e in es], axis=1)


def _tail1(attn_p, attn_s, rb_p, rb_s, x_all, w_o, ln1_g, ln1_b, w_router, b_router, alpha):
    n, d = x_all.shape
    wa, wb = attn_p.shape[1], rb_p.shape[1]
    n_prompt = attn_p.shape[0]
    n_exp = w_router.shape[1]
    tm = _pick(np.gcd(n_prompt, n - n_prompt), (256, 128, 64, 32, 16, 8))
    nbp = n_prompt // tm
    prompt_rows = lambda w: pl.BlockSpec((tm, w), lambda i: (jnp.minimum(i, nbp - 1), 0))
    sample_rows = lambda w: pl.BlockSpec((tm, w), lambda i: (jnp.maximum(i - nbp, 0), 0))
    nq = d // LANES
    wr = jnp.zeros((d, LANES), f32).at[:, :n_exp].set(w_router)
    wrh, wrl = _split_hi_lo(wr)
    br = jnp.zeros((1, LANES), f32).at[0, :n_exp].set(b_router)
    row = lambda w: pl.BlockSpec((1, w), lambda i: (0, 0))
    return pl.pallas_call(
        functools.partial(_tail1_kernel, tm=tm, n_exp=n_exp, alpha=alpha, wa=wa, nbp=nbp),
        grid=(n // tm,),
        in_specs=[
            prompt_rows(wa), sample_rows(wa), prompt_rows(wb), sample_rows(wb),
            pl.BlockSpec((tm, d), lambda i: (i, 0)),
            pl.BlockSpec(w_o.shape, lambda i: (0, 0), pipeline_mode=pl.Buffered(1)),
            row(d), row(d),
            pl.BlockSpec((d, LANES), lambda i: (0, 0)), pl.BlockSpec((d, LANES), lambda i: (0, 0)), row(LANES),
        ],
        out_specs=[
            pl.BlockSpec((tm, d), lambda i: (i, 0)),
            pl.BlockSpec((tm * TOK_PITCH, LANES), lambda i: (i, 0)),
            pl.BlockSpec((tm, TOP_K), lambda i: (i, 0)),
            pl.BlockSpec((tm, TOP_K), lambda i: (i, 0)),
        ],
        out_shape=[jax.ShapeDtypeStruct((n, d), f32), jax.ShapeDtypeStruct((n * TOK_PITCH, LANES), f32),
                   jax.ShapeDtypeStruct((n, TOP_K), i32), jax.ShapeDtypeStruct((n, TOP_K), f32)],
        compiler_params=_cparams(("arbitrary",)),
        name="tail1",
    )(attn_p, attn_s, rb_p, rb_s, x_all, w_o.astype(bf16), ln1_g.reshape(1, d), ln1_b.reshape(1, d), wrh, wrl, br)


MOE_R = 2048
MOE_SUB = 256
MOE_TF = 256
MOE_ISSUE_UNROLL = 8
MOE_TILES = (512, 256, 128)


def _moe_plan(eid, n_exp, r_cap):
    n = eid.shape[0]
    p = n * TOP_K
    flat = eid.reshape(p)
    onehot = (flat[:, None] == jnp.arange(n_exp, dtype=i32)[None, :]).astype(i32)
    csum = jnp.cumsum(onehot, axis=0)
    rank = jnp.sum((csum - onehot) * onehot, axis=1)
    counts = csum[-1]
    ngrp = (counts + r_cap - 1) // r_cap
    gend = jnp.cumsum(ngrp)
    gstart = gend - ngrp
    g_of = gstart[flat] + rank // r_cap
    slot = rank % r_cap
    n_groups = n_exp + p // r_cap
    pair = jnp.full((n_groups, r_cap), -1, i32).at[g_of, slot].set(jnp.arange(p, dtype=i32))
    t_idx, k_idx = pair // TOP_K, pair % TOP_K
    spare = jnp.broadcast_to(p + jnp.arange(r_cap, dtype=i32) % MOE_SUB, (n_groups, r_cap))
    tok = jnp.where(pair >= 0, t_idx, 0)
    dst = jnp.where(pair >= 0, k_idx * n + t_idx, spare)
    gid = jnp.arange(n_groups, dtype=i32)
    total = gend[-1]
    gclamp = jnp.minimum(gid, total - 1)
    g_exp = jnp.sum((gend[None, :] <= gclamp[:, None]).astype(i32), axis=1)
    g_rows = jnp.clip(counts[g_exp] - (gclamp - gstart[g_exp]) * r_cap, 0, r_cap)
    g_rows = jnp.where(gid < total, g_rows, 0).astype(i32)
    return g_exp, g_rows, tok.reshape(n_groups, 1, r_cap), dst.reshape(n_groups, 1, r_cap)


def _moe_kernel(ge_ref, gr_ref, tok_ref, dst_ref, hq_ref, wgu_ref, wd_ref, bgu_ref, bd_ref, pm_ref, y_ref,
                qbuf, xb, acc, wgu_b, wd_b, sem_in, sem_out, *, nq, nj, spare_row0):
    pitch = TOK_PITCH
    g = pl.program_id(0)
    j = pl.program_id(1)
    rows = gr_ref[g]
    nsub = (rows + MOE_SUB - 1) // MOE_SUB
    active = rows > 0

    def in_copy(s, i, slot):
        t = tok_ref[0, 0, s * MOE_SUB + i]
        return pltpu.make_async_copy(hq_ref.at[pl.ds(pl.multiple_of(t * pitch, pitch), pitch), :],
                                     qbuf.at[slot, pl.ds(pl.multiple_of(i * pitch, pitch), pitch), :], sem_in.at[slot])

    def out_copy(s, i, slot):
        d = dst_ref[0, 0, s * MOE_SUB + i]
        return pltpu.make_async_copy(qbuf.at[slot, pl.ds(pl.multiple_of(i * pitch, 8), nq), :],
                                     y_ref.at[pl.ds(pl.multiple_of(d * nq, nq), nq), :], sem_out.at[slot])

    def for_rows(fn):
        def body(b, c):
            for u in range(MOE_ISSUE_UNROLL):
                fn(b * MOE_ISSUE_UNROLL + u)
            return c

        lax.fori_loop(0, MOE_SUB // MOE_ISSUE_UNROLL, body, 0)

    def wait_in(slot):
        pltpu.make_async_copy(hq_ref.at[pl.ds(0, MOE_SUB * pitch), :], qbuf.at[slot], sem_in.at[slot]).wait()

    def wait_out(slot):
        pltpu.make_async_copy(qbuf.at[slot, pl.ds(0, MOE_SUB * nq)], y_ref.at[pl.ds(0, MOE_SUB * nq), :],
                              sem_out.at[slot]).wait()

    def convert(s, slot):
        r0 = pl.multiple_of(s * MOE_SUB, MOE_SUB)
        for jj in range(nq):
            xb[pl.ds(r0, MOE_SUB), jj * LANES:(jj + 1) * LANES] = (
                qbuf[slot, pl.ds(jj, MOE_SUB, stride=pitch), :].astype(bf16))
        acc[pl.ds(r0, MOE_SUB), :] = jnp.zeros((MOE_SUB, acc.shape[1]), f32)

    def stage(s, slot):
        r0 = pl.multiple_of(s * MOE_SUB, MOE_SUB)
        a = acc[pl.ds(r0, MOE_SUB), :] + bd_ref[0]
        for jj in range(nq):
            qbuf[slot, pl.ds(jj, MOE_SUB, stride=pitch), :] = a[:, jj * LANES:(jj + 1) * LANES]

    npairs = (nsub + 1) // 2

    @pl.when((g == 0) & (j == 0))
    def _():
        qbuf[1] = jnp.zeros(qbuf.shape[1:], f32)
        fill = pltpu.make_async_copy(qbuf.at[1, pl.ds(0, MOE_SUB * nq)], y_ref.at[pl.ds(spare_row0, MOE_SUB * nq), :],
                                     sem_out.at[1])
        fill.start()
        fill.wait()

    def tile(r0, size):
        gu = jnp.dot(xb[pl.ds(r0, size), :], wgu_b[...], preferred_element_type=f32) + bgu_ref[0]
        gub = gu.astype(bf16)
        gates, ups = [], []
        for q in range(2 * MOE_TF // 256):
            de = jnp.dot(gub[:, q * 256:(q + 1) * 256], pm_ref[...], preferred_element_type=f32)
            gates.append(de[:, 0:LANES])
            ups.append(de[:, LANES:2 * LANES])
        gate = jnp.minimum(jnp.concatenate(gates, axis=1), SWIGLU_LIMIT)
        up = jnp.clip(jnp.concatenate(ups, axis=1), -SWIGLU_LIMIT, SWIGLU_LIMIT)
        act = (up + 1.0) * gate * (1.0 / (1.0 + jnp.exp(-SWIGLU_ALPHA * gate)))
        acc[pl.ds(r0, size), :] += jnp.dot(act.astype(bf16), wd_b[...], preferred_element_type=f32)

    first = j == 0
    last = j == nj - 1
    pair_rows = 2 * MOE_SUB
    assert MOE_TILES == (pair_rows, MOE_SUB, MOE_SUB // 2)
    padded = (rows + MOE_TILES[-1] - 1) // MOE_TILES[-1] * MOE_TILES[-1]

    @pl.when(active)
    def _():
        wgu_b[...] = wgu_ref[0].astype(bf16)
        wd_b[...] = wd_ref[0].astype(bf16)

    @pl.when(active & first)
    def _():
        for_rows(lambda i: in_copy(0, i, 0).start())

        @pl.when(nsub > 1)
        def _():
            for_rows(lambda i: in_copy(1, i, 1).start())

    def pair(pp, c):
        s0 = 2 * pp

        @pl.when(first)
        def _():
            for slot in (0, 1):
                s = s0 + slot

                @pl.when(s < nsub)
                def _(s=s, slot=slot):
                    wait_in(slot)
                    convert(s, slot)

                    @pl.when(s + 2 < nsub)
                    def _():
                        for_rows(lambda i: in_copy(s + 2, i, slot).start())

        r0 = pl.multiple_of(s0 * MOE_SUB, pair_rows)
        left = jnp.minimum(padded - r0, pair_rows)

        @pl.when(left == pair_rows)
        def _():
            tile(r0, pair_rows)

        @pl.when((left >= MOE_SUB) & (left < pair_rows))
        def _():
            tile(r0, MOE_SUB)

        @pl.when(left % MOE_SUB != 0)
        def _():
            tile(pl.multiple_of(r0 + left // MOE_SUB * MOE_SUB, MOE_TILES[-1]), MOE_TILES[-1])

        @pl.when(last)
        def _():
            for slot in (0, 1):
                s = s0 + slot

                @pl.when(s < nsub)
                def _(s=s, slot=slot):
                    @pl.when(pp > 0)
                    def _():
                        wait_out(slot)

                    stage(s, slot)
                    for_rows(lambda i: out_copy(s, i, slot).start())

        return c

    @pl.when(active)
    def _():
        lax.fori_loop(0, npairs, pair, 0)

    @pl.when(active & last)
    def _():
        wait_out(0)

        @pl.when(nsub >= 2)
        def _():
            wait_out(1)


def _deinterleave_matrix():
    pm = np.zeros((256, 256), np.float32)
    i = np.arange(LANES)
    pm[2 * i, i] = 1.0
    pm[2 * i + 1, LANES + i] = 1.0
    return jnp.asarray(pm, bf16)


def _moe(hq, eid, w_gate_up, b_gate_up, w_down, b_down, n):
    n_exp, d, f2 = w_gate_up.shape
    dff = f2 // 2
    nq = d // LANES
    nj = dff // MOE_TF
    assert nj >= 2, "the first and last hidden chunks carry the row gather and scatter"
    g_exp, g_rows, tok, dst = _moe_plan(eid, n_exp, MOE_R)
    n_groups = g_exp.shape[0]

    def jeff(g, j, gr):
        return jnp.where(gr[g] > 0, j, nj - 1)

    return pl.pallas_call(
        functools.partial(_moe_kernel, nq=nq, nj=nj, spare_row0=TOP_K * n * nq),
        grid_spec=pltpu.PrefetchScalarGridSpec(
            num_scalar_prefetch=2,
            grid=(n_groups, nj),
            in_specs=[
                pl.BlockSpec((1, 1, MOE_R), lambda g, j, ge, gr: (g, 0, 0), memory_space=pltpu.SMEM),
                pl.BlockSpec((1, 1, MOE_R), lambda g, j, ge, gr: (g, 0, 0), memory_space=pltpu.SMEM),
                pl.BlockSpec(memory_space=pl.ANY),
                pl.BlockSpec((1, d, 2 * MOE_TF), lambda g, j, ge, gr: (ge[g], 0, jeff(g, j, gr))),
                pl.BlockSpec((1, MOE_TF, d), lambda g, j, ge, gr: (ge[g], jeff(g, j, gr), 0)),
                pl.BlockSpec((1, 1, 2 * MOE_TF), lambda g, j, ge, gr: (ge[g], 0, jeff(g, j, gr))),
                pl.BlockSpec((1, 1, d), lambda g, j, ge, gr: (ge[g], 0, 0)),
                pl.BlockSpec((256, 256), lambda g, j, ge, gr: (0, 0)),
            ],
            out_specs=pl.BlockSpec(memory_space=pl.ANY),
            scratch_shapes=[
                pltpu.VMEM((2, MOE_SUB * TOK_PITCH, LANES), f32),
                pltpu.VMEM((MOE_R, d), bf16),
                pltpu.VMEM((MOE_R, d), f32),
                pltpu.VMEM((d, 2 * MOE_TF), bf16),
                pltpu.VMEM((MOE_TF, d), bf16),
                pltpu.SemaphoreType.DMA((2,)),
                pltpu.SemaphoreType.DMA((2,)),
            ],
        ),
        out_shape=jax.ShapeDtypeStruct(((TOP_K * n + MOE_SUB) * nq, LANES), f32),
        compiler_params=_cparams(("arbitrary", "arbitrary")),
        name="moe",
    )(g_exp, g_rows, tok, dst, hq, w_gate_up, w_down, b_gate_up.reshape(n_exp, 1, f2),
      b_down.reshape(n_exp, 1, d), _deinterleave_matrix())


def _final_kernel(h_ref, y0_ref, y1_ref, y2_ref, y3_ref, gate_ref, g2_ref, b2_ref, op_ref, os_ref, *, tm, nq, alpha, nbp):
    gates = gate_ref[...]
    f = jnp.zeros(h_ref.shape, f32)
    for k, y_ref in enumerate((y0_ref, y1_ref, y2_ref, y3_ref)):
        yk = jnp.concatenate([y_ref[pl.ds(jj, tm, stride=nq), :] for jj in range(nq)], axis=1)
        f = f + gates[:, k:k + 1] * yk
    z = alpha * h_ref[...] + f
    mu = jnp.mean(z, axis=-1, keepdims=True)
    var = jnp.mean(jnp.square(z - mu), axis=-1, keepdims=True)
    y = (z - mu) * lax.rsqrt(var + LN_EPS) * g2_ref[...] + b2_ref[...]
    i = pl.program_id(0)

    @pl.when(i < nbp)
    def _():
        op_ref[...] = y

    @pl.when(i >= nbp)
    def _():
        os_ref[...] = y


def _final(h, y4q, gates, ln2_g, ln2_b, alpha, n_prompt):
    n, d = h.shape
    nq = d // LANES
    tm = _pick(np.gcd(n_prompt, n - n_prompt), (256, 128, 64, 32, 16, 8))
    nb, nbp = n // tm, n_prompt // tm
    row = pl.BlockSpec((1, d), lambda i: (0, 0))
    yspec = lambda k: pl.BlockSpec((tm * nq, LANES), lambda i, k=k: (k * nb + i, 0))
    return pl.pallas_call(
        functools.partial(_final_kernel, tm=tm, nq=nq, alpha=alpha, nbp=nbp),
        grid=(nb,),
        in_specs=[pl.BlockSpec((tm, d), lambda i: (i, 0)), yspec(0), yspec(1), yspec(2), yspec(3),
                  pl.BlockSpec((tm, TOP_K), lambda i: (i, 0)), row, row],
        out_specs=[pl.BlockSpec((tm, d), lambda i: (jnp.minimum(i, nbp - 1), 0)),
                   pl.BlockSpec((tm, d), lambda i: (jnp.maximum(i - nbp, 0), 0))],
        out_shape=[jax.ShapeDtypeStruct((n_prompt, d), f32), jax.ShapeDtypeStruct((n - n_prompt, d), f32)],
        compiler_params=_cparams(("arbitrary",)),
        name="final",
    )(h, y4q, y4q, y4q, y4q, gates, ln2_g.reshape(1, d), ln2_b.reshape(1, d))


def kernel(x_prompt, x_sample, cache_k, cache_v, cache_idx_k, state_ret, page_table, w_in, w_o, ret_gn_w,
           ln1_g, ln1_b, w_router, b_router, w_gate_up, b_gate_up, w_down, b_down, ln2_g, ln2_b):
    depth = w_in.shape[0]
    assert depth == 1, "single-layer step"
    bp, t, d = x_prompt.shape
    bd, tn, _ = x_sample.shape
    assert bp == 1
    past = page_table.shape[1] * PAGE_SIZE
    np_, ns = bp * t, bd * tn
    n = np_ + ns
    alpha = (2 * depth) ** 0.25
    wkv = N_KV_A * HEAD_DIM_A
    layer = lambda a: a.reshape(a.shape[1:])

    x_all = jnp.concatenate([x_prompt.reshape(np_, d), x_sample.reshape(ns, d)], axis=0)
    pos_all = jnp.concatenate([jnp.arange(t), jnp.tile(past + jnp.arange(tn), bd)])
    u_f, u_b = _project(x_all, pos_all, layer(w_in))

    attn_p = _dsa_prompt(u_f, u_b, t)
    pages = lambda a: a.reshape(a.shape[1], PAGE_SIZE * N_KV_A, HEAD_DIM_A)
    attn_s = _dsa_sample(u_f, np_, bd, tn, pages(cache_k), pages(cache_v), jnp.swapaxes(layer(cache_idx_k), 1, 2),
                         page_table)

    zero_state = jnp.zeros((bp, N_HEADS_B, QK_DIM_B, V_DIM_B), f32)
    rb_p, s_p = _retention(u_f, 0, bp, t, zero_state, layer(ret_gn_w), bf16)
    rb_s, s_s = _retention(u_f, np_, bd, tn, layer(state_ret), layer(ret_gn_w), f32)

    h, hq, eid, gates = _tail1(attn_p, attn_s, rb_p, rb_s, x_all, layer(w_o), layer(ln1_g), layer(ln1_b), layer(w_router),
                               layer(b_router), alpha)
    y4q = _moe(hq, eid, layer(w_gate_up), layer(b_gate_up), layer(w_down), layer(b_down), n)
    y_p, y_s = _final(h, y4q, gates, layer(ln2_g), layer(ln2_b), alpha, np_)

    kv = lambda rows, c0, lead: u_f[rows, c0:c0 + wkv].reshape(lead + (N_KV_A, HEAD_DIM_A))[None]
    ps, ss = slice(0, np_), slice(np_, n)
    return (
        y_p.reshape(bp, t, d), y_s.reshape(bd, tn, d),
        kv(ps, C_KA, (bp, t)), kv(ps, C_VA, (bp, t)), u_f[ps, C_TAIL:C_TAIL + IDX_DIM].reshape(1, bp, t, IDX_DIM),
        s_p[None],
        kv(ss, C_KA, (bd, tn)), kv(ss, C_VA, (bd, tn)), u_f[ss, C_TAIL:C_TAIL + IDX_DIM].reshape(1, bd, tn, IDX_DIM),
        s_s[None],
    )
```

```python
import functools

import numpy as np
import jax
import jax.numpy as jnp
from jax import lax
from jax.experimental import pallas as pl
from jax.experimental.pallas import tpu as pltpu

f32 = jnp.float32
bf16 = jnp.bfloat16
i32 = jnp.int32

PAGE_SIZE = 128
HEAD_DIM_A = 128
N_HEADS_A = 8
N_KV_A = 4
ROPE_DIM_A = 32
ROPE_THETA = 500000.0
IDX_HEADS = 16
IDX_DIM = 64
IDX_ROPE_DIM = 16
TOPK_MAX = 256
V_DIM_B = 128
N_HEADS_B = 8
QK_DIM_B = 64
RET_CHUNK = 128
RET_THETA = 10000.0
TOP_K = 4
SWIGLU_LIMIT = 7.0
SWIGLU_ALPHA = 1.702
LN_EPS = 1e-5

LANES = 128
TOK_PITCH = 24
VMEM_LIMIT = 56 * 1024 * 1024

PROJ_TN = 512
C_QA, C_KA, C_VA, C_IQ, C_QB, C_KB, C_VB, C_GB, C_TAIL, PROJ_W = 0, 1024, 1536, 2048, 3072, 3584, 4096, 5120, 6144, 6656
PROJ_TILE_TYPES = (6, 6, 1, 7, 2, 2, 3, 4, 0, 0, 0, 0, 5)
PROJ_N_TABLES = 7
QA_SCALE = HEAD_DIM_A ** -0.5 * 1.4426950408889634

INT_MIN = -2 ** 31
KEY_NEG_INF = -2139095041
NEG_BIG = -1e30

NT_DIMS = (((1,), (1,)), ((), ()))
TN_DIMS = (((0,), (0,)), ((), ()))


def _pick(n, cands):
    for c in cands:
        if n % c == 0:
            return c
    raise ValueError(f"no tile for {n}")


def _cparams(sem, vmem=VMEM_LIMIT):
    return pltpu.CompilerParams(dimension_semantics=sem, vmem_limit_bytes=vmem)


def _rope_table(pos, rot_dim, theta, period, scale=1.0, active=LANES):
    half = rot_dim // 2
    inv_freq = 1.0 / (theta ** (jnp.arange(half, dtype=f32) / half))
    ang = pos.astype(f32)[:, None] * inv_freq[None, :]
    cos, sin = jnp.cos(ang), jnp.sin(ang)
    lane = np.arange(LANES)
    d = lane % period
    first = (d < half) & (lane < active)
    second = (d >= half) & (d < rot_dim) & (lane < active)
    idx = np.where(first, d, np.where(second, d - half, 0))
    cos_l, sin_l = cos[:, idx], sin[:, idx]
    c = jnp.where(first | second, cos_l, 1.0)
    s1 = jnp.where(second, sin_l, 0.0)
    s2 = jnp.where(first, -sin_l, 0.0)
    return jnp.concatenate([c, s1, s2], axis=1) * scale


def _proj_kernel(tt_ref, x_ref, w_ref, tab_ref, of_ref, ob_ref, kr_ref, vr_ref, xb_scr):
    j = pl.program_id(1)

    @pl.when(j == 0)
    def _():
        xb_scr[...] = x_ref[...].astype(bf16)

    t = tt_ref[j]

    def emit(half, rows_ref=None):
        piece = 2 * LANES
        tm = x_ref.shape[0]
        for c0 in range(0, PROJ_TN, piece):
            u = jnp.dot(xb_scr[...], w_ref[:, c0:c0 + piece], preferred_element_type=f32)
            if half is not None:
                c = tab_ref[0, :, 0:LANES]
                s1 = tab_ref[0, :, LANES:2 * LANES]
                s2 = tab_ref[0, :, 2 * LANES:3 * LANES]
                outs = []
                for q in range(piece // LANES):
                    uc = u[:, q * LANES:(q + 1) * LANES]
                    outs.append(uc * c + pltpu.roll(uc, half, 1) * s1 + pltpu.roll(uc, LANES - half, 1) * s2)
                u = jnp.concatenate(outs, axis=1)
            of_ref[:, c0:c0 + piece] = u
            ob_ref[:, c0:c0 + piece] = u.astype(bf16)
            if rows_ref is not None:
                for q in range(piece // LANES):
                    g = c0 // LANES + q
                    rows_ref[pl.ds(g, tm, stride=N_KV_A), :] = u[:, q * LANES:(q + 1) * LANES]

    @pl.when(t == 0)
    def _():
        emit(None)

    @pl.when(t == 7)
    def _():
        emit(None, vr_ref)

    @pl.when(t == 1)
    def _():
        emit(ROPE_DIM_A // 2, kr_ref)

    @pl.when(t == 6)
    def _():
        emit(ROPE_DIM_A // 2)

    @pl.when((t == 2) | (t == 5))
    def _():
        emit(IDX_ROPE_DIM // 2)

    @pl.when((t == 3) | (t == 4))
    def _():
        emit(QK_DIM_B // 2)


def _project(x_all, pos_all, w_in):
    n, d = x_all.shape
    tm = _pick(n, (768, 512, 384, 256, 128, 64, 32, 16, 8))
    o = np.cumsum((0, 1024, 512, 512, 1024, 64, 16, 512, 512, 1024, 1024))
    wp = jnp.concatenate([w_in[:, o[0]:o[4]], w_in[:, o[6]:o[10]], w_in[:, o[4]:o[6]],
                          jnp.zeros((d, PROJ_W - C_TAIL - 80), w_in.dtype)], axis=1).astype(bf16)
    tabs = jnp.stack([
        _rope_table(pos_all, ROPE_DIM_A, ROPE_THETA, HEAD_DIM_A),
        _rope_table(pos_all, ROPE_DIM_A, ROPE_THETA, HEAD_DIM_A),
        _rope_table(pos_all, IDX_ROPE_DIM, ROPE_THETA, IDX_DIM),
        _rope_table(pos_all, QK_DIM_B, RET_THETA, QK_DIM_B),
        _rope_table(pos_all, QK_DIM_B, RET_THETA, QK_DIM_B, scale=QK_DIM_B ** -0.5),
        _rope_table(pos_all, IDX_ROPE_DIM, ROPE_THETA, IDX_DIM, active=IDX_DIM),
        _rope_table(pos_all, ROPE_DIM_A, ROPE_THETA, HEAD_DIM_A, scale=QA_SCALE),
    ])
    tt = jnp.asarray(PROJ_TILE_TYPES, i32)
    nj = PROJ_W // PROJ_TN
    return pl.pallas_call(
        _proj_kernel,
        grid_spec=pltpu.PrefetchScalarGridSpec(
            num_scalar_prefetch=1,
            grid=(n // tm, nj),
            in_specs=[
                pl.BlockSpec((tm, d), lambda i, j, tt: (i, 0)),
                pl.BlockSpec((d, PROJ_TN), lambda i, j, tt: (0, j)),
                pl.BlockSpec((1, tm, 3 * LANES), lambda i, j, tt: (jnp.minimum(tt[j], PROJ_N_TABLES - 1), i, 0)),
            ],
            out_specs=[
                pl.BlockSpec((tm, PROJ_TN), lambda i, j, tt: (i, j)),
                pl.BlockSpec((tm, PROJ_TN), lambda i, j, tt: (i, j)),
                pl.BlockSpec((tm * N_KV_A, HEAD_DIM_A), lambda i, j, tt: (i, 0)),
                pl.BlockSpec((tm * N_KV_A, HEAD_DIM_A), lambda i, j, tt: (i, 0)),
            ],
            scratch_shapes=[pltpu.VMEM((tm, d), bf16)],
        ),
        out_shape=[jax.ShapeDtypeStruct((n, PROJ_W), f32), jax.ShapeDtypeStruct((n, PROJ_W), bf16),
                   jax.ShapeDtypeStruct((n * N_KV_A, HEAD_DIM_A), f32), jax.ShapeDtypeStruct((n * N_KV_A, HEAD_DIM_A), f32)],
        compiler_params=_cparams(("arbitrary", "arbitrary")),
        name="proj",
    )(tt, x_all, wp, tabs)


def _key_to_float(key):
    return pltpu.bitcast(key ^ ((key >> 31) & 0x7FFFFFFF), f32)


def _kth_threshold(count_ge, shape, k):
    def body(step, ans):
        cand = ans + jnp.left_shift(jnp.int32(1), 31 - step)
        return jnp.where(count_ge(_key_to_float(cand)) >= k, cand, ans)

    ans = lax.fori_loop(0, 32, body, jnp.full(shape, INT_MIN, i32))
    return _key_to_float(jnp.maximum(ans, KEY_NEG_INF + 1))


def _lane_blocks(x):
    return [x[:, i * LANES:(i + 1) * LANES] for i in range(x.shape[1] // LANES)]


def _flash_update(qs, ks, vs, bias, m_scr, l_scr, acc_scr, batched):
    n = len(qs)
    score = lambda g: lax.dot_general(qs[g], ks[g], NT_DIMS, preferred_element_type=f32) + bias

    def update(g, sm):
        m_old = m_scr[g]
        m_new = jnp.maximum(m_old, jnp.max(functools.reduce(jnp.maximum, _lane_blocks(sm)), axis=1, keepdims=True))
        alpha = jnp.exp2(m_old - m_new)
        p = jnp.exp2(sm - m_new)
        l_scr[g] = alpha * l_scr[g] + jnp.sum(functools.reduce(jnp.add, _lane_blocks(p)), axis=1, keepdims=True)
        m_scr[g] = m_new
        acc_scr[g] = alpha * acc_scr[g] + jnp.dot(p.astype(bf16), vs[g], preferred_element_type=f32)

    if batched:
        sms = [score(g) for g in range(n)]
        for g in range(n):
            update(g, sms[g])
    else:
        for g in range(n):
            update(g, score(g))


DSA_GROUP_BATCH = 4


def _dsa_prompt_kernel(iq_ref, iw_ref, qa_ref, ikd_ref, ka_ref, vat_ref, o_ref,
                       qst, qgt, sct, m_scr, l_scr, acc_scr, *, tq, tk, topk):
    i = pl.program_id(0)
    n_chunks = (i * tq + tq + tk - 1) // tk
    row = lax.broadcasted_iota(i32, (LANES, tq), 0)

    for p in range(IDX_HEADS // 2):
        blk = iq_ref[:, p * LANES:(p + 1) * LANES].T
        qst[:, (2 * p) * tq:(2 * p + 1) * tq] = jnp.where(row < IDX_DIM, blk, 0.0).astype(bf16)
        qst[:, (2 * p + 1) * tq:(2 * p + 2) * tq] = jnp.where(row >= IDX_DIM, blk, 0.0).astype(bf16)
    wt = iw_ref[...].T * (IDX_DIM ** -0.5 * IDX_HEADS ** -0.5)
    for h in range(N_HEADS_A):
        qgt[h // 2, :, (h % 2) * tq:(h % 2 + 1) * tq] = qa_ref[:, h * LANES:(h + 1) * LANES].T.astype(bf16)

    qpos = i * tq + lax.broadcasted_iota(i32, (tk, tq), 1)

    def score_body(c, carry):
        k0 = pl.multiple_of(c * tk, tk)
        logits = jnp.dot(ikd_ref[pl.ds(k0, tk), :], qst[...], preferred_element_type=f32)
        acc = jnp.zeros((tk, tq), f32)
        for h in range(IDX_HEADS):
            acc = acc + jnp.maximum(logits[:, h * tq:(h + 1) * tq], 0.0) * wt[IDX_DIM + h:IDX_DIM + h + 1, :]
        kpos = k0 + lax.broadcasted_iota(i32, (tk, tq), 0)
        sct[c] = jnp.where(kpos <= qpos, acc, -jnp.inf)
        return carry

    lax.fori_loop(0, n_chunks, score_body, 0)

    def count_ge(cand):
        cb = jnp.broadcast_to(cand, (8, tq))

        def body(c, accs):
            accs = list(accs)
            for r in range(tk // 8):
                accs[r % 4] = accs[r % 4] + jnp.where(sct[c, r * 8:(r + 1) * 8, :] >= cb, 1.0, 0.0)
            return tuple(accs)

        accs = lax.fori_loop(0, n_chunks, body, (jnp.zeros((8, tq), f32),) * 4)
        return jnp.sum((accs[0] + accs[1]) + (accs[2] + accs[3]), axis=0, keepdims=True)

    thr = _kth_threshold(count_ge, (1, tq), float(topk))

    m_scr[...] = jnp.full(m_scr.shape, NEG_BIG, f32)
    l_scr[...] = jnp.zeros(l_scr.shape, f32)
    acc_scr[...] = jnp.zeros(acc_scr.shape, f32)

    def att_body(c, carry):
        k0 = pl.multiple_of(c * tk, tk)
        bias1 = jnp.where(sct[c] >= thr, 0.0, NEG_BIG)
        bias = jnp.concatenate([bias1, bias1], axis=1)
        for g0 in range(0, N_KV_A, DSA_GROUP_BATCH):
            groups = range(g0, g0 + DSA_GROUP_BATCH)
            sms = {g: jnp.dot(ka_ref[pl.ds(k0, tk), g * LANES:(g + 1) * LANES], qgt[g],
                              preferred_element_type=f32) + bias for g in groups}
            m_new = {g: jnp.maximum(m_scr[g], jnp.max(sms[g], axis=0, keepdims=True)) for g in groups}
            alpha = {g: jnp.exp2(m_scr[g] - m_new[g]) for g in groups}
            ps = {g: jnp.exp2(sms[g] - m_new[g]) for g in groups}
            for g in groups:
                l_scr[g] = alpha[g] * l_scr[g] + jnp.sum(ps[g], axis=0, keepdims=True)
                m_scr[g] = m_new[g]
            for g in groups:
                acc_scr[g] = alpha[g] * acc_scr[g] + jnp.dot(vat_ref[c, g * LANES:(g + 1) * LANES, :],
                                                             ps[g].astype(bf16), preferred_element_type=f32)
        return carry

    lax.fori_loop(0, n_chunks, att_body, 0)

    for g in range(N_KV_A):
        o = acc_scr[g] / l_scr[g]
        o_ref[:, (2 * g) * LANES:(2 * g + 1) * LANES] = o[:, 0:tq].T.astype(o_ref.dtype)
        o_ref[:, (2 * g + 1) * LANES:(2 * g + 2) * LANES] = o[:, tq:2 * tq].T.astype(o_ref.dtype)


def _dsa_prompt(u_f, u_b, t):
    tq = _pick(t, (128,))
    tk = _pick(t, (512, 256, 128))
    topk = min(TOPK_MAX, t // 4)
    wa = N_HEADS_A * HEAD_DIM_A
    wkv = N_KV_A * HEAD_DIM_A
    ik_b = u_b[:t, C_TAIL:C_TAIL + IDX_DIM]
    ikd = jnp.concatenate([ik_b, ik_b], axis=1)
    vat = u_b[:t, C_VA:C_VA + wkv].reshape(t // tk, tk, wkv).transpose(0, 2, 1)
    kern = functools.partial(_dsa_prompt_kernel, tq=tq, tk=tk, topk=topk)
    one = pl.Buffered(1)
    return pl.pallas_call(
        kern,
        grid=(t // tq,),
        in_specs=[
            pl.BlockSpec((tq, IDX_HEADS * IDX_DIM), lambda i: (i, C_IQ // (IDX_HEADS * IDX_DIM))),
            pl.BlockSpec((tq, LANES), lambda i: (i, C_TAIL // LANES)),
            pl.BlockSpec((tq, wa), lambda i: (i, C_QA // wa)),
            pl.BlockSpec((t, LANES), lambda i: (0, 0), pipeline_mode=one),
            pl.BlockSpec((t, wkv), lambda i: (0, C_KA // wkv), pipeline_mode=one),
            pl.BlockSpec((t // tk, wkv, tk), lambda i: (0, 0, 0), pipeline_mode=one),
        ],
        out_specs=pl.BlockSpec((tq, wa), lambda i: (i, 0)),
        out_shape=jax.ShapeDtypeStruct((t, wa), bf16),
        scratch_shapes=[
            pltpu.VMEM((LANES, IDX_HEADS * tq), bf16),
            pltpu.VMEM((N_KV_A, LANES, 2 * tq), bf16),
            pltpu.VMEM((t // tk, tk, tq), f32),
            pltpu.VMEM((N_KV_A, 1, 2 * tq), f32),
            pltpu.VMEM((N_KV_A, 1, 2 * tq), f32),
            pltpu.VMEM((N_KV_A, LANES, 2 * tq), f32),
        ],
        compiler_params=_cparams(("arbitrary",)),
        name="dsa_prompt",
    )(u_f, u_f, u_f, ikd, u_b, vat)


DSA_SCORE_CHUNK_PAGES = 32
DSA_PAGE_ISSUE_UNROLL = 8


def _dsa_sample_score_kernel(pt_ref, idx_hbm, iq_ref, tail_ref, sc_ref, buf, qs, wst, sem, *, tn, n_pages):
    b = pl.program_id(0)
    nb = pl.num_programs(0)
    cpp = DSA_SCORE_CHUNK_PAGES
    n_chunks = n_pages // cpp
    kw = cpp * PAGE_SIZE
    wscale = IDX_DIM ** -0.5 * IDX_HEADS ** -0.5

    def issue(seq, slot):
        def body(blk, c):
            for u in range(DSA_PAGE_ISSUE_UNROLL):
                p = blk * DSA_PAGE_ISSUE_UNROLL + u
                pltpu.make_async_copy(idx_hbm.at[pt_ref[seq, p]], buf.at[slot, p], sem.at[slot]).start()
            return c

        lax.fori_loop(0, n_pages // DSA_PAGE_ISSUE_UNROLL, body, 0)

    def wait(slot):
        pltpu.make_async_copy(idx_hbm.at[pl.ds(0, n_pages)], buf.at[slot], sem.at[slot]).wait()

    @pl.when(b == 0)
    def _():
        issue(0, 0)

    iq = iq_ref[...]
    w = tail_ref[...]
    for h in range(IDX_HEADS):
        qs[h * tn:(h + 1) * tn, :] = iq[:, h * IDX_DIM:(h + 1) * IDX_DIM]
        wst[h * tn:(h + 1) * tn, :] = jnp.broadcast_to(w[:, IDX_DIM + h:IDX_DIM + h + 1] * wscale, (tn, LANES))
    qsb = qs[...].astype(bf16)

    def head_sum(logits):
        width = logits.shape[1]
        acc = jnp.zeros((tn, width), f32)
        for h in range(IDX_HEADS):
            wh = wst[h * tn:(h + 1) * tn, :]
            acc = acc + jnp.maximum(logits[h * tn:(h + 1) * tn, :], 0.0) * jnp.concatenate([wh] * (width // LANES), axis=1)
        return acc

    def past_scores(slot):
        @pl.when(b + 1 < nb)
        def _():
            issue(b + 1, 1 - slot)

        wait(slot)
        for c in range(n_chunks):
            keys_t = jnp.concatenate([buf[slot, c * cpp + p] for p in range(cpp)], axis=1).astype(bf16)
            sc_ref[0, c] = head_sum(jnp.dot(qsb, keys_t, preferred_element_type=f32))

    @pl.when(b % 2 == 0)
    def _():
        past_scores(0)

    @pl.when(b % 2 == 1)
    def _():
        past_scores(1)

    ik_new = tail_ref[:, 0:IDX_DIM].astype(bf16)
    kpad = jnp.concatenate([ik_new, jnp.zeros((LANES - tn, IDX_DIM), bf16)], axis=0)
    s_new = head_sum(lax.dot_general(qsb, kpad, NT_DIMS, preferred_element_type=f32))
    qi = lax.broadcasted_iota(i32, (tn, LANES), 0)
    kj = lax.broadcasted_iota(i32, (tn, LANES), 1)
    snew = jnp.where(kj <= qi, s_new, -jnp.inf)
    sc_ref[0, n_chunks] = jnp.concatenate([snew, jnp.full((tn, kw - LANES), -jnp.inf, f32)], axis=1)


def _dsa_sample_thr_kernel(sc_ref, thr_ref, *, ns, nc, tn, kw, topk):
    def count_ge(cand):
        out = []
        for s in range(ns):
            cb = jnp.broadcast_to(cand[s], (tn, LANES))
            parts = [jnp.zeros((tn, LANES), f32)] * 4
            for c in range(nc):
                blk = sc_ref[s, c]
                for q in range(kw // LANES):
                    parts[q % 4] = parts[q % 4] + jnp.where(blk[:, q * LANES:(q + 1) * LANES] >= cb, 1.0, 0.0)
            out.append(jnp.sum((parts[0] + parts[1]) + (parts[2] + parts[3]), axis=1, keepdims=True))
        return jnp.stack(out)

    thr = _kth_threshold(count_ge, (ns, tn, 1), float(topk))
    thr_ref[...] = jnp.broadcast_to(thr, (ns, tn, LANES))


def _dsa_sample_attn_kernel(pt_ref, *refs, tn, ppg, n_groups):
    kpages = refs[:ppg]
    vpages = refs[ppg:2 * ppg]
    qa_ref, kn_ref, vn_ref, sc_ref, scn_ref, thr_ref, o_ref, qg, m_scr, l_scr, acc_scr = refs[2 * ppg:]
    c = pl.program_id(1)

    @pl.when(c == 0)
    def _():
        qa = qa_ref[...]
        for g in range(N_KV_A):
            qg[g, 0:tn, :] = qa[:, (2 * g) * LANES:(2 * g + 1) * LANES]
            qg[g, tn:2 * tn, :] = qa[:, (2 * g + 1) * LANES:(2 * g + 2) * LANES]
        m_scr[...] = jnp.full(m_scr.shape, NEG_BIG, f32)
        l_scr[...] = jnp.zeros(l_scr.shape, f32)
        acc_scr[...] = jnp.zeros(acc_scr.shape, f32)

    thr = thr_ref[0][:, 0:1]
    thr2 = jnp.concatenate([thr, thr], axis=0)

    def attend(ks, vs, sc):
        bias = jnp.where(jnp.concatenate([sc, sc], axis=0) >= thr2, 0.0, NEG_BIG)
        _flash_update([qg[g].astype(bf16) for g in range(N_KV_A)], ks, vs, bias, m_scr, l_scr, acc_scr, batched=True)

    group = lambda pages, g: jnp.concatenate(
        [p[pl.ds(g, PAGE_SIZE, stride=N_KV_A), :] for p in pages], axis=0).astype(bf16)
    attend([group(kpages, g) for g in range(N_KV_A)], [group(vpages, g) for g in range(N_KV_A)], sc_ref[0, 0])

    @pl.when(c == n_groups - 1)
    def _():
        zpad = jnp.zeros((LANES - tn, N_KV_A * HEAD_DIM_A), bf16)
        attend(_lane_blocks(jnp.concatenate([kn_ref[...].astype(bf16), zpad], axis=0)),
               _lane_blocks(jnp.concatenate([vn_ref[...].astype(bf16), zpad], axis=0)),
               scn_ref[0, 0][:, 0:LANES])
        for g in range(N_KV_A):
            o = acc_scr[g] / l_scr[g]
            o_ref[:, (2 * g) * LANES:(2 * g + 1) * LANES] = o[0:tn]
            o_ref[:, (2 * g + 1) * LANES:(2 * g + 2) * LANES] = o[tn:2 * tn]


def _dsa_sample(u_f, row0, bd, tn, cache_k, cache_v, idx_kt, page_table):
    n_pages = page_table.shape[1]
    past = n_pages * PAGE_SIZE
    topk = min(TOPK_MAX, (past + tn) // 4)
    ppa = _pick(n_pages, (16, 8, 4, 2, 1))
    pps = _pick(n_pages, (DSA_SCORE_CHUNK_PAGES,))
    nga, ngs = n_pages // ppa, n_pages // pps
    kwa, kws = ppa * PAGE_SIZE, pps * PAGE_SIZE
    per = kws // kwa
    wkv = N_KV_A * HEAD_DIM_A
    wa = N_HEADS_A * HEAD_DIM_A
    rb0 = row0 // tn

    def kv_spec(j):
        return pl.BlockSpec((None, PAGE_SIZE * N_KV_A, HEAD_DIM_A), lambda b, c, pt, j=j: (pt[b, c * ppa + j], 0, 0))

    score = pl.pallas_call(
        functools.partial(_dsa_sample_score_kernel, tn=tn, n_pages=n_pages),
        grid_spec=pltpu.PrefetchScalarGridSpec(
            num_scalar_prefetch=1,
            grid=(bd,),
            in_specs=[
                pl.BlockSpec(memory_space=pl.ANY),
                pl.BlockSpec((tn, IDX_HEADS * IDX_DIM), lambda b, pt: (rb0 + b, C_IQ // (IDX_HEADS * IDX_DIM))),
                pl.BlockSpec((tn, LANES), lambda b, pt: (rb0 + b, C_TAIL // LANES)),
            ],
            out_specs=pl.BlockSpec((1, ngs + 1, tn, kws), lambda b, pt: (b, 0, 0, 0)),
            scratch_shapes=[
                pltpu.VMEM((2, n_pages, IDX_DIM, PAGE_SIZE), f32),
                pltpu.VMEM((IDX_HEADS * tn, IDX_DIM), f32),
                pltpu.VMEM((IDX_HEADS * tn, LANES), f32),
                pltpu.SemaphoreType.DMA((2,)),
            ],
        ),
        out_shape=jax.ShapeDtypeStruct((bd, ngs + 1, tn, kws), f32),
        compiler_params=_cparams(("arbitrary",)),
        name="dsa_sample_score",
    )
    scores = score(page_table, idx_kt, u_f, u_f)

    nst = _pick(bd, (8, 4, 2, 1))
    thr = pl.pallas_call(
        functools.partial(_dsa_sample_thr_kernel, ns=nst, nc=ngs + 1, tn=tn, kw=kws, topk=topk),
        grid=(bd // nst,),
        in_specs=[pl.BlockSpec((nst, ngs + 1, tn, kws), lambda i: (i, 0, 0, 0))],
        out_specs=pl.BlockSpec((nst, tn, LANES), lambda i: (i, 0, 0)),
        out_shape=jax.ShapeDtypeStruct((bd, tn, LANES), f32),
        compiler_params=_cparams(("arbitrary",)),
        name="dsa_sample_thr",
    )(scores)

    attn = pl.pallas_call(
        functools.partial(_dsa_sample_attn_kernel, tn=tn, ppg=ppa, n_groups=nga),
        grid_spec=pltpu.PrefetchScalarGridSpec(
            num_scalar_prefetch=1,
            grid=(bd, nga),
            in_specs=[kv_spec(j) for j in range(ppa)] + [kv_spec(j) for j in range(ppa)] + [
                pl.BlockSpec((tn, wa), lambda b, c, pt: (rb0 + b, C_QA // wa)),
                pl.BlockSpec((tn, wkv), lambda b, c, pt: (rb0 + b, C_KA // wkv)),
                pl.BlockSpec((tn, wkv), lambda b, c, pt: (rb0 + b, C_VA // wkv)),
                pl.BlockSpec((1, 1, tn, kwa), lambda b, c, pt: (b, c // per, 0, c % per)),
                pl.BlockSpec((1, 1, tn, kwa), lambda b, c, pt: (b, ngs, 0, 0)),
                pl.BlockSpec((1, tn, LANES), lambda b, c, pt: (b, 0, 0)),
            ],
            out_specs=pl.BlockSpec((tn, wa), lambda b, c, pt: (b, 0)),
            scratch_shapes=[
                pltpu.VMEM((N_KV_A, 2 * tn, LANES), f32),
                pltpu.VMEM((N_KV_A, 2 * tn, 1), f32),
                pltpu.VMEM((N_KV_A, 2 * tn, 1), f32),
                pltpu.VMEM((N_KV_A, 2 * tn, LANES), f32),
            ],
        ),
        out_shape=jax.ShapeDtypeStruct((bd * tn, wa), f32),
        compiler_params=_cparams(("arbitrary", "arbitrary")),
        name="dsa_sample_attn",
    )
    return attn(page_table, *([cache_k] * ppa), *([cache_v] * ppa), u_f, u_f, u_f, scores, scores, thr)


def _ret_kernel(q_ref, k_ref, v_ref, g_ref, gnw_ref, dmask_ref, qdec_ref, kdec_ref, sdec_ref, s0_ref,
                o_ref, sout_ref, s_scr, *, n_chunks):
    c = pl.program_id(1)

    @pl.when(c == 0)
    def _():
        s_scr[...] = s0_ref[0]

    for h in range(N_HEADS_B):
        q = q_ref[:, h * QK_DIM_B:(h + 1) * QK_DIM_B]
        k = k_ref[:, h * QK_DIM_B:(h + 1) * QK_DIM_B]
        v = v_ref[:, h * V_DIM_B:(h + 1) * V_DIM_B].astype(bf16)
        gate = g_ref[:, h * V_DIM_B:(h + 1) * V_DIM_B]
        qb = q.astype(bf16)
        att = lax.dot_general(qb, k.astype(bf16), NT_DIMS, preferred_element_type=f32) * dmask_ref[h]
        s_old = s_scr[h]
        o = (jnp.dot(att.astype(bf16), v, preferred_element_type=f32)
             + jnp.dot(qb, s_old.astype(bf16), preferred_element_type=f32) * qdec_ref[h])
        kd = (k * kdec_ref[h]).astype(bf16)
        s_scr[h] = s_old * sdec_ref[h] + lax.dot_general(kd, v, TN_DIMS, preferred_element_type=f32)
        mu = jnp.mean(o, axis=-1, keepdims=True)
        var = jnp.mean(jnp.square(o - mu), axis=-1, keepdims=True)
        rb = (o - mu) * lax.rsqrt(var + LN_EPS) * gnw_ref[:, h * V_DIM_B:(h + 1) * V_DIM_B]
        rb = rb * (gate / (1.0 + jnp.exp(-gate)))
        o_ref[:, h * V_DIM_B:(h + 1) * V_DIM_B] = rb.astype(o_ref.dtype)

    @pl.when(c == n_chunks - 1)
    def _():
        sout_ref[0] = s_scr[...]


def _retention(u_f, row0, nb, t, state0, gn_w, out_dtype):
    ch = min(RET_CHUNK, t)
    if t % ch:
        ch = t
    n = t // ch
    hb = N_HEADS_B
    lg = jnp.log1p(-jnp.exp2(-5.0 - jnp.arange(hb, dtype=f32)))
    i = jnp.arange(ch)
    diff = i[:, None] - i[None, :]
    dmask = jnp.where(diff >= 0, jnp.exp(lg[:, None, None] * jnp.maximum(diff, 0)), 0.0)
    qdec = jnp.broadcast_to(jnp.exp(lg[:, None] * (i + 1))[:, :, None], (hb, ch, V_DIM_B))
    kdec = jnp.broadcast_to(jnp.exp(lg[:, None] * (ch - 1 - i))[:, :, None], (hb, ch, QK_DIM_B))
    sdec = jnp.broadcast_to(jnp.exp(lg * ch)[:, None, None], (hb, 1, V_DIM_B))
    wqk = hb * QK_DIM_B
    wv = hb * V_DIM_B
    rb0 = row0 // ch
    full3 = lambda shp: pl.BlockSpec(shp, lambda b, c: (0, 0, 0))
    return pl.pallas_call(
        functools.partial(_ret_kernel, n_chunks=n),
        grid=(nb, n),
        in_specs=[
            pl.BlockSpec((ch, wqk), lambda b, c: (rb0 + b * n + c, C_QB // wqk)),
            pl.BlockSpec((ch, wqk), lambda b, c: (rb0 + b * n + c, C_KB // wqk)),
            pl.BlockSpec((ch, wv), lambda b, c: (rb0 + b * n + c, C_VB // wv)),
            pl.BlockSpec((ch, wv), lambda b, c: (rb0 + b * n + c, C_GB // wv)),
            pl.BlockSpec((1, wv), lambda b, c: (0, 0)),
            full3((hb, ch, ch)), full3((hb, ch, V_DIM_B)), full3((hb, ch, QK_DIM_B)), full3((hb, 1, V_DIM_B)),
            pl.BlockSpec((1, hb, QK_DIM_B, V_DIM_B), lambda b, c: (b, 0, 0, 0)),
        ],
        out_specs=[
            pl.BlockSpec((ch, wv), lambda b, c: (b * n + c, 0)),
            pl.BlockSpec((1, hb, QK_DIM_B, V_DIM_B), lambda b, c: (b, 0, 0, 0)),
        ],
        out_shape=[jax.ShapeDtypeStruct((nb * t, wv), out_dtype),
                   jax.ShapeDtypeStruct((nb, hb, QK_DIM_B, V_DIM_B), f32)],
        scratch_shapes=[pltpu.VMEM((hb, QK_DIM_B, V_DIM_B), f32)],
        compiler_params=_cparams(("arbitrary", "arbitrary")),
        name="retention",
    )(u_f, u_f, u_f, u_f, gn_w.reshape(1, wv), dmask, qdec, kdec, sdec, state0)


def _split_hi_lo(a):
    hi = a.astype(bf16)
    return hi, (a - hi.astype(f32)).astype(bf16)


def _tail1_kernel(attn_p_ref, attn_s_ref, rb_p_ref, rb_s_ref, x_ref, wo_ref, g1_ref, b1_ref, wrh_ref, wrl_ref, br_ref,
                  h_ref, hq_ref, eid_ref, gate_ref, *, tm, n_exp, alpha, wa, nbp):
    is_prompt = pl.program_id(0) < nbp
    attn = jnp.where(is_prompt, attn_p_ref[...], attn_s_ref[...].astype(bf16))
    rb = jnp.where(is_prompt, rb_p_ref[...], rb_s_ref[...].astype(bf16))
    mix = (jnp.dot(attn, wo_ref[0:wa, :], preferred_element_type=f32)
           + jnp.dot(rb, wo_ref[wa:, :], preferred_element_type=f32))
    z = alpha * x_ref[...] + mix
    mu = jnp.mean(z, axis=-1, keepdims=True)
    var = jnp.mean(jnp.square(z - mu), axis=-1, keepdims=True)
    h = (z - mu) * lax.rsqrt(var + LN_EPS) * g1_ref[...] + b1_ref[...]
    h_ref[...] = h
    nq = h.shape[1] // LANES
    for j in range(TOK_PITCH):
        slab = h[:, j * LANES:(j + 1) * LANES] if j < nq else jnp.zeros((tm, LANES), f32)
        hq_ref[pl.ds(j, tm, stride=TOK_PITCH), :] = slab

    hh, hl = _split_hi_lo(h)
    logits = (jnp.dot(hh, wrh_ref[...], preferred_element_type=f32)
              + jnp.dot(hl, wrh_ref[...], preferred_element_type=f32)
              + jnp.dot(hh, wrl_ref[...], preferred_element_type=f32)) + br_ref[...]
    lane = lax.broadcasted_iota(i32, (tm, LANES), 1)
    lanef = lane.astype(f32)
    logits = jnp.where(lane < n_exp, logits, -jnp.inf)
    vals, ids = [], []
    for _ in range(TOP_K):
        m = jnp.max(logits, axis=1, keepdims=True)
        idx = jnp.min(jnp.where(logits == m, lanef, float(LANES)), axis=1, keepdims=True)
        vals.append(m)
        ids.append(idx)
        logits = jnp.where(lanef == idx, -jnp.inf, logits)
    es = [jnp.exp(v - vals[0]) for v in vals]
    den = es[0] + es[1] + es[2] + es[3]
    eid_ref[...] = jnp.concatenate(ids, axis=1).astype(i32)
    gate_ref[...] = jnp.concatenate([e / den for e in es], axis=1)


def _tail1(attn_p, attn_s, rb_p, rb_s, x_all, w_o, ln1_g, ln1_b, w_router, b_router, alpha):
    n, d = x_all.shape
    wa, wb = attn_p.shape[1], rb_p.shape[1]
    n_prompt = attn_p.shape[0]
    n_exp = w_router.shape[1]
    tm = _pick(np.gcd(n_prompt, n - n_prompt), (256, 128, 64, 32, 16, 8))
    nbp = n_prompt // tm
    prompt_rows = lambda w: pl.BlockSpec((tm, w), lambda i: (jnp.minimum(i, nbp - 1), 0))
    sample_rows = lambda w: pl.BlockSpec((tm, w), lambda i: (jnp.maximum(i - nbp, 0), 0))
    nq = d // LANES
    wr = jnp.zeros((d, LANES), f32).at[:, :n_exp].set(w_router)
    wrh, wrl = _split_hi_lo(wr)
    br = jnp.zeros((1, LANES), f32).at[0, :n_exp].set(b_router)
    row = lambda w: pl.BlockSpec((1, w), lambda i: (0, 0))
    return pl.pallas_call(
        functools.partial(_tail1_kernel, tm=tm, n_exp=n_exp, alpha=alpha, wa=wa, nbp=nbp),
        grid=(n // tm,),
        in_specs=[
            prompt_rows(wa), sample_rows(wa), prompt_rows(wb), sample_rows(wb),
            pl.BlockSpec((tm, d), lambda i: (i, 0)),
            pl.BlockSpec(w_o.shape, lambda i: (0, 0), pipeline_mode=pl.Buffered(1)),
            row(d), row(d),
            pl.BlockSpec((d, LANES), lambda i: (0, 0)), pl.BlockSpec((d, LANES), lambda i: (0, 0)), row(LANES),
        ],
        out_specs=[
            pl.BlockSpec((tm, d), lambda i: (i, 0)),
            pl.BlockSpec((tm * TOK_PITCH, LANES), lambda i: (i, 0)),
            pl.BlockSpec((tm, TOP_K), lambda i: (i, 0)),
            pl.BlockSpec((tm, TOP_K), lambda i: (i, 0)),
        ],
        out_shape=[jax.ShapeDtypeStruct((n, d), f32), jax.ShapeDtypeStruct((n * TOK_PITCH, LANES), f32),
                   jax.ShapeDtypeStruct((n, TOP_K), i32), jax.ShapeDtypeStruct((n, TOP_K), f32)],
        compiler_params=_cparams(("arbitrary",)),
        name="tail1",
    )(attn_p, attn_s, rb_p, rb_s, x_all, w_o.astype(bf16), ln1_g.reshape(1, d), ln1_b.reshape(1, d), wrh, wrl, br)


MOE_R = 2048
MOE_SUB = 256
MOE_TF = 256
MOE_ISSUE_UNROLL = 8
MOE_TILES = (512, 256, 128)


def _moe_plan(eid, n_exp, r_cap):
    n = eid.shape[0]
    p = n * TOP_K
    flat = eid.reshape(p)
    onehot = (flat[:, None] == jnp.arange(n_exp, dtype=i32)[None, :]).astype(i32)
    csum = jnp.cumsum(onehot, axis=0)
    rank = jnp.sum((csum - onehot) * onehot, axis=1)
    counts = csum[-1]
    ngrp = (counts + r_cap - 1) // r_cap
    gend = jnp.cumsum(ngrp)
    gstart = gend - ngrp
    g_of = gstart[flat] + rank // r_cap
    slot = rank % r_cap
    n_groups = n_exp + p // r_cap
    pair = jnp.full((n_groups, r_cap), -1, i32).at[g_of, slot].set(jnp.arange(p, dtype=i32))
    t_idx, k_idx = pair // TOP_K, pair % TOP_K
    spare = jnp.broadcast_to(p + jnp.arange(r_cap, dtype=i32) % MOE_SUB, (n_groups, r_cap))
    tok = jnp.where(pair >= 0, t_idx, 0)
    dst = jnp.where(pair >= 0, k_idx * n + t_idx, spare)
    gid = jnp.arange(n_groups, dtype=i32)
    total = gend[-1]
    gclamp = jnp.minimum(gid, total - 1)
    g_exp = jnp.sum((gend[None, :] <= gclamp[:, None]).astype(i32), axis=1)
    g_rows = jnp.clip(counts[g_exp] - (gclamp - gstart[g_exp]) * r_cap, 0, r_cap)
    g_rows = jnp.where(gid < total, g_rows, 0).astype(i32)
    return g_exp, g_rows, tok.reshape(n_groups, 1, r_cap), dst.reshape(n_groups, 1, r_cap)


def _moe_kernel(ge_ref, gr_ref, tok_ref, dst_ref, hq_ref, wgu_ref, wd_ref, bgu_ref, bd_ref, pm_ref, y_ref,
                qbuf, xb, acc, wgu_b, wd_b, sem_in, sem_out, *, nq, nj, spare_row0):
    pitch = TOK_PITCH
    g = pl.program_id(0)
    j = pl.program_id(1)
    rows = gr_ref[g]
    nsub = (rows + MOE_SUB - 1) // MOE_SUB
    active = rows > 0

    def in_copy(s, i, slot):
        t = tok_ref[0, 0, s * MOE_SUB + i]
        return pltpu.make_async_copy(hq_ref.at[pl.ds(pl.multiple_of(t * pitch, pitch), pitch), :],
                                     qbuf.at[slot, pl.ds(pl.multiple_of(i * pitch, pitch), pitch), :], sem_in.at[slot])

    def out_copy(s, i, slot):
        d = dst_ref[0, 0, s * MOE_SUB + i]
        return pltpu.make_async_copy(qbuf.at[slot, pl.ds(pl.multiple_of(i * pitch, 8), nq), :],
                                     y_ref.at[pl.ds(pl.multiple_of(d * nq, nq), nq), :], sem_out.at[slot])

    def for_rows(fn):
        def body(b, c):
            for u in range(MOE_ISSUE_UNROLL):
                fn(b * MOE_ISSUE_UNROLL + u)
            return c

        lax.fori_loop(0, MOE_SUB // MOE_ISSUE_UNROLL, body, 0)

    def wait_in(slot):
        pltpu.make_async_copy(hq_ref.at[pl.ds(0, MOE_SUB * pitch), :], qbuf.at[slot], sem_in.at[slot]).wait()

    def wait_out(slot):
        pltpu.make_async_copy(qbuf.at[slot, pl.ds(0, MOE_SUB * nq)], y_ref.at[pl.ds(0, MOE_SUB * nq), :],
                              sem_out.at[slot]).wait()

    def convert(s, slot):
        r0 = pl.multiple_of(s * MOE_SUB, MOE_SUB)
        for jj in range(nq):
            xb[pl.ds(r0, MOE_SUB), jj * LANES:(jj + 1) * LANES] = (
                qbuf[slot, pl.ds(jj, MOE_SUB, stride=pitch), :].astype(bf16))
        acc[pl.ds(r0, MOE_SUB), :] = jnp.zeros((MOE_SUB, acc.shape[1]), f32)

    def stage(s, slot):
        r0 = pl.multiple_of(s * MOE_SUB, MOE_SUB)
        a = acc[pl.ds(r0, MOE_SUB), :] + bd_ref[0]
        for jj in range(nq):
            qbuf[slot, pl.ds(jj, MOE_SUB, stride=pitch), :] = a[:, jj * LANES:(jj + 1) * LANES]

    npairs = (nsub + 1) // 2

    @pl.when((g == 0) & (j == 0))
    def _():
        qbuf[1] = jnp.zeros(qbuf.shape[1:], f32)
        fill = pltpu.make_async_copy(qbuf.at[1, pl.ds(0, MOE_SUB * nq)], y_ref.at[pl.ds(spare_row0, MOE_SUB * nq), :],
                                     sem_out.at[1])
        fill.start()
        fill.wait()

    def tile(r0, size):
        gu = jnp.dot(xb[pl.ds(r0, size), :], wgu_b[...], preferred_element_type=f32) + bgu_ref[0]
        gub = gu.astype(bf16)
        gates, ups = [], []
        for q in range(2 * MOE_TF // 256):
            de = jnp.dot(gub[:, q * 256:(q + 1) * 256], pm_ref[...], preferred_element_type=f32)
            gates.append(de[:, 0:LANES])
            ups.append(de[:, LANES:2 * LANES])
        gate = jnp.minimum(jnp.concatenate(gates, axis=1), SWIGLU_LIMIT)
        up = jnp.clip(jnp.concatenate(ups, axis=1), -SWIGLU_LIMIT, SWIGLU_LIMIT)
        act = (up + 1.0) * gate * (1.0 / (1.0 + jnp.exp(-SWIGLU_ALPHA * gate)))
        acc[pl.ds(r0, size), :] += jnp.dot(act.astype(bf16), wd_b[...], preferred_element_type=f32)

    first = j == 0
    last = j == nj - 1
    pair_rows = 2 * MOE_SUB
    assert MOE_TILES == (pair_rows, MOE_SUB, MOE_SUB // 2)
    padded = (rows + MOE_TILES[-1] - 1) // MOE_TILES[-1] * MOE_TILES[-1]

    @pl.when(active)
    def _():
        wgu_b[...] = wgu_ref[0].astype(bf16)
        wd_b[...] = wd_ref[0].astype(bf16)

    @pl.when(active & first)
    def _():
        for_rows(lambda i: in_copy(0, i, 0).start())

        @pl.when(nsub > 1)
        def _():
            for_rows(lambda i: in_copy(1, i, 1).start())

    def pair(pp, c):
        s0 = 2 * pp

        @pl.when(first)
        def _():
            for slot in (0, 1):
                s = s0 + slot

                @pl.when(s < nsub)
                def _(s=s, slot=slot):
                    wait_in(slot)
                    convert(s, slot)

                    @pl.when(s + 2 < nsub)
                    def _():
                        for_rows(lambda i: in_copy(s + 2, i, slot).start())

        r0 = pl.multiple_of(s0 * MOE_SUB, pair_rows)
        left = jnp.minimum(padded - r0, pair_rows)

        @pl.when(left == pair_rows)
        def _():
            tile(r0, pair_rows)

        @pl.when((left >= MOE_SUB) & (left < pair_rows))
        def _():
            tile(r0, MOE_SUB)

        @pl.when(left % MOE_SUB != 0)
        def _():
            tile(pl.multiple_of(r0 + left // MOE_SUB * MOE_SUB, MOE_TILES[-1]), MOE_TILES[-1])

        @pl.when(last)
        def _():
            for slot in (0, 1):
                s = s0 + slot

                @pl.when(s < nsub)
                def _(s=s, slot=slot):
                    @pl.when(pp > 0)
                    def _():
                        wait_out(slot)

                    stage(s, slot)
                    for_rows(lambda i: out_copy(s, i, slot).start())

        return c

    @pl.when(active)
    def _():
        lax.fori_loop(0, npairs, pair, 0)

    @pl.when(active & last)
    def _():
        wait_out(0)

        @pl.when(nsub >= 2)
        def _():
            wait_out(1)


def _deinterleave_matrix():
    pm = np.zeros((256, 256), np.float32)
    i = np.arange(LANES)
    pm[2 * i, i] = 1.0
    pm[2 * i + 1, LANES + i] = 1.0
    return jnp.asarray(pm, bf16)


def _moe(hq, eid, w_gate_up, b_gate_up, w_down, b_down, n):
    n_exp, d, f2 = w_gate_up.shape
    dff = f2 // 2
    nq = d // LANES
    nj = dff // MOE_TF
    assert nj >= 2, "the first and last hidden chunks carry the row gather and scatter"
    g_exp, g_rows, tok, dst = _moe_plan(eid, n_exp, MOE_R)
    n_groups = g_exp.shape[0]

    def jeff(g, j, gr):
        return jnp.where(gr[g] > 0, j, nj - 1)

    return pl.pallas_call(
        functools.partial(_moe_kernel, nq=nq, nj=nj, spare_row0=TOP_K * n * nq),
        grid_spec=pltpu.PrefetchScalarGridSpec(
            num_scalar_prefetch=2,
            grid=(n_groups, nj),
            in_specs=[
                pl.BlockSpec((1, 1, MOE_R), lambda g, j, ge, gr: (g, 0, 0), memory_space=pltpu.SMEM),
                pl.BlockSpec((1, 1, MOE_R), lambda g, j, ge, gr: (g, 0, 0), memory_space=pltpu.SMEM),
                pl.BlockSpec(memory_space=pl.ANY),
                pl.BlockSpec((1, d, 2 * MOE_TF), lambda g, j, ge, gr: (ge[g], 0, jeff(g, j, gr))),
                pl.BlockSpec((1, MOE_TF, d), lambda g, j, ge, gr: (ge[g], jeff(g, j, gr), 0)),
                pl.BlockSpec((1, 1, 2 * MOE_TF), lambda g, j, ge, gr: (ge[g], 0, jeff(g, j, gr))),
                pl.BlockSpec((1, 1, d), lambda g, j, ge, gr: (ge[g], 0, 0)),
                pl.BlockSpec((256, 256), lambda g, j, ge, gr: (0, 0)),
            ],
            out_specs=pl.BlockSpec(memory_space=pl.ANY),
            scratch_shapes=[
                pltpu.VMEM((2, MOE_SUB * TOK_PITCH, LANES), f32),
                pltpu.VMEM((MOE_R, d), bf16),
                pltpu.VMEM((MOE_R, d), f32),
                pltpu.VMEM((d, 2 * MOE_TF), bf16),
                pltpu.VMEM((MOE_TF, d), bf16),
                pltpu.SemaphoreType.DMA((2,)),
                pltpu.SemaphoreType.DMA((2,)),
            ],
        ),
        out_shape=jax.ShapeDtypeStruct(((TOP_K * n + MOE_SUB) * nq, LANES), f32),
        compiler_params=_cparams(("arbitrary", "arbitrary")),
        name="moe",
    )(g_exp, g_rows, tok, dst, hq, w_gate_up, w_down, b_gate_up.reshape(n_exp, 1, f2),
      b_down.reshape(n_exp, 1, d), _deinterleave_matrix())


def _final_kernel(h_ref, y0_ref, y1_ref, y2_ref, y3_ref, gate_ref, g2_ref, b2_ref, op_ref, os_ref, *, tm, nq, alpha, nbp):
    gates = gate_ref[...]
    f = jnp.zeros(h_ref.shape, f32)
    for k, y_ref in enumerate((y0_ref, y1_ref, y2_ref, y3_ref)):
        yk = jnp.concatenate([y_ref[pl.ds(jj, tm, stride=nq), :] for jj in range(nq)], axis=1)
        f = f + gates[:, k:k + 1] * yk
    z = alpha * h_ref[...] + f
    mu = jnp.mean(z, axis=-1, keepdims=True)
    var = jnp.mean(jnp.square(z - mu), axis=-1, keepdims=True)
    y = (z - mu) * lax.rsqrt(var + LN_EPS) * g2_ref[...] + b2_ref[...]
    i = pl.program_id(0)

    @pl.when(i < nbp)
    def _():
        op_ref[...] = y

    @pl.when(i >= nbp)
    def _():
        os_ref[...] = y


def _final(h, y4q, gates, ln2_g, ln2_b, alpha, n_prompt):
    n, d = h.shape
    nq = d // LANES
    tm = _pick(np.gcd(n_prompt, n - n_prompt), (256, 128, 64, 32, 16, 8))
    nb, nbp = n // tm, n_prompt // tm
    row = pl.BlockSpec((1, d), lambda i: (0, 0))
    yspec = lambda k: pl.BlockSpec((tm * nq, LANES), lambda i, k=k: (k * nb + i, 0))
    return pl.pallas_call(
        functools.partial(_final_kernel, tm=tm, nq=nq, alpha=alpha, nbp=nbp),
        grid=(nb,),
        in_specs=[pl.BlockSpec((tm, d), lambda i: (i, 0)), yspec(0), yspec(1), yspec(2), yspec(3),
                  pl.BlockSpec((tm, TOP_K), lambda i: (i, 0)), row, row],
        out_specs=[pl.BlockSpec((tm, d), lambda i: (jnp.minimum(i, nbp - 1), 0)),
                   pl.BlockSpec((tm, d), lambda i: (jnp.maximum(i - nbp, 0), 0))],
        out_shape=[jax.ShapeDtypeStruct((n_prompt, d), f32), jax.ShapeDtypeStruct((n - n_prompt, d), f32)],
        compiler_params=_cparams(("arbitrary",)),
        name="final",
    )(h, y4q, y4q, y4q, y4q, gates, ln2_g.reshape(1, d), ln2_b.reshape(1, d))


def kernel(x_prompt, x_sample, cache_k, cache_v, cache_idx_k, state_ret, page_table, w_in, w_o, ret_gn_w,
           ln1_g, ln1_b, w_router, b_router, w_gate_up, b_gate_up, w_down, b_down, ln2_g, ln2_b):
    depth = w_in.shape[0]
    assert depth == 1, "single-layer step"
    bp, t, d = x_prompt.shape
    bd, tn, _ = x_sample.shape
    assert bp == 1
    past = page_table.shape[1] * PAGE_SIZE
    np_, ns = bp * t, bd * tn
    n = np_ + ns
    alpha = (2 * depth) ** 0.25
    wkv = N_KV_A * HEAD_DIM_A
    layer = lambda a: a.reshape(a.shape[1:])

    x_all = jnp.concatenate([x_prompt.reshape(np_, d), x_sample.reshape(ns, d)], axis=0)
    pos_all = jnp.concatenate([jnp.arange(t), jnp.tile(past + jnp.arange(tn), bd)])
    u_f, u_b, k_rows, v_rows = _project(x_all, pos_all, layer(w_in))

    attn_p = _dsa_prompt(u_f, u_b, t)
    pages = lambda a: a.reshape(a.shape[1], PAGE_SIZE * N_KV_A, HEAD_DIM_A)
    attn_s = _dsa_sample(u_f, np_, bd, tn, pages(cache_k), pages(cache_v), jnp.swapaxes(layer(cache_idx_k), 1, 2),
                         page_table)

    zero_state = jnp.zeros((bp, N_HEADS_B, QK_DIM_B, V_DIM_B), f32)
    rb_p, s_p = _retention(u_f, 0, bp, t, zero_state, layer(ret_gn_w), bf16)
    rb_s, s_s = _retention(u_f, np_, bd, tn, layer(state_ret), layer(ret_gn_w), f32)

    h, hq, eid, gates = _tail1(attn_p, attn_s, rb_p, rb_s, x_all, layer(w_o), layer(ln1_g), layer(ln1_b), layer(w_router),
                               layer(b_router), alpha)
    y4q = _moe(hq, eid, layer(w_gate_up), layer(b_gate_up), layer(w_down), layer(b_down), n)
    y_p, y_s = _final(h, y4q, gates, layer(ln2_g), layer(ln2_b), alpha, np_)

    split = np_ * N_KV_A
    kv_p = lambda a: a[:split].reshape(1, bp, t, N_KV_A, HEAD_DIM_A)
    kv_s = lambda a: a[split:].reshape(1, bd, tn, N_KV_A, HEAD_DIM_A)
    ps, ss = slice(0, np_), slice(np_, n)
    return (
        y_p.reshape(bp, t, d), y_s.reshape(bd, tn, d),
        kv_p(k_rows), kv_p(v_rows), u_f[ps, C_TAIL:C_TAIL + IDX_DIM].reshape(1, bp, t, IDX_DIM),
        s_p[None],
        kv_s(k_rows), kv_s(v_rows), u_f[ss, C_TAIL:C_TAIL + IDX_DIM].reshape(1, bd, tn, IDX_DIM),
        s_s[None],
    )
```

```python
import functools

import numpy as np
import jax
import jax.numpy as jnp
from jax import lax
from jax.experimental import pallas as pl
from jax.experimental.pallas import tpu as pltpu

f32 = jnp.float32
bf16 = jnp.bfloat16
i32 = jnp.int32

PAGE_SIZE = 128
HEAD_DIM_A = 128
N_HEADS_A = 8
N_KV_A = 4
ROPE_DIM_A = 32
ROPE_THETA = 500000.0
IDX_HEADS = 16
IDX_DIM = 64
IDX_ROPE_DIM = 16
TOPK_MAX = 256
V_DIM_B = 128
N_HEADS_B = 8
QK_DIM_B = 64
RET_CHUNK = 128
RET_THETA = 10000.0
TOP_K = 4
SWIGLU_LIMIT = 7.0
SWIGLU_ALPHA = 1.702
LN_EPS = 1e-5

LANES = 128
TOK_PITCH = 24
VMEM_LIMIT = 56 * 1024 * 1024

PROJ_TN = 512
C_QA, C_KA, C_VA, C_IQ, C_QB, C_KB, C_VB, C_GB, C_TAIL, PROJ_W = 0, 1024, 1536, 2048, 3072, 3584, 4096, 5120, 6144, 6656
PROJ_TILE_TYPES = (6, 6, 1, 7, 2, 2, 3, 4, 0, 0, 0, 0, 5)
PROJ_N_TABLES = 7
QA_SCALE = HEAD_DIM_A ** -0.5 * 1.4426950408889634

INT_MIN = -2 ** 31
KEY_NEG_INF = -2139095041
NEG_BIG = -1e30

NT_DIMS = (((1,), (1,)), ((), ()))
TN_DIMS = (((0,), (0,)), ((), ()))


def _pick(n, cands):
    for c in cands:
        if n % c == 0:
            return c
    raise ValueError(f"no tile for {n}")


def _cparams(sem, vmem=VMEM_LIMIT):
    return pltpu.CompilerParams(dimension_semantics=sem, vmem_limit_bytes=vmem)


def _rope_table(pos, rot_dim, theta, period, scale=1.0, active=LANES):
    half = rot_dim // 2
    inv_freq = 1.0 / (theta ** (jnp.arange(half, dtype=f32) / half))
    ang = pos.astype(f32)[:, None] * inv_freq[None, :]
    cos, sin = jnp.cos(ang), jnp.sin(ang)
    lane = np.arange(LANES)
    d = lane % period
    first = (d < half) & (lane < active)
    second = (d >= half) & (d < rot_dim) & (lane < active)
    idx = np.where(first, d, np.where(second, d - half, 0))
    cos_l, sin_l = cos[:, idx], sin[:, idx]
    c = jnp.where(first | second, cos_l, 1.0)
    s1 = jnp.where(second, sin_l, 0.0)
    s2 = jnp.where(first, -sin_l, 0.0)
    return jnp.concatenate([c, s1, s2], axis=1) * scale


def _proj_kernel(tt_ref, x_ref, w_ref, tab_ref, of_ref, ob_ref, kr_ref, vr_ref, xb_scr):
    j = pl.program_id(1)

    @pl.when(j == 0)
    def _():
        xb_scr[...] = x_ref[...].astype(bf16)

    t = tt_ref[j]

    def emit(half, rows_ref=None):
        piece = 2 * LANES
        tm = x_ref.shape[0]
        for c0 in range(0, PROJ_TN, piece):
            u = jnp.dot(xb_scr[...], w_ref[:, c0:c0 + piece], preferred_element_type=f32)
            if half is not None:
                c = tab_ref[0, :, 0:LANES]
                s1 = tab_ref[0, :, LANES:2 * LANES]
                s2 = tab_ref[0, :, 2 * LANES:3 * LANES]
                outs = []
                for q in range(piece // LANES):
                    uc = u[:, q * LANES:(q + 1) * LANES]
                    outs.append(uc * c + pltpu.roll(uc, half, 1) * s1 + pltpu.roll(uc, LANES - half, 1) * s2)
                u = jnp.concatenate(outs, axis=1)
            of_ref[:, c0:c0 + piece] = u
            ob_ref[:, c0:c0 + piece] = u.astype(bf16)
            if rows_ref is not None:
                for q in range(piece // LANES):
                    g = c0 // LANES + q
                    rows_ref[pl.ds(g, tm, stride=N_KV_A), :] = u[:, q * LANES:(q + 1) * LANES]

    @pl.when(t == 0)
    def _():
        emit(None)

    @pl.when(t == 7)
    def _():
        emit(None, vr_ref)

    @pl.when(t == 1)
    def _():
        emit(ROPE_DIM_A // 2, kr_ref)

    @pl.when(t == 6)
    def _():
        emit(ROPE_DIM_A // 2)

    @pl.when((t == 2) | (t == 5))
    def _():
        emit(IDX_ROPE_DIM // 2)

    @pl.when((t == 3) | (t == 4))
    def _():
        emit(QK_DIM_B // 2)


def _project(x_all, pos_all, w_in):
    n, d = x_all.shape
    tm = _pick(n, (1056, 768, 512, 384, 256, 128, 64, 32, 16, 8))
    o = np.cumsum((0, 1024, 512, 512, 1024, 64, 16, 512, 512, 1024, 1024))
    wp = jnp.concatenate([w_in[:, o[0]:o[4]], w_in[:, o[6]:o[10]], w_in[:, o[4]:o[6]],
                          jnp.zeros((d, PROJ_W - C_TAIL - 80), w_in.dtype)], axis=1).astype(bf16)
    tabs = jnp.stack([
        _rope_table(pos_all, ROPE_DIM_A, ROPE_THETA, HEAD_DIM_A),
        _rope_table(pos_all, ROPE_DIM_A, ROPE_THETA, HEAD_DIM_A),
        _rope_table(pos_all, IDX_ROPE_DIM, ROPE_THETA, IDX_DIM),
        _rope_table(pos_all, QK_DIM_B, RET_THETA, QK_DIM_B),
        _rope_table(pos_all, QK_DIM_B, RET_THETA, QK_DIM_B, scale=QK_DIM_B ** -0.5),
        _rope_table(pos_all, IDX_ROPE_DIM, ROPE_THETA, IDX_DIM, active=IDX_DIM),
        _rope_table(pos_all, ROPE_DIM_A, ROPE_THETA, HEAD_DIM_A, scale=QA_SCALE),
    ])
    tt = jnp.asarray(PROJ_TILE_TYPES, i32)
    nj = PROJ_W // PROJ_TN
    return pl.pallas_call(
        _proj_kernel,
        grid_spec=pltpu.PrefetchScalarGridSpec(
            num_scalar_prefetch=1,
            grid=(n // tm, nj),
            in_specs=[
                pl.BlockSpec((tm, d), lambda i, j, tt: (i, 0)),
                pl.BlockSpec((d, PROJ_TN), lambda i, j, tt: (0, j)),
                pl.BlockSpec((1, tm, 3 * LANES), lambda i, j, tt: (jnp.minimum(tt[j], PROJ_N_TABLES - 1), i, 0)),
            ],
            out_specs=[
                pl.BlockSpec((tm, PROJ_TN), lambda i, j, tt: (i, j)),
                pl.BlockSpec((tm, PROJ_TN), lambda i, j, tt: (i, j)),
                pl.BlockSpec((tm * N_KV_A, HEAD_DIM_A), lambda i, j, tt: (i, 0)),
                pl.BlockSpec((tm * N_KV_A, HEAD_DIM_A), lambda i, j, tt: (i, 0)),
            ],
            scratch_shapes=[pltpu.VMEM((tm, d), bf16)],
        ),
        out_shape=[jax.ShapeDtypeStruct((n, PROJ_W), f32), jax.ShapeDtypeStruct((n, PROJ_W), bf16),
                   jax.ShapeDtypeStruct((n * N_KV_A, HEAD_DIM_A), f32), jax.ShapeDtypeStruct((n * N_KV_A, HEAD_DIM_A), f32)],
        compiler_params=_cparams(("arbitrary", "arbitrary")),
        name="proj",
    )(tt, x_all, wp, tabs)


def _key_to_float(key):
    return pltpu.bitcast(key ^ ((key >> 31) & 0x7FFFFFFF), f32)


def _kth_threshold(count_ge, shape, k):
    def body(step, ans):
        cand = ans + jnp.left_shift(jnp.int32(1), 31 - step)
        return jnp.where(count_ge(_key_to_float(cand)) >= k, cand, ans)

    ans = lax.fori_loop(0, 32, body, jnp.full(shape, INT_MIN, i32))
    return _key_to_float(jnp.maximum(ans, KEY_NEG_INF + 1))


def _lane_blocks(x):
    return [x[:, i * LANES:(i + 1) * LANES] for i in range(x.shape[1] // LANES)]


def _flash_update(qs, ks, vs, bias, m_scr, l_scr, acc_scr, batched):
    n = len(qs)
    score = lambda g: lax.dot_general(qs[g], ks[g], NT_DIMS, preferred_element_type=f32) + bias

    def update(g, sm):
        m_old = m_scr[g]
        m_new = jnp.maximum(m_old, jnp.max(functools.reduce(jnp.maximum, _lane_blocks(sm)), axis=1, keepdims=True))
        alpha = jnp.exp2(m_old - m_new)
        p = jnp.exp2(sm - m_new)
        l_scr[g] = alpha * l_scr[g] + jnp.sum(functools.reduce(jnp.add, _lane_blocks(p)), axis=1, keepdims=True)
        m_scr[g] = m_new
        acc_scr[g] = alpha * acc_scr[g] + jnp.dot(p.astype(bf16), vs[g], preferred_element_type=f32)

    if batched:
        sms = [score(g) for g in range(n)]
        for g in range(n):
            update(g, sms[g])
    else:
        for g in range(n):
            update(g, score(g))


DSA_GROUP_BATCH = 4


def _dsa_prompt_kernel(iq_ref, iw_ref, qa_ref, ikd_ref, ka_ref, vat_ref, o_ref,
                       qst, qgt, sct, m_scr, l_scr, acc_scr, *, tq, tk, topk):
    i = pl.program_id(0)
    n_chunks = (i * tq + tq + tk - 1) // tk
    row = lax.broadcasted_iota(i32, (LANES, tq), 0)

    for p in range(IDX_HEADS // 2):
        blk = iq_ref[:, p * LANES:(p + 1) * LANES].T
        qst[:, (2 * p) * tq:(2 * p + 1) * tq] = jnp.where(row < IDX_DIM, blk, 0.0).astype(bf16)
        qst[:, (2 * p + 1) * tq:(2 * p + 2) * tq] = jnp.where(row >= IDX_DIM, blk, 0.0).astype(bf16)
    wt = iw_ref[...].T * (IDX_DIM ** -0.5 * IDX_HEADS ** -0.5)
    for h in range(N_HEADS_A):
        qgt[h // 2, :, (h % 2) * tq:(h % 2 + 1) * tq] = qa_ref[:, h * LANES:(h + 1) * LANES].T.astype(bf16)

    qpos = i * tq + lax.broadcasted_iota(i32, (tk, tq), 1)

    def score_body(c, carry):
        k0 = pl.multiple_of(c * tk, tk)
        logits = jnp.dot(ikd_ref[pl.ds(k0, tk), :], qst[...], preferred_element_type=f32)
        acc = jnp.zeros((tk, tq), f32)
        for h in range(IDX_HEADS):
            acc = acc + jnp.maximum(logits[:, h * tq:(h + 1) * tq], 0.0) * wt[IDX_DIM + h:IDX_DIM + h + 1, :]
        kpos = k0 + lax.broadcasted_iota(i32, (tk, tq), 0)
        sct[c] = jnp.where(kpos <= qpos, acc, -jnp.inf)
        return carry

    lax.fori_loop(0, n_chunks, score_body, 0)

    def count_ge(cand):
        cb = jnp.broadcast_to(cand, (8, tq))

        def body(c, accs):
            accs = list(accs)
            for r in range(tk // 8):
                accs[r % 4] = accs[r % 4] + jnp.where(sct[c, r * 8:(r + 1) * 8, :] >= cb, 1.0, 0.0)
            return tuple(accs)

        accs = lax.fori_loop(0, n_chunks, body, (jnp.zeros((8, tq), f32),) * 4)
        return jnp.sum((accs[0] + accs[1]) + (accs[2] + accs[3]), axis=0, keepdims=True)

    thr = _kth_threshold(count_ge, (1, tq), float(topk))

    m_scr[...] = jnp.full(m_scr.shape, NEG_BIG, f32)
    l_scr[...] = jnp.zeros(l_scr.shape, f32)
    acc_scr[...] = jnp.zeros(acc_scr.shape, f32)

    def att_body(c, carry):
        k0 = pl.multiple_of(c * tk, tk)
        bias1 = jnp.where(sct[c] >= thr, 0.0, NEG_BIG)
        bias = jnp.concatenate([bias1, bias1], axis=1)
        for g0 in range(0, N_KV_A, DSA_GROUP_BATCH):
            groups = range(g0, g0 + DSA_GROUP_BATCH)
            sms = {g: jnp.dot(ka_ref[pl.ds(k0, tk), g * LANES:(g + 1) * LANES], qgt[g],
                              preferred_element_type=f32) + bias for g in groups}
            m_new = {g: jnp.maximum(m_scr[g], jnp.max(sms[g], axis=0, keepdims=True)) for g in groups}
            alpha = {g: jnp.exp2(m_scr[g] - m_new[g]) for g in groups}
            ps = {g: jnp.exp2(sms[g] - m_new[g]) for g in groups}
            for g in groups:
                l_scr[g] = alpha[g] * l_scr[g] + jnp.sum(ps[g], axis=0, keepdims=True)
                m_scr[g] = m_new[g]
            for g in groups:
                acc_scr[g] = alpha[g] * acc_scr[g] + jnp.dot(vat_ref[c, g * LANES:(g + 1) * LANES, :],
                                                             ps[g].astype(bf16), preferred_element_type=f32)
        return carry

    lax.fori_loop(0, n_chunks, att_body, 0)

    for g in range(N_KV_A):
        o = acc_scr[g] / l_scr[g]
        o_ref[:, (2 * g) * LANES:(2 * g + 1) * LANES] = o[:, 0:tq].T.astype(o_ref.dtype)
        o_ref[:, (2 * g + 1) * LANES:(2 * g + 2) * LANES] = o[:, tq:2 * tq].T.astype(o_ref.dtype)


def _dsa_prompt(u_f, u_b, t):
    tq = _pick(t, (128,))
    tk = _pick(t, (512, 256, 128))
    topk = min(TOPK_MAX, t // 4)
    wa = N_HEADS_A * HEAD_DIM_A
    wkv = N_KV_A * HEAD_DIM_A
    ik_b = u_b[:t, C_TAIL:C_TAIL + IDX_DIM]
    ikd = jnp.concatenate([ik_b, ik_b], axis=1)
    vat = u_b[:t, C_VA:C_VA + wkv].reshape(t // tk, tk, wkv).transpose(0, 2, 1)
    kern = functools.partial(_dsa_prompt_kernel, tq=tq, tk=tk, topk=topk)
    one = pl.Buffered(1)
    return pl.pallas_call(
        kern,
        grid=(t // tq,),
        in_specs=[
            pl.BlockSpec((tq, IDX_HEADS * IDX_DIM), lambda i: (i, C_IQ // (IDX_HEADS * IDX_DIM))),
            pl.BlockSpec((tq, LANES), lambda i: (i, C_TAIL // LANES)),
            pl.BlockSpec((tq, wa), lambda i: (i, C_QA // wa)),
            pl.BlockSpec((t, LANES), lambda i: (0, 0), pipeline_mode=one),
            pl.BlockSpec((t, wkv), lambda i: (0, C_KA // wkv), pipeline_mode=one),
            pl.BlockSpec((t // tk, wkv, tk), lambda i: (0, 0, 0), pipeline_mode=one),
        ],
        out_specs=pl.BlockSpec((tq, wa), lambda i: (i, 0)),
        out_shape=jax.ShapeDtypeStruct((t, wa), bf16),
        scratch_shapes=[
            pltpu.VMEM((LANES, IDX_HEADS * tq), bf16),
            pltpu.VMEM((N_KV_A, LANES, 2 * tq), bf16),
            pltpu.VMEM((t // tk, tk, tq), f32),
            pltpu.VMEM((N_KV_A, 1, 2 * tq), f32),
            pltpu.VMEM((N_KV_A, 1, 2 * tq), f32),
            pltpu.VMEM((N_KV_A, LANES, 2 * tq), f32),
        ],
        compiler_params=_cparams(("arbitrary",)),
        name="dsa_prompt",
    )(u_f, u_f, u_f, ikd, u_b, vat)


DSA_SCORE_CHUNK_PAGES = 32
DSA_PAGE_ISSUE_UNROLL = 8


def _dsa_sample_score_kernel(pt_ref, idx_hbm, iq_ref, tail_ref, sc_ref, buf, qs, wst, sem, *, tn, n_pages):
    b = pl.program_id(0)
    nb = pl.num_programs(0)
    cpp = DSA_SCORE_CHUNK_PAGES
    n_chunks = n_pages // cpp
    kw = cpp * PAGE_SIZE
    wscale = IDX_DIM ** -0.5 * IDX_HEADS ** -0.5

    def issue(seq, slot):
        def body(blk, c):
            for u in range(DSA_PAGE_ISSUE_UNROLL):
                p = blk * DSA_PAGE_ISSUE_UNROLL + u
                pltpu.make_async_copy(idx_hbm.at[pt_ref[seq, p]], buf.at[slot, p], sem.at[slot]).start()
            return c

        lax.fori_loop(0, n_pages // DSA_PAGE_ISSUE_UNROLL, body, 0)

    def wait(slot):
        pltpu.make_async_copy(idx_hbm.at[pl.ds(0, n_pages)], buf.at[slot], sem.at[slot]).wait()

    @pl.when(b == 0)
    def _():
        issue(0, 0)

    iq = iq_ref[...]
    w = tail_ref[...]
    for h in range(IDX_HEADS):
        qs[h * tn:(h + 1) * tn, :] = iq[:, h * IDX_DIM:(h + 1) * IDX_DIM]
        wst[h * tn:(h + 1) * tn, :] = jnp.broadcast_to(w[:, IDX_DIM + h:IDX_DIM + h + 1] * wscale, (tn, LANES))
    qsb = qs[...].astype(bf16)

    def head_sum(logits):
        width = logits.shape[1]
        acc = jnp.zeros((tn, width), f32)
        for h in range(IDX_HEADS):
            wh = wst[h * tn:(h + 1) * tn, :]
            acc = acc + jnp.maximum(logits[h * tn:(h + 1) * tn, :], 0.0) * jnp.concatenate([wh] * (width // LANES), axis=1)
        return acc

    def past_scores(slot):
        @pl.when(b + 1 < nb)
        def _():
            issue(b + 1, 1 - slot)

        wait(slot)
        for c in range(n_chunks):
            keys_t = jnp.concatenate([buf[slot, c * cpp + p] for p in range(cpp)], axis=1).astype(bf16)
            sc_ref[0, c] = head_sum(jnp.dot(qsb, keys_t, preferred_element_type=f32))

    @pl.when(b % 2 == 0)
    def _():
        past_scores(0)

    @pl.when(b % 2 == 1)
    def _():
        past_scores(1)

    ik_new = tail_ref[:, 0:IDX_DIM].astype(bf16)
    kpad = jnp.concatenate([ik_new, jnp.zeros((LANES - tn, IDX_DIM), bf16)], axis=0)
    s_new = head_sum(lax.dot_general(qsb, kpad, NT_DIMS, preferred_element_type=f32))
    qi = lax.broadcasted_iota(i32, (tn, LANES), 0)
    kj = lax.broadcasted_iota(i32, (tn, LANES), 1)
    snew = jnp.where(kj <= qi, s_new, -jnp.inf)
    sc_ref[0, n_chunks] = jnp.concatenate([snew, jnp.full((tn, kw - LANES), -jnp.inf, f32)], axis=1)


def _dsa_sample_thr_kernel(sc_ref, thr_ref, *, ns, nc, tn, kw, topk):
    def count_ge(cand):
        out = []
        for s in range(ns):
            cb = jnp.broadcast_to(cand[s], (tn, LANES))
            parts = [jnp.zeros((tn, LANES), f32)] * 4
            for c in range(nc):
                blk = sc_ref[s, c]
                for q in range(kw // LANES):
                    parts[q % 4] = parts[q % 4] + jnp.where(blk[:, q * LANES:(q + 1) * LANES] >= cb, 1.0, 0.0)
            out.append(jnp.sum((parts[0] + parts[1]) + (parts[2] + parts[3]), axis=1, keepdims=True))
        return jnp.stack(out)

    thr = _kth_threshold(count_ge, (ns, tn, 1), float(topk))
    thr_ref[...] = jnp.broadcast_to(thr, (ns, tn, LANES))


def _dsa_sample_attn_kernel(pt_ref, *refs, tn, ppg, n_groups):
    kpages = refs[:ppg]
    vpages = refs[ppg:2 * ppg]
    qa_ref, kn_ref, vn_ref, sc_ref, scn_ref, thr_ref, o_ref, qg, m_scr, l_scr, acc_scr = refs[2 * ppg:]
    c = pl.program_id(1)

    @pl.when(c == 0)
    def _():
        qa = qa_ref[...]
        for g in range(N_KV_A):
            qg[g, 0:tn, :] = qa[:, (2 * g) * LANES:(2 * g + 1) * LANES]
            qg[g, tn:2 * tn, :] = qa[:, (2 * g + 1) * LANES:(2 * g + 2) * LANES]
        m_scr[...] = jnp.full(m_scr.shape, NEG_BIG, f32)
        l_scr[...] = jnp.zeros(l_scr.shape, f32)
        acc_scr[...] = jnp.zeros(acc_scr.shape, f32)

    thr = thr_ref[0][:, 0:1]
    thr2 = jnp.concatenate([thr, thr], axis=0)

    def attend(ks, vs, sc):
        bias = jnp.where(jnp.concatenate([sc, sc], axis=0) >= thr2, 0.0, NEG_BIG)
        _flash_update([qg[g].astype(bf16) for g in range(N_KV_A)], ks, vs, bias, m_scr, l_scr, acc_scr, batched=True)

    group = lambda pages, g: jnp.concatenate(
        [p[pl.ds(g, PAGE_SIZE, stride=N_KV_A), :] for p in pages], axis=0).astype(bf16)
    attend([group(kpages, g) for g in range(N_KV_A)], [group(vpages, g) for g in range(N_KV_A)], sc_ref[0, 0])

    @pl.when(c == n_groups - 1)
    def _():
        zpad = jnp.zeros((LANES - tn, N_KV_A * HEAD_DIM_A), bf16)
        attend(_lane_blocks(jnp.concatenate([kn_ref[...].astype(bf16), zpad], axis=0)),
               _lane_blocks(jnp.concatenate([vn_ref[...].astype(bf16), zpad], axis=0)),
               scn_ref[0, 0][:, 0:LANES])
        for g in range(N_KV_A):
            o = acc_scr[g] / l_scr[g]
            o_ref[:, (2 * g) * LANES:(2 * g + 1) * LANES] = o[0:tn]
            o_ref[:, (2 * g + 1) * LANES:(2 * g + 2) * LANES] = o[tn:2 * tn]


def _dsa_sample(u_f, row0, bd, tn, cache_k, cache_v, idx_kt, page_table):
    n_pages = page_table.shape[1]
    past = n_pages * PAGE_SIZE
    topk = min(TOPK_MAX, (past + tn) // 4)
    ppa = _pick(n_pages, (16, 8, 4, 2, 1))
    pps = _pick(n_pages, (DSA_SCORE_CHUNK_PAGES,))
    nga, ngs = n_pages // ppa, n_pages // pps
    kwa, kws = ppa * PAGE_SIZE, pps * PAGE_SIZE
    per = kws // kwa
    wkv = N_KV_A * HEAD_DIM_A
    wa = N_HEADS_A * HEAD_DIM_A
    rb0 = row0 // tn

    def kv_spec(j):
        return pl.BlockSpec((None, PAGE_SIZE * N_KV_A, HEAD_DIM_A), lambda b, c, pt, j=j: (pt[b, c * ppa + j], 0, 0))

    score = pl.pallas_call(
        functools.partial(_dsa_sample_score_kernel, tn=tn, n_pages=n_pages),
        grid_spec=pltpu.PrefetchScalarGridSpec(
            num_scalar_prefetch=1,
            grid=(bd,),
            in_specs=[
                pl.BlockSpec(memory_space=pl.ANY),
                pl.BlockSpec((tn, IDX_HEADS * IDX_DIM), lambda b, pt: (rb0 + b, C_IQ // (IDX_HEADS * IDX_DIM))),
                pl.BlockSpec((tn, LANES), lambda b, pt: (rb0 + b, C_TAIL // LANES)),
            ],
            out_specs=pl.BlockSpec((1, ngs + 1, tn, kws), lambda b, pt: (b, 0, 0, 0)),
            scratch_shapes=[
                pltpu.VMEM((2, n_pages, IDX_DIM, PAGE_SIZE), f32),
                pltpu.VMEM((IDX_HEADS * tn, IDX_DIM), f32),
                pltpu.VMEM((IDX_HEADS * tn, LANES), f32),
                pltpu.SemaphoreType.DMA((2,)),
            ],
        ),
        out_shape=jax.ShapeDtypeStruct((bd, ngs + 1, tn, kws), f32),
        compiler_params=_cparams(("arbitrary",)),
        name="dsa_sample_score",
    )
    scores = score(page_table, idx_kt, u_f, u_f)

    nst = _pick(bd, (8, 4, 2, 1))
    thr = pl.pallas_call(
        functools.partial(_dsa_sample_thr_kernel, ns=nst, nc=ngs + 1, tn=tn, kw=kws, topk=topk),
        grid=(bd // nst,),
        in_specs=[pl.BlockSpec((nst, ngs + 1, tn, kws), lambda i: (i, 0, 0, 0))],
        out_specs=pl.BlockSpec((nst, tn, LANES), lambda i: (i, 0, 0)),
        out_shape=jax.ShapeDtypeStruct((bd, tn, LANES), f32),
        compiler_params=_cparams(("arbitrary",)),
        name="dsa_sample_thr",
    )(scores)

    attn = pl.pallas_call(
        functools.partial(_dsa_sample_attn_kernel, tn=tn, ppg=ppa, n_groups=nga),
        grid_spec=pltpu.PrefetchScalarGridSpec(
            num_scalar_prefetch=1,
            grid=(bd, nga),
            in_specs=[kv_spec(j) for j in range(ppa)] + [kv_spec(j) for j in range(ppa)] + [
                pl.BlockSpec((tn, wa), lambda b, c, pt: (rb0 + b, C_QA // wa)),
                pl.BlockSpec((tn, wkv), lambda b, c, pt: (rb0 + b, C_KA // wkv)),
                pl.BlockSpec((tn, wkv), lambda b, c, pt: (rb0 + b, C_VA // wkv)),
                pl.BlockSpec((1, 1, tn, kwa), lambda b, c, pt: (b, c // per, 0, c % per)),
                pl.BlockSpec((1, 1, tn, kwa), lambda b, c, pt: (b, ngs, 0, 0)),
                pl.BlockSpec((1, tn, LANES), lambda b, c, pt: (b, 0, 0)),
            ],
            out_specs=pl.BlockSpec((tn, wa), lambda b, c, pt: (b, 0)),
            scratch_shapes=[
                pltpu.VMEM((N_KV_A, 2 * tn, LANES), f32),
                pltpu.VMEM((N_KV_A, 2 * tn, 1), f32),
                pltpu.VMEM((N_KV_A, 2 * tn, 1), f32),
                pltpu.VMEM((N_KV_A, 2 * tn, LANES), f32),
            ],
        ),
        out_shape=jax.ShapeDtypeStruct((bd * tn, wa), f32),
        compiler_params=_cparams(("arbitrary", "arbitrary")),
        name="dsa_sample_attn",
    )
    return attn(page_table, *([cache_k] * ppa), *([cache_v] * ppa), u_f, u_f, u_f, scores, scores, thr)


def _ret_kernel(q_ref, k_ref, v_ref, g_ref, gnw_ref, dmask_ref, qdec_ref, kdec_ref, sdec_ref, s0_ref,
                o_ref, sout_ref, s_scr, *, n_chunks):
    c = pl.program_id(1)

    @pl.when(c == 0)
    def _():
        s_scr[...] = s0_ref[0]

    for h in range(N_HEADS_B):
        q = q_ref[:, h * QK_DIM_B:(h + 1) * QK_DIM_B]
        k = k_ref[:, h * QK_DIM_B:(h + 1) * QK_DIM_B]
        v = v_ref[:, h * V_DIM_B:(h + 1) * V_DIM_B].astype(bf16)
        gate = g_ref[:, h * V_DIM_B:(h + 1) * V_DIM_B]
        qb = q.astype(bf16)
        att = lax.dot_general(qb, k.astype(bf16), NT_DIMS, preferred_element_type=f32) * dmask_ref[h]
        s_old = s_scr[h]
        o = (jnp.dot(att.astype(bf16), v, preferred_element_type=f32)
             + jnp.dot(qb, s_old.astype(bf16), preferred_element_type=f32) * qdec_ref[h])
        kd = (k * kdec_ref[h]).astype(bf16)
        s_scr[h] = s_old * sdec_ref[h] + lax.dot_general(kd, v, TN_DIMS, preferred_element_type=f32)
        mu = jnp.mean(o, axis=-1, keepdims=True)
        var = jnp.mean(jnp.square(o - mu), axis=-1, keepdims=True)
        rb = (o - mu) * lax.rsqrt(var + LN_EPS) * gnw_ref[:, h * V_DIM_B:(h + 1) * V_DIM_B]
        rb = rb * (gate / (1.0 + jnp.exp(-gate)))
        o_ref[:, h * V_DIM_B:(h + 1) * V_DIM_B] = rb.astype(o_ref.dtype)

    @pl.when(c == n_chunks - 1)
    def _():
        sout_ref[0] = s_scr[...]


def _retention(u_f, row0, nb, t, state0, gn_w, out_dtype):
    ch = min(RET_CHUNK, t)
    if t % ch:
        ch = t
    n = t // ch
    hb = N_HEADS_B
    lg = jnp.log1p(-jnp.exp2(-5.0 - jnp.arange(hb, dtype=f32)))
    i = jnp.arange(ch)
    diff = i[:, None] - i[None, :]
    dmask = jnp.where(diff >= 0, jnp.exp(lg[:, None, None] * jnp.maximum(diff, 0)), 0.0)
    qdec = jnp.broadcast_to(jnp.exp(lg[:, None] * (i + 1))[:, :, None], (hb, ch, V_DIM_B))
    kdec = jnp.broadcast_to(jnp.exp(lg[:, None] * (ch - 1 - i))[:, :, None], (hb, ch, QK_DIM_B))
    sdec = jnp.broadcast_to(jnp.exp(lg * ch)[:, None, None], (hb, 1, V_DIM_B))
    wqk = hb * QK_DIM_B
    wv = hb * V_DIM_B
    rb0 = row0 // ch
    full3 = lambda shp: pl.BlockSpec(shp, lambda b, c: (0, 0, 0))
    return pl.pallas_call(
        functools.partial(_ret_kernel, n_chunks=n),
        grid=(nb, n),
        in_specs=[
            pl.BlockSpec((ch, wqk), lambda b, c: (rb0 + b * n + c, C_QB // wqk)),
            pl.BlockSpec((ch, wqk), lambda b, c: (rb0 + b * n + c, C_KB // wqk)),
            pl.BlockSpec((ch, wv), lambda b, c: (rb0 + b * n + c, C_VB // wv)),
            pl.BlockSpec((ch, wv), lambda b, c: (rb0 + b * n + c, C_GB // wv)),
            pl.BlockSpec((1, wv), lambda b, c: (0, 0)),
            full3((hb, ch, ch)), full3((hb, ch, V_DIM_B)), full3((hb, ch, QK_DIM_B)), full3((hb, 1, V_DIM_B)),
            pl.BlockSpec((1, hb, QK_DIM_B, V_DIM_B), lambda b, c: (b, 0, 0, 0)),
        ],
        out_specs=[
            pl.BlockSpec((ch, wv), lambda b, c: (b * n + c, 0)),
            pl.BlockSpec((1, hb, QK_DIM_B, V_DIM_B), lambda b, c: (b, 0, 0, 0)),
        ],
        out_shape=[jax.ShapeDtypeStruct((nb * t, wv), out_dtype),
                   jax.ShapeDtypeStruct((nb, hb, QK_DIM_B, V_DIM_B), f32)],
        scratch_shapes=[pltpu.VMEM((hb, QK_DIM_B, V_DIM_B), f32)],
        compiler_params=_cparams(("arbitrary", "arbitrary")),
        name="retention",
    )(u_f, u_f, u_f, u_f, gn_w.reshape(1, wv), dmask, qdec, kdec, sdec, state0)


def _split_hi_lo(a):
    hi = a.astype(bf16)
    return hi, (a - hi.astype(f32)).astype(bf16)


def _tail1_kernel(attn_p_ref, attn_s_ref, rb_p_ref, rb_s_ref, x_ref, wo_ref, g1_ref, b1_ref, wrh_ref, wrl_ref, br_ref,
                  h_ref, hq_ref, eid_ref, gate_ref, *, tm, n_exp, alpha, wa, nbp):
    is_prompt = pl.program_id(0) < nbp
    attn = jnp.where(is_prompt, attn_p_ref[...], attn_s_ref[...].astype(bf16))
    rb = jnp.where(is_prompt, rb_p_ref[...], rb_s_ref[...].astype(bf16))
    mix = (jnp.dot(attn, wo_ref[0:wa, :], preferred_element_type=f32)
           + jnp.dot(rb, wo_ref[wa:, :], preferred_element_type=f32))
    z = alpha * x_ref[...] + mix
    mu = jnp.mean(z, axis=-1, keepdims=True)
    var = jnp.mean(jnp.square(z - mu), axis=-1, keepdims=True)
    h = (z - mu) * lax.rsqrt(var + LN_EPS) * g1_ref[...] + b1_ref[...]
    h_ref[...] = h
    nq = h.shape[1] // LANES
    for j in range(TOK_PITCH):
        slab = h[:, j * LANES:(j + 1) * LANES] if j < nq else jnp.zeros((tm, LANES), f32)
        hq_ref[pl.ds(j, tm, stride=TOK_PITCH), :] = slab

    hh, hl = _split_hi_lo(h)
    logits = (jnp.dot(hh, wrh_ref[...], preferred_element_type=f32)
              + jnp.dot(hl, wrh_ref[...], preferred_element_type=f32)
              + jnp.dot(hh, wrl_ref[...], preferred_element_type=f32)) + br_ref[...]
    lane = lax.broadcasted_iota(i32, (tm, LANES), 1)
    lanef = lane.astype(f32)
    logits = jnp.where(lane < n_exp, logits, -jnp.inf)
    vals, ids = [], []
    for _ in range(TOP_K):
        m = jnp.max(logits, axis=1, keepdims=True)
        idx = jnp.min(jnp.where(logits == m, lanef, float(LANES)), axis=1, keepdims=True)
        vals.append(m)
        ids.append(idx)
        logits = jnp.where(lanef == idx, -jnp.inf, logits)
    es = [jnp.exp(v - vals[0]) for v in vals]
    den = es[0] + es[1] + es[2] + es[3]
    eid_ref[...] = jnp.concatenate(ids, axis=1).astype(i32)
    gate_ref[...] = jnp.concatenate([e / den for e in es], axis=1)


def _tail1(attn_p, attn_s, rb_p, rb_s, x_all, w_o, ln1_g, ln1_b, w_router, b_router, alpha):
    n, d = x_all.shape
    wa, wb = attn_p.shape[1], rb_p.shape[1]
    n_prompt = attn_p.shape[0]
    n_exp = w_router.shape[1]
    tm = _pick(np.gcd(n_prompt, n - n_prompt), (256, 128, 64, 32, 16, 8))
    nbp = n_prompt // tm
    prompt_rows = lambda w: pl.BlockSpec((tm, w), lambda i: (jnp.minimum(i, nbp - 1), 0))
    sample_rows = lambda w: pl.BlockSpec((tm, w), lambda i: (jnp.maximum(i - nbp, 0), 0))
    nq = d // LANES
    wr = jnp.zeros((d, LANES), f32).at[:, :n_exp].set(w_router)
    wrh, wrl = _split_hi_lo(wr)
    br = jnp.zeros((1, LANES), f32).at[0, :n_exp].set(b_router)
    row = lambda w: pl.BlockSpec((1, w), lambda i: (0, 0))
    return pl.pallas_call(
        functools.partial(_tail1_kernel, tm=tm, n_exp=n_exp, alpha=alpha, wa=wa, nbp=nbp),
        grid=(n // tm,),
        in_specs=[
            prompt_rows(wa), sample_rows(wa), prompt_rows(wb), sample_rows(wb),
            pl.BlockSpec((tm, d), lambda i: (i, 0)),
            pl.BlockSpec(w_o.shape, lambda i: (0, 0), pipeline_mode=pl.Buffered(1)),
            row(d), row(d),
            pl.BlockSpec((d, LANES), lambda i: (0, 0)), pl.BlockSpec((d, LANES), lambda i: (0, 0)), row(LANES),
        ],
        out_specs=[
            pl.BlockSpec((tm, d), lambda i: (i, 0)),
            pl.BlockSpec((tm * TOK_PITCH, LANES), lambda i: (i, 0)),
            pl.BlockSpec((tm, TOP_K), lambda i: (i, 0)),
            pl.BlockSpec((tm, TOP_K), lambda i: (i, 0)),
        ],
        out_shape=[jax.ShapeDtypeStruct((n, d), f32), jax.ShapeDtypeStruct((n * TOK_PITCH, LANES), f32),
                   jax.ShapeDtypeStruct((n, TOP_K), i32), jax.ShapeDtypeStruct((n, TOP_K), f32)],
        compiler_params=_cparams(("arbitrary",)),
        name="tail1",
    )(attn_p, attn_s, rb_p, rb_s, x_all, w_o.astype(bf16), ln1_g.reshape(1, d), ln1_b.reshape(1, d), wrh, wrl, br)


MOE_R = 2048
MOE_SUB = 256
MOE_TF = 256
MOE_ISSUE_UNROLL = 8
MOE_TILES = (512, 256, 128)


def _moe_plan(eid, n_exp, r_cap):
    n = eid.shape[0]
    p = n * TOP_K
    flat = eid.reshape(p)
    onehot = (flat[:, None] == jnp.arange(n_exp, dtype=i32)[None, :]).astype(i32)
    csum = jnp.cumsum(onehot, axis=0)
    rank = jnp.sum((csum - onehot) * onehot, axis=1)
    counts = csum[-1]
    ngrp = (counts + r_cap - 1) // r_cap
    gend = jnp.cumsum(ngrp)
    gstart = gend - ngrp
    g_of = gstart[flat] + rank // r_cap
    slot = rank % r_cap
    n_groups = n_exp + p // r_cap
    pair = jnp.full((n_groups, r_cap), -1, i32).at[g_of, slot].set(jnp.arange(p, dtype=i32))
    t_idx, k_idx = pair // TOP_K, pair % TOP_K
    spare = jnp.broadcast_to(p + jnp.arange(r_cap, dtype=i32) % MOE_SUB, (n_groups, r_cap))
    tok = jnp.where(pair >= 0, t_idx, 0)
    dst = jnp.where(pair >= 0, k_idx * n + t_idx, spare)
    gid = jnp.arange(n_groups, dtype=i32)
    total = gend[-1]
    gclamp = jnp.minimum(gid, total - 1)
    g_exp = jnp.sum((gend[None, :] <= gclamp[:, None]).astype(i32), axis=1)
    g_rows = jnp.clip(counts[g_exp] - (gclamp - gstart[g_exp]) * r_cap, 0, r_cap)
    g_rows = jnp.where(gid < total, g_rows, 0).astype(i32)
    return g_exp, g_rows, tok.reshape(n_groups, 1, r_cap), dst.reshape(n_groups, 1, r_cap)


def _moe_kernel(ge_ref, gr_ref, tok_ref, dst_ref, hq_ref, wgu_ref, wd_ref, bgu_ref, bd_ref, pm_ref, y_ref,
                qbuf, xb, acc, wgu_b, wd_b, sem_in, sem_out, *, nq, nj, spare_row0):
    pitch = TOK_PITCH
    g = pl.program_id(0)
    j = pl.program_id(1)
    rows = gr_ref[g]
    nsub = (rows + MOE_SUB - 1) // MOE_SUB
    active = rows > 0

    def in_copy(s, i, slot):
        t = tok_ref[0, 0, s * MOE_SUB + i]
        return pltpu.make_async_copy(hq_ref.at[pl.ds(pl.multiple_of(t * pitch, pitch), pitch), :],
                                     qbuf.at[slot, pl.ds(pl.multiple_of(i * pitch, pitch), pitch), :], sem_in.at[slot])

    def out_copy(s, i, slot):
        d = dst_ref[0, 0, s * MOE_SUB + i]
        return pltpu.make_async_copy(qbuf.at[slot, pl.ds(pl.multiple_of(i * pitch, 8), nq), :],
                                     y_ref.at[pl.ds(pl.multiple_of(d * nq, nq), nq), :], sem_out.at[slot])

    def for_rows(fn):
        def body(b, c):
            for u in range(MOE_ISSUE_UNROLL):
                fn(b * MOE_ISSUE_UNROLL + u)
            return c

        lax.fori_loop(0, MOE_SUB // MOE_ISSUE_UNROLL, body, 0)

    def wait_in(slot):
        pltpu.make_async_copy(hq_ref.at[pl.ds(0, MOE_SUB * pitch), :], qbuf.at[slot], sem_in.at[slot]).wait()

    def wait_out(slot):
        pltpu.make_async_copy(qbuf.at[slot, pl.ds(0, MOE_SUB * nq)], y_ref.at[pl.ds(0, MOE_SUB * nq), :],
                              sem_out.at[slot]).wait()

    def convert(s, slot):
        r0 = pl.multiple_of(s * MOE_SUB, MOE_SUB)
        for jj in range(nq):
            xb[pl.ds(r0, MOE_SUB), jj * LANES:(jj + 1) * LANES] = (
                qbuf[slot, pl.ds(jj, MOE_SUB, stride=pitch), :].astype(bf16))
        acc[pl.ds(r0, MOE_SUB), :] = jnp.zeros((MOE_SUB, acc.shape[1]), f32)

    def stage(s, slot):
        r0 = pl.multiple_of(s * MOE_SUB, MOE_SUB)
        a = acc[pl.ds(r0, MOE_SUB), :] + bd_ref[0]
        for jj in range(nq):
            qbuf[slot, pl.ds(jj, MOE_SUB, stride=pitch), :] = a[:, jj * LANES:(jj + 1) * LANES]

    npairs = (nsub + 1) // 2

    @pl.when((g == 0) & (j == 0))
    def _():
        qbuf[1] = jnp.zeros(qbuf.shape[1:], f32)
        fill = pltpu.make_async_copy(qbuf.at[1, pl.ds(0, MOE_SUB * nq)], y_ref.at[pl.ds(spare_row0, MOE_SUB * nq), :],
                                     sem_out.at[1])
        fill.start()
        fill.wait()

    def tile(r0, size):
        gu = jnp.dot(xb[pl.ds(r0, size), :], wgu_b[...], preferred_element_type=f32) + bgu_ref[0]
        gub = gu.astype(bf16)
        gates, ups = [], []
        for q in range(2 * MOE_TF // 256):
            de = jnp.dot(gub[:, q * 256:(q + 1) * 256], pm_ref[...], preferred_element_type=f32)
            gates.append(de[:, 0:LANES])
            ups.append(de[:, LANES:2 * LANES])
        gate = jnp.minimum(jnp.concatenate(gates, axis=1), SWIGLU_LIMIT)
        up = jnp.clip(jnp.concatenate(ups, axis=1), -SWIGLU_LIMIT, SWIGLU_LIMIT)
        act = (up + 1.0) * gate * (1.0 / (1.0 + jnp.exp(-SWIGLU_ALPHA * gate)))
        acc[pl.ds(r0, size), :] += jnp.dot(act.astype(bf16), wd_b[...], preferred_element_type=f32)

    first = j == 0
    last = j == nj - 1
    pair_rows = 2 * MOE_SUB
    assert MOE_TILES == (pair_rows, MOE_SUB, MOE_SUB // 2)
    padded = (rows + MOE_TILES[-1] - 1) // MOE_TILES[-1] * MOE_TILES[-1]

    @pl.when(active)
    def _():
        wgu_b[...] = wgu_ref[0].astype(bf16)
        wd_b[...] = wd_ref[0].astype(bf16)

    @pl.when(active & first)
    def _():
        for_rows(lambda i: in_copy(0, i, 0).start())

        @pl.when(nsub > 1)
        def _():
            for_rows(lambda i: in_copy(1, i, 1).start())

    def pair(pp, c):
        s0 = 2 * pp

        @pl.when(first)
        def _():
            for slot in (0, 1):
                s = s0 + slot

                @pl.when(s < nsub)
                def _(s=s, slot=slot):
                    wait_in(slot)
                    convert(s, slot)

                    @pl.when(s + 2 < nsub)
                    def _():
                        for_rows(lambda i: in_copy(s + 2, i, slot).start())

        r0 = pl.multiple_of(s0 * MOE_SUB, pair_rows)
        left = jnp.minimum(padded - r0, pair_rows)

        @pl.when(left == pair_rows)
        def _():
            tile(r0, pair_rows)

        @pl.when((left >= MOE_SUB) & (left < pair_rows))
        def _():
            tile(r0, MOE_SUB)

        @pl.when(left % MOE_SUB != 0)
        def _():
            tile(pl.multiple_of(r0 + left // MOE_SUB * MOE_SUB, MOE_TILES[-1]), MOE_TILES[-1])

        @pl.when(last)
        def _():
            for slot in (0, 1):
                s = s0 + slot

                @pl.when(s < nsub)
                def _(s=s, slot=slot):
                    @pl.when(pp > 0)
                    def _():
                        wait_out(slot)

                    stage(s, slot)
                    for_rows(lambda i: out_copy(s, i, slot).start())

        return c

    @pl.when(active)
    def _():
        lax.fori_loop(0, npairs, pair, 0)

    @pl.when(active & last)
    def _():
        wait_out(0)

        @pl.when(nsub >= 2)
        def _():
            wait_out(1)


def _deinterleave_matrix():
    pm = np.zeros((256, 256), np.float32)
    i = np.arange(LANES)
    pm[2 * i, i] = 1.0
    pm[2 * i + 1, LANES + i] = 1.0
    return jnp.asarray(pm, bf16)


def _moe(hq, eid, w_gate_up, b_gate_up, w_down, b_down, n):
    n_exp, d, f2 = w_gate_up.shape
    dff = f2 // 2
    nq = d // LANES
    nj = dff // MOE_TF
    assert nj >= 2, "the first and last hidden chunks carry the row gather and scatter"
    g_exp, g_rows, tok, dst = _moe_plan(eid, n_exp, MOE_R)
    n_groups = g_exp.shape[0]

    def jeff(g, j, gr):
        return jnp.where(gr[g] > 0, j, nj - 1)

    return pl.pallas_call(
        functools.partial(_moe_kernel, nq=nq, nj=nj, spare_row0=TOP_K * n * nq),
        grid_spec=pltpu.PrefetchScalarGridSpec(
            num_scalar_prefetch=2,
            grid=(n_groups, nj),
            in_specs=[
                pl.BlockSpec((1, 1, MOE_R), lambda g, j, ge, gr: (g, 0, 0), memory_space=pltpu.SMEM),
                pl.BlockSpec((1, 1, MOE_R), lambda g, j, ge, gr: (g, 0, 0), memory_space=pltpu.SMEM),
                pl.BlockSpec(memory_space=pl.ANY),
                pl.BlockSpec((1, d, 2 * MOE_TF), lambda g, j, ge, gr: (ge[g], 0, jeff(g, j, gr))),
                pl.BlockSpec((1, MOE_TF, d), lambda g, j, ge, gr: (ge[g], jeff(g, j, gr), 0)),
                pl.BlockSpec((1, 1, 2 * MOE_TF), lambda g, j, ge, gr: (ge[g], 0, jeff(g, j, gr))),
                pl.BlockSpec((1, 1, d), lambda g, j, ge, gr: (ge[g], 0, 0)),
                pl.BlockSpec((256, 256), lambda g, j, ge, gr: (0, 0)),
            ],
            out_specs=pl.BlockSpec(memory_space=pl.ANY),
            scratch_shapes=[
                pltpu.VMEM((2, MOE_SUB * TOK_PITCH, LANES), f32),
                pltpu.VMEM((MOE_R, d), bf16),
                pltpu.VMEM((MOE_R, d), f32),
                pltpu.VMEM((d, 2 * MOE_TF), bf16),
                pltpu.VMEM((MOE_TF, d), bf16),
                pltpu.SemaphoreType.DMA((2,)),
                pltpu.SemaphoreType.DMA((2,)),
            ],
        ),
        out_shape=jax.ShapeDtypeStruct(((TOP_K * n + MOE_SUB) * nq, LANES), f32),
        compiler_params=_cparams(("arbitrary", "arbitrary")),
        name="moe",
    )(g_exp, g_rows, tok, dst, hq, w_gate_up, w_down, b_gate_up.reshape(n_exp, 1, f2),
      b_down.reshape(n_exp, 1, d), _deinterleave_matrix())


def _final_kernel(h_ref, y0_ref, y1_ref, y2_ref, y3_ref, gate_ref, g2_ref, b2_ref, op_ref, os_ref, *, tm, nq, alpha, nbp):
    gates = gate_ref[...]
    f = jnp.zeros(h_ref.shape, f32)
    for k, y_ref in enumerate((y0_ref, y1_ref, y2_ref, y3_ref)):
        yk = jnp.concatenate([y_ref[pl.ds(jj, tm, stride=nq), :] for jj in range(nq)], axis=1)
        f = f + gates[:, k:k + 1] * yk
    z = alpha * h_ref[...] + f
    mu = jnp.mean(z, axis=-1, keepdims=True)
    var = jnp.mean(jnp.square(z - mu), axis=-1, keepdims=True)
    y = (z - mu) * lax.rsqrt(var + LN_EPS) * g2_ref[...] + b2_ref[...]
    i = pl.program_id(0)

    @pl.when(i < nbp)
    def _():
        op_ref[...] = y

    @pl.when(i >= nbp)
    def _():
        os_ref[...] = y


def _final(h, y4q, gates, ln2_g, ln2_b, alpha, n_prompt):
    n, d = h.shape
    nq = d // LANES
    tm = _pick(np.gcd(n_prompt, n - n_prompt), (256, 128, 64, 32, 16, 8))
    nb, nbp = n // tm, n_prompt // tm
    row = pl.BlockSpec((1, d), lambda i: (0, 0))
    yspec = lambda k: pl.BlockSpec((tm * nq, LANES), lambda i, k=k: (k * nb + i, 0))
    return pl.pallas_call(
        functools.partial(_final_kernel, tm=tm, nq=nq, alpha=alpha, nbp=nbp),
        grid=(nb,),
        in_specs=[pl.BlockSpec((tm, d), lambda i: (i, 0)), yspec(0), yspec(1), yspec(2), yspec(3),
                  pl.BlockSpec((tm, TOP_K), lambda i: (i, 0)), row, row],
        out_specs=[pl.BlockSpec((tm, d), lambda i: (jnp.minimum(i, nbp - 1), 0)),
                   pl.BlockSpec((tm, d), lambda i: (jnp.maximum(i - nbp, 0), 0))],
        out_shape=[jax.ShapeDtypeStruct((n_prompt, d), f32), jax.ShapeDtypeStruct((n - n_prompt, d), f32)],
        compiler_params=_cparams(("arbitrary",)),
        name="final",
    )(h, y4q, y4q, y4q, y4q, gates, ln2_g.reshape(1, d), ln2_b.reshape(1, d))


def kernel(x_prompt, x_sample, cache_k, cache_v, cache_idx_k, state_ret, page_table, w_in, w_o, ret_gn_w,
           ln1_g, ln1_b, w_router, b_router, w_gate_up, b_gate_up, w_down, b_down, ln2_g, ln2_b):
    depth = w_in.shape[0]
    assert depth == 1, "single-layer step"
    bp, t, d = x_prompt.shape
    bd, tn, _ = x_sample.shape
    assert bp == 1
    past = page_table.shape[1] * PAGE_SIZE
    np_, ns = bp * t, bd * tn
    n = np_ + ns
    alpha = (2 * depth) ** 0.25
    wkv = N_KV_A * HEAD_DIM_A
    layer = lambda a: a.reshape(a.shape[1:])

    x_all = jnp.concatenate([x_prompt.reshape(np_, d), x_sample.reshape(ns, d)], axis=0)
    pos_all = jnp.concatenate([jnp.arange(t), jnp.tile(past + jnp.arange(tn), bd)])
    u_f, u_b, k_rows, v_rows = _project(x_all, pos_all, layer(w_in))

    attn_p = _dsa_prompt(u_f, u_b, t)
    pages = lambda a: a.reshape(a.shape[1], PAGE_SIZE * N_KV_A, HEAD_DIM_A)
    attn_s = _dsa_sample(u_f, np_, bd, tn, pages(cache_k), pages(cache_v), jnp.swapaxes(layer(cache_idx_k), 1, 2),
                         page_table)

    zero_state = jnp.zeros((bp, N_HEADS_B, QK_DIM_B, V_DIM_B), f32)
    rb_p, s_p = _retention(u_f, 0, bp, t, zero_state, layer(ret_gn_w), bf16)
    rb_s, s_s = _retention(u_f, np_, bd, tn, layer(state_ret), layer(ret_gn_w), f32)

    h, hq, eid, gates = _tail1(attn_p, attn_s, rb_p, rb_s, x_all, layer(w_o), layer(ln1_g), layer(ln1_b), layer(w_router),
                               layer(b_router), alpha)
    y4q = _moe(hq, eid, layer(w_gate_up), layer(b_gate_up), layer(w_down), layer(b_down), n)
    y_p, y_s = _final(h, y4q, gates, layer(ln2_g), layer(ln2_b), alpha, np_)

    split = np_ * N_KV_A
    kv_p = lambda a: a[:split].reshape(1, bp, t, N_KV_A, HEAD_DIM_A)
    kv_s = lambda a: a[split:].reshape(1, bd, tn, N_KV_A, HEAD_DIM_A)
    ps, ss = slice(0, np_), slice(np_, n)
    return (
        y_p.reshape(bp, t, d), y_s.reshape(bd, tn, d),
        kv_p(k_rows), kv_p(v_rows), u_f[ps, C_TAIL:C_TAIL + IDX_DIM].reshape(1, bp, t, IDX_DIM),
        s_p[None],
        kv_s(k_rows), kv_s(v_rows), u_f[ss, C_TAIL:C_TAIL + IDX_DIM].reshape(1, bd, tn, IDX_DIM),
        s_s[None],
    )
```
